```python
import math
import jax, jax.numpy as jnp
from jax import lax
import numpy as np

D_MODEL = 1024
BATCH = 4
SEQ = 8192
DEPTH = 4
DEC_BATCH = 32
DEC_SEQ = 64
PAST_LEN = 4096

CHUNK = 64
N_MIXERS = 2
N_MLA_LAYERS = (DEPTH + 1) // 2
N_GDN_LAYERS = DEPTH // 2
MLA_HEADS = 16
MLA_Q_LORA = 768
MLA_KV_LORA = 256
MLA_NOPE = 64
MLA_ROPE = 32
MLA_V = 64
MLA_QK = MLA_NOPE + MLA_ROPE
MLA_IN_DIM = MLA_Q_LORA + MLA_KV_LORA + MLA_ROPE
MLA_SCALE = MLA_QK ** -0.5
ROPE_BASE = 10000.0
ATTN_QBLOCK = 128
GDN_QK_HEADS = 8
GDN_V_HEADS = 16
GDN_DK = 128
GDN_DV = 128
GDN_QK_DIM = GDN_QK_HEADS * GDN_DK
GDN_V_DIM = GDN_V_HEADS * GDN_DV
GDN_CONV_DIM = 2 * GDN_QK_DIM + GDN_V_DIM
GDN_IN_DIM = GDN_CONV_DIM + GDN_V_DIM + 2 * GDN_V_HEADS
GDN_CONV_W = 4
GDN_CHUNK = 64
N_EXPERTS = 32
TOP_K = 4
D_FF = D_MODEL
SWIGLU_ALPHA = 1.702
SWIGLU_LIMIT = 7.0
MOE_BLOCK = 128
DEEPNORM_ALPHA = (2 * DEPTH) ** 0.25
DEEPNORM_BETA = (8 * DEPTH) ** -0.25
LN_EPS = 1e-5
RMS_EPS = 1e-6
L2_EPS = 1e-6

kernel_name = "hybrid_mla_gdn_moe_stream_step"

F32 = jnp.float32


def _layernorm(x, g, b):
    xf = x.astype(F32)
    mu = jnp.mean(xf, -1, keepdims=True)
    var = jnp.mean(jnp.square(xf - mu), -1, keepdims=True)
    return ((xf - mu) * lax.rsqrt(var + LN_EPS) * g.astype(F32) + b.astype(F32)).astype(x.dtype)


def _rmsnorm(x, g):
    xf = x.astype(F32)
    y = xf * lax.rsqrt(jnp.mean(jnp.square(xf), -1, keepdims=True) + RMS_EPS)
    return (y * g.astype(F32)).astype(x.dtype)


def _l2norm(x):
    xf = x.astype(F32)
    return xf * lax.rsqrt(jnp.sum(jnp.square(xf), -1, keepdims=True) + L2_EPS)


def _rope(x, pos):
    half = MLA_ROPE // 2
    inv_freq = ROPE_BASE ** (-jnp.arange(half, dtype=F32) / half)
    ang = pos.astype(F32)[:, None] * inv_freq[None, :]
    cos = jnp.cos(ang)[None, :, None, :]
    sin = jnp.sin(ang)[None, :, None, :]
    xf = x.astype(F32)
    x1, x2 = xf[..., :half], xf[..., half:]
    return jnp.concatenate([x1 * cos - x2 * sin, x2 * cos + x1 * sin], -1).astype(x.dtype)


def _chunk_attend(q, k, v, q_pos, k_pos):
    s = jnp.einsum('bqhd,bkhd->bhqk', q, k, preferred_element_type=F32) * MLA_SCALE
    visible = (k_pos[None, :] // CHUNK) <= (q_pos[:, None] // CHUNK)
    s = jnp.where(visible[None, None], s, -jnp.inf)
    p = jax.nn.softmax(s, axis=-1).astype(v.dtype)
    return jnp.einsum('bhqk,bkhv->bqhv', p, v)


def _mla_mixer(h, pos, past_lat, past_kr, w_in, q_norm, kv_norm, w_uq, w_uk, w_uv, w_o):
    B, T, _ = h.shape
    down = h @ w_in
    c_q = _rmsnorm(down[..., :MLA_Q_LORA], q_norm)
    lat = _rmsnorm(down[..., MLA_Q_LORA:MLA_Q_LORA + MLA_KV_LORA], kv_norm)
    kr = _rope(down[..., MLA_Q_LORA + MLA_KV_LORA:][:, :, None, :], pos)[:, :, 0]
    q = (c_q @ w_uq).reshape(B, T, MLA_HEADS, MLA_QK)
    q = jnp.concatenate([q[..., :MLA_NOPE], _rope(q[..., MLA_NOPE:], pos)], -1)
    if past_lat is None:
        lat_all, kr_all, k_pos = lat, kr, pos
    else:
        lat_all = jnp.concatenate([past_lat.astype(lat.dtype), lat], 1)
        kr_all = jnp.concatenate([past_kr.astype(kr.dtype), kr], 1)
        k_pos = jnp.arange(lat_all.shape[1], dtype=jnp.int32)
    Tk = lat_all.shape[1]
    k_nope = (lat_all @ w_uk).reshape(B, Tk, MLA_HEADS, MLA_NOPE)
    v = (lat_all @ w_uv).reshape(B, Tk, MLA_HEADS, MLA_V)
    k = jnp.concatenate([k_nope, jnp.broadcast_to(kr_all[:, :, None, :], (B, Tk, MLA_HEADS, MLA_ROPE))], -1)
    if T <= ATTN_QBLOCK:
        ctx = _chunk_attend(q, k, v, pos, k_pos)
    else:
        nb = T // ATTN_QBLOCK
        qb = jnp.moveaxis(q.reshape(B, nb, ATTN_QBLOCK, MLA_HEADS, MLA_QK), 1, 0)
        pb = pos.reshape(nb, ATTN_QBLOCK)
        ctx = lax.map(lambda a: _chunk_attend(a[0], k, v, a[1], k_pos), (qb, pb))
        ctx = jnp.moveaxis(ctx, 0, 1).reshape(B, T, MLA_HEADS, MLA_V)
    out = ctx.reshape(B, T, MLA_HEADS * MLA_V) @ w_o
    return out, lat, kr


def _causal_dwconv(xpad, w):
    return lax.conv_general_dilated(xpad, w[:, None, :].astype(xpad.dtype), (1,), 'VALID',
                                    dimension_numbers=('NWC', 'WIO', 'NWC'),
                                    feature_group_count=xpad.shape[-1])


def _gated_delta_rule(q, k, v, g, beta, s0):
    B, T, H, _ = q.shape
    C = GDN_CHUNK
    n = -(-T // C)
    pad = n * C - T

    def blocks(t):
        t = jnp.pad(t, [(0, 0), (0, pad)] + [(0, 0)] * (t.ndim - 2))
        t = t.reshape((B, n, C) + t.shape[2:])
        return jnp.moveaxis(t, [1, 3], [0, 2])

    qb, kb, vb, gb, bb = blocks(q), blocks(k), blocks(v), blocks(g), blocks(beta)
    gc = jnp.cumsum(gb, axis=-1)
    idx = jnp.arange(C)
    causal = idx[:, None] >= idx[None, :]
    strict = idx[:, None] > idx[None, :]
    decay = jnp.exp(jnp.where(causal, gc[..., :, None] - gc[..., None, :], -jnp.inf))
    kbeta = kb * bb[..., None]
    lmat = jnp.where(strict, jnp.einsum('nbhid,nbhjd->nbhij', kbeta, kb) * decay, 0.0)
    rhs = jnp.concatenate([vb * bb[..., None], kbeta * jnp.exp(gc)[..., None]], -1)
    sol = lax.linalg.triangular_solve(lmat, rhs, left_side=True, lower=True, unit_diagonal=True)
    u, w = sol[..., :GDN_DV], sol[..., GDN_DV:]
    attn = jnp.einsum('nbhid,nbhjd->nbhij', qb, kb) * decay
    q_dec = qb * jnp.exp(gc)[..., None]
    g_last = gc[..., -1:]
    k_dec = kb * jnp.exp(g_last - gc)[..., None]
    d_tot = jnp.exp(g_last[..., 0])

    def step(S, xs):
        u_c, w_c, attn_c, qd_c, kd_c, dt_c = xs
        v_new = u_c - jnp.einsum('bhck,bhkv->bhcv', w_c, S)
        o_c = jnp.einsum('bhck,bhkv->bhcv', qd_c, S) + jnp.einsum('bhij,bhjv->bhiv', attn_c, v_new)
        S = S * dt_c[..., None, None] + jnp.einsum('bhck,bhcv->bhkv', kd_c, v_new)
        return S, o_c

    s_fin, o = lax.scan(step, s0, (u, w, attn, q_dec, k_dec, d_tot))
    o = jnp.moveaxis(o, [0, 2], [1, 3]).reshape(B, n * C, H, GDN_DV)[:, :T]
    return o, s_fin


def _gdn_mixer(h, conv_state, ssm_state, w_in, conv_w, a_log, dt_bias, norm_w, w_o):
    B, T, _ = h.shape
    proj = h @ w_in
    qkv = proj[..., :GDN_CONV_DIM]
    z = proj[..., GDN_CONV_DIM:GDN_CONV_DIM + GDN_V_DIM]
    b = proj[..., GDN_CONV_DIM + GDN_V_DIM:GDN_CONV_DIM + GDN_V_DIM + GDN_V_HEADS]
    a = proj[..., GDN_CONV_DIM + GDN_V_DIM + GDN_V_HEADS:]
    if conv_state is None:
        conv_state = jnp.zeros((B, GDN_CONV_W - 1, GDN_CONV_DIM), qkv.dtype)
    if ssm_state is None:
        ssm_state = jnp.zeros((B, GDN_V_HEADS, GDN_DK, GDN_DV), F32)
    xpad = jnp.concatenate([conv_state.astype(qkv.dtype), qkv], 1)
    new_conv = xpad[:, -(GDN_CONV_W - 1):]
    qkv = jax.nn.silu(_causal_dwconv(xpad, conv_w))
    rep = GDN_V_HEADS // GDN_QK_HEADS
    q = _l2norm(qkv[..., :GDN_QK_DIM].reshape(B, T, GDN_QK_HEADS, GDN_DK)) * (GDN_DK ** -0.5)
    k = _l2norm(qkv[..., GDN_QK_DIM:2 * GDN_QK_DIM].reshape(B, T, GDN_QK_HEADS, GDN_DK))
    q = jnp.repeat(q, rep, axis=2)
    k = jnp.repeat(k, rep, axis=2)
    v = qkv[..., 2 * GDN_QK_DIM:].reshape(B, T, GDN_V_HEADS, GDN_DV).astype(F32)
    beta = jax.nn.sigmoid(b.astype(F32))
    g = -jnp.exp(a_log.astype(F32)) * jax.nn.softplus(a.astype(F32) + dt_bias.astype(F32))
    o, s_new = _gated_delta_rule(q, k, v, g, beta, ssm_state.astype(F32))
    o = _rmsnorm(o, norm_w) * jax.nn.silu(z.reshape(B, T, GDN_V_HEADS, GDN_DV).astype(F32))
    out = o.reshape(B, T, GDN_V_DIM).astype(h.dtype) @ w_o
    return out, new_conv, s_new.astype(h.dtype)


def _moe(h, w_r, b_r, w_gu, b_gu, w_down, b_down):
    B, T, D = h.shape
    xt = h.reshape(-1, D)
    N = xt.shape[0]
    logits = (xt @ w_r).astype(F32) + b_r.astype(F32)
    top_val, top_idx = lax.top_k(logits, TOP_K)
    gates = jax.nn.softmax(top_val, axis=-1)
    A = N * TOP_K
    flat_e = top_idx.reshape(-1).astype(jnp.int32)
    order = jnp.argsort(flat_e)
    e_sorted = flat_e[order]
    tok_sorted = (order // TOP_K).astype(jnp.int32)
    counts = jnp.bincount(flat_e, length=N_EXPERTS)
    padded = (counts + MOE_BLOCK - 1) // MOE_BLOCK * MOE_BLOCK
    pad_end = jnp.cumsum(padded)
    pad_start = pad_end - padded
    grp_start = jnp.cumsum(counts) - counts
    dest = pad_start[e_sorted] + jnp.arange(A, dtype=jnp.int32) - grp_start[e_sorted]
    nb = (A + N_EXPERTS * (MOE_BLOCK - 1) + MOE_BLOCK - 1) // MOE_BLOCK
    P = nb * MOE_BLOCK
    row_tok = jnp.full((P,), N, jnp.int32).at[dest].set(tok_sorted)
    x_rows = jnp.concatenate([xt, jnp.zeros((1, D), xt.dtype)], 0)[row_tok].reshape(nb, MOE_BLOCK, D)
    blk_e = jnp.minimum(jnp.searchsorted(pad_end, jnp.arange(nb, dtype=jnp.int32) * MOE_BLOCK, side='right'),
                        N_EXPERTS - 1)

    def expert_block(args):
        xb, e = args
        gu = xb @ w_gu[e] + b_gu[e]
        gate = jnp.minimum(gu[..., :D_FF], SWIGLU_LIMIT)
        up = jnp.clip(gu[..., D_FF:], -SWIGLU_LIMIT, SWIGLU_LIMIT)
        act = (up + 1.0) * gate * jax.nn.sigmoid(SWIGLU_ALPHA * gate)
        return act @ w_down[e] + b_down[e]

    y_rows = lax.map(expert_block, (x_rows, blk_e)).reshape(P, D)
    y_assign = y_rows[dest] * gates.reshape(-1)[order][:, None].astype(y_rows.dtype)
    y = jax.ops.segment_sum(y_assign, tok_sorted, num_segments=N)
    return y.reshape(B, T, D)


def _trunk(x, c, pos, past, prm):
    c_act = jax.nn.silu(c)
    lats, krs, convs, ssms = [], [], [], []
    for layer in range(DEPTH):
        mod = c_act @ prm['w_cond'][layer] + prm['b_cond'][layer]
        sh_m, sc_m, g_m, sh_f, sc_f, g_f = [m[:, None, :] for m in jnp.split(mod, 6, axis=-1)]
        h = x * (1.0 + sc_m) + sh_m
        j = layer // N_MIXERS
        if layer % N_MIXERS == 0:
            p_lat = None if past is None else past[0][j]
            p_kr = None if past is None else past[1][j]
            y, lat, kr = _mla_mixer(h, pos, p_lat, p_kr, prm['mla_w_in'][j], prm['mla_q_norm'][j],
                                    prm['mla_kv_norm'][j], prm['mla_w_uq'][j], prm['mla_w_uk'][j],
                                    prm['mla_w_uv'][j], prm['mla_w_o'][j])
            lats.append(lat)
            krs.append(kr)
        else:
            p_conv = None if past is None else past[2][j]
            p_ssm = None if past is None else past[3][j]
            y, conv, ssm = _gdn_mixer(h, p_conv, p_ssm, prm['gdn_w_in'][j], prm['gdn_conv_w'][j],
                                      prm['gdn_a_log'][j], prm['gdn_dt_bias'][j], prm['gdn_norm'][j],
                                      prm['gdn_w_o'][j])
            convs.append(conv)
            ssms.append(ssm)
        x = _layernorm(DEEPNORM_ALPHA * x + (1.0 + g_m) * y, prm['ln1_g'][layer], prm['ln1_b'][layer])
        h = x * (1.0 + sc_f) + sh_f
        y = _moe(h, prm['moe_w_router'][layer], prm['moe_b_router'][layer], prm['moe_w_gu'][layer],
                 prm['moe_b_gu'][layer], prm['moe_w_down'][layer], prm['moe_b_down'][layer])
        x = _layernorm(DEEPNORM_ALPHA * x + (1.0 + g_f) * y, prm['ln2_g'][layer], prm['ln2_b'][layer])
    return x, jnp.stack(lats), jnp.stack(krs), jnp.stack(convs), jnp.stack(ssms)


def setup_inputs(seed: int = 0) -> dict:
    key = jax.random.key(seed)
    ks = iter(jax.random.split(key, 48))

    def nrm(shape, scale):
        return jax.random.normal(next(ks), shape, F32) * scale

    def gain(shape):
        return 1.0 + nrm(shape, 0.02)

    dt = jnp.exp(jax.random.uniform(next(ks), (N_GDN_LAYERS, GDN_V_HEADS), F32,
                                    minval=math.log(1e-3), maxval=math.log(1e-1)))
    inp = {
        'x_prompt': nrm((BATCH, SEQ, D_MODEL), 1.0),
        'x_sample': nrm((DEC_BATCH, DEC_SEQ, D_MODEL), 1.0),
        'c_prompt': nrm((BATCH, D_MODEL), 1.0),
        'c_sample': nrm((DEC_BATCH, D_MODEL), 1.0),
        'cache_mla_latent': nrm((N_MLA_LAYERS, DEC_BATCH, PAST_LEN, MLA_KV_LORA), 1.0),
        'cache_mla_krope': nrm((N_MLA_LAYERS, DEC_BATCH, PAST_LEN, MLA_ROPE), 1.0),
        'state_gdn_conv': nrm((N_GDN_LAYERS, DEC_BATCH, GDN_CONV_W - 1, GDN_CONV_DIM), 1.0),
        'state_gdn_ssm': nrm((N_GDN_LAYERS, DEC_BATCH, GDN_V_HEADS, GDN_DK, GDN_DV), 0.1),
        'w_cond': nrm((DEPTH, D_MODEL, 6 * D_MODEL), 0.3 * D_MODEL ** -0.5),
        'b_cond': nrm((DEPTH, 6 * D_MODEL), 0.02),
        'ln1_g': gain((DEPTH, D_MODEL)),
        'ln1_b': nrm((DEPTH, D_MODEL), 0.02),
        'ln2_g': gain((DEPTH, D_MODEL)),
        'ln2_b': nrm((DEPTH, D_MODEL), 0.02),
        'mla_w_in': nrm((N_MLA_LAYERS, D_MODEL, MLA_IN_DIM), D_MODEL ** -0.5),
        'mla_q_norm': gain((N_MLA_LAYERS, MLA_Q_LORA)),
        'mla_kv_norm': gain((N_MLA_LAYERS, MLA_KV_LORA)),
        'mla_w_uq': nrm((N_MLA_LAYERS, MLA_Q_LORA, MLA_HEADS * MLA_QK), MLA_Q_LORA ** -0.5),
        'mla_w_uk': nrm((N_MLA_LAYERS, MLA_KV_LORA, MLA_HEADS * MLA_NOPE), MLA_KV_LORA ** -0.5),
        'mla_w_uv': nrm((N_MLA_LAYERS, MLA_KV_LORA, MLA_HEADS * MLA_V), MLA_KV_LORA ** -0.5),
        'mla_w_o': nrm((N_MLA_LAYERS, MLA_HEADS * MLA_V, D_MODEL), DEEPNORM_BETA * (MLA_HEADS * MLA_V) ** -0.5),
        'gdn_w_in': nrm((N_GDN_LAYERS, D_MODEL, GDN_IN_DIM), D_MODEL ** -0.5),
        'gdn_conv_w': nrm((N_GDN_LAYERS, GDN_CONV_W, GDN_CONV_DIM), GDN_CONV_W ** -0.5),
        'gdn_a_log': jnp.log(jax.random.uniform(next(ks), (N_GDN_LAYERS, GDN_V_HEADS), F32, minval=1.0, maxval=16.0)),
        'gdn_dt_bias': dt + jnp.log(-jnp.expm1(-dt)),
        'gdn_norm': gain((N_GDN_LAYERS, GDN_DV)),
        'gdn_w_o': nrm((N_GDN_LAYERS, GDN_V_DIM, D_MODEL), DEEPNORM_BETA * GDN_V_DIM ** -0.5),
        'moe_w_router': nrm((DEPTH, D_MODEL, N_EXPERTS), D_MODEL ** -0.5),
        'moe_b_router': nrm((DEPTH, N_EXPERTS), 0.01),
        'moe_w_gu': nrm((DEPTH, N_EXPERTS, D_MODEL, 2 * D_FF), D_MODEL ** -0.5),
        'moe_b_gu': nrm((DEPTH, N_EXPERTS, 2 * D_FF), 0.02),
        'moe_w_down': nrm((DEPTH, N_EXPERTS, D_FF, D_MODEL), DEEPNORM_BETA * D_FF ** -0.5),
        'moe_b_down': nrm((DEPTH, N_EXPERTS, D_MODEL), 0.02),
    }
    return inp


def reference(x_prompt, x_sample, c_prompt, c_sample, cache_mla_latent, cache_mla_krope, state_gdn_conv,
              state_gdn_ssm, w_cond, b_cond, ln1_g, ln1_b, ln2_g, ln2_b, mla_w_in, mla_q_norm, mla_kv_norm,
              mla_w_uq, mla_w_uk, mla_w_uv, mla_w_o, gdn_w_in, gdn_conv_w, gdn_a_log, gdn_dt_bias, gdn_norm,
              gdn_w_o, moe_w_router, moe_b_router, moe_w_gu, moe_b_gu, moe_w_down, moe_b_down):
    prm = dict(w_cond=w_cond, b_cond=b_cond, ln1_g=ln1_g, ln1_b=ln1_b, ln2_g=ln2_g, ln2_b=ln2_b,
               mla_w_in=mla_w_in, mla_q_norm=mla_q_norm, mla_kv_norm=mla_kv_norm, mla_w_uq=mla_w_uq,
               mla_w_uk=mla_w_uk, mla_w_uv=mla_w_uv, mla_w_o=mla_w_o, gdn_w_in=gdn_w_in,
               gdn_conv_w=gdn_conv_w, gdn_a_log=gdn_a_log, gdn_dt_bias=gdn_dt_bias, gdn_norm=gdn_norm,
               gdn_w_o=gdn_w_o, moe_w_router=moe_w_router, moe_b_router=moe_b_router, moe_w_gu=moe_w_gu,
               moe_b_gu=moe_b_gu, moe_w_down=moe_w_down, moe_b_down=moe_b_down)
    pos_p = jnp.arange(x_prompt.shape[1], dtype=jnp.int32)
    past_len = cache_mla_latent.shape[2]
    pos_s = past_len + jnp.arange(x_sample.shape[1], dtype=jnp.int32)
    y_prompt, p_lat, p_kr, p_conv, p_ssm = _trunk(x_prompt, c_prompt, pos_p, None, prm)
    past = (cache_mla_latent, cache_mla_krope, state_gdn_conv, state_gdn_ssm)
    y_sample, s_lat, s_kr, s_conv, s_ssm = _trunk(x_sample, c_sample, pos_s, past, prm)
    return (y_prompt, y_sample, p_lat, p_kr, p_conv, p_ssm, s_lat, s_kr, s_conv, s_ssm)
```

```python
import functools
import math

import jax
import jax.numpy as jnp
from jax import lax
from jax.experimental import pallas as pl
from jax.experimental.pallas import tpu as pltpu

F32 = jnp.float32
BF16 = jnp.bfloat16
I32 = jnp.int32

D_MODEL = 1024
DEPTH = 4
CHUNK = 64
MLA_HEADS = 16
MLA_Q_LORA = 768
MLA_KV_LORA = 256
MLA_NOPE = 64
MLA_ROPE = 32
MLA_V = 64
MLA_QK = MLA_NOPE + MLA_ROPE
MLA_SCALE = MLA_QK ** -0.5
ROPE_BASE = 10000.0
GDN_QK_HEADS = 8
GDN_V_HEADS = 16
GDN_DK = 128
GDN_DV = 128
GDN_QK_DIM = GDN_QK_HEADS * GDN_DK
GDN_V_DIM = GDN_V_HEADS * GDN_DV
GDN_CONV_DIM = 2 * GDN_QK_DIM + GDN_V_DIM
GDN_CONV_W = 4
N_EXPERTS = 32
TOP_K = 4
D_FF = D_MODEL
SWIGLU_ALPHA = 1.702
SWIGLU_LIMIT = 7.0
DEEPNORM_ALPHA = (2 * DEPTH) ** 0.25
LN_EPS = 1e-5
RMS_EPS = 1e-6
L2_EPS = 1e-6

LANES = 128
HALO = 8
LOG2E = 1.4426950408889634
MOE_ROWS = 256
INV_BASE = 16
VMEM_LIMIT = 48 * 1024 * 1024


def _params(*sem):
    return pltpu.CompilerParams(dimension_semantics=sem, vmem_limit_bytes=VMEM_LIMIT)


def _tile(n, pref, mult=8):
    t = min(pref, n)
    while t >= mult:
        if n % t == 0 and t % mult == 0:
            return t
        t -= 1
    return n


def _dot(a, b):
    return jnp.dot(a, b, preferred_element_type=F32)


def _dot_nt(a, b):
    return lax.dot_general(a, b, (((1,), (1,)), ((), ())), preferred_element_type=F32)


def _dot_tn(a, b):
    return lax.dot_general(a, b, (((0,), (0,)), ((), ())), preferred_element_type=F32)


def _split(a):
    hi = a.astype(BF16)
    lo = (a - hi.astype(F32)).astype(BF16)
    return hi, lo


def _dot3(a, b, dot=_dot):
    ah, al = _split(a)
    bh, bl = _split(b)
    return dot(ah, bh) + (dot(ah, bl) + dot(al, bh))


def _dot_exact_rhs(a, b01, dot=_dot):
    a1 = a.astype(BF16)
    r1 = a - a1.astype(F32)
    a2 = r1.astype(BF16)
    a3 = (r1 - a2.astype(F32)).astype(BF16)
    return dot(a1, b01) + (dot(a2, b01) + dot(a3, b01))


def _sigmoid(x):
    return 1.0 / (1.0 + jnp.exp(-x))


def _silu(x):
    return x * _sigmoid(x)


def _softplus(x):
    return jnp.maximum(x, 0.0) + jnp.log(1.0 + jnp.exp(-jnp.abs(x)))


def _rows_scale(x, s):
    spt = s.shape[0]
    if spt == 1:
        return x * s[0]
    tm, d = x.shape
    return (x.reshape(spt, tm // spt, d) * s).reshape(tm, d)


def _rows_add(x, s):
    spt = s.shape[0]
    if spt == 1:
        return x + s[0]
    tm, d = x.shape
    return (x.reshape(spt, tm // spt, d) + s).reshape(tm, d)


def _modulate(x, sc, sh):
    return _rows_add(_rows_scale(x, 1.0 + sc), sh)


def _seq_spec(seq_len, tm, d=D_MODEL):
    if tm <= seq_len:
        per = seq_len // tm
        return pl.BlockSpec((1, 1, d), lambda i, *_: (i // per, 0, 0))
    return pl.BlockSpec((tm // seq_len, 1, d), lambda i, *_: (i, 0, 0))


def _row_tile(n_seq, seq_len, pref):
    if seq_len >= pref:
        return _tile(seq_len, pref)
    spt = _tile(n_seq, max(pref // seq_len, 1), mult=1)
    return spt * seq_len


def _full(shape):
    nd = len(shape)
    return pl.BlockSpec(shape, lambda *_: (0,) * nd)


def _cond_kernel(c_ref, w_ref, b_ref, o_ref):
    c = c_ref[...]
    o_ref[0] = _dot3(_silu(c), w_ref[0]) + b_ref[0]


def _cond(c, w_cond, b_cond):
    n_seq = c.shape[0]
    c = jnp.pad(c, ((0, -n_seq % HALO), (0, 0)))
    s = c.shape[0]
    n_out = w_cond.shape[-1]
    tn = _tile(n_out, 1536, LANES)
    return pl.pallas_call(
        _cond_kernel,
        grid=(DEPTH, n_out // tn),
        in_specs=[pl.BlockSpec((s, D_MODEL), lambda l, j: (0, 0)),
                  pl.BlockSpec((1, D_MODEL, tn), lambda l, j: (l, 0, j)),
                  pl.BlockSpec((1, 1, tn), lambda l, j: (l, 0, j))],
        out_specs=pl.BlockSpec((1, s, tn), lambda l, j: (l, 0, j)),
        out_shape=jax.ShapeDtypeStruct((DEPTH, s, n_out), F32),
        compiler_params=_params("parallel", "parallel"),
    )(c, w_cond, b_cond.reshape(DEPTH, 1, n_out))[:, :n_seq]


def _rope_tile(x, cos, sinp, sinm):
    half = MLA_ROPE // 2
    return x * cos + pltpu.roll(x, half, 1) * sinp + pltpu.roll(x, LANES - half, 1) * sinm


def _mla_proj_kernel(x_ref, sc_ref, sh_ref, win_ref, qn_ref, kvn_ref, wuq_ref, cos_ref, sinp_ref, sinm_ref,
                     q_ref, lat_ref, kr_ref):
    h = _modulate(x_ref[...], sc_ref[...], sh_ref[...]).astype(BF16)
    down = _dot(h, win_ref[...])
    cq = down[:, :MLA_Q_LORA]
    cq = cq * lax.rsqrt(jnp.mean(cq * cq, -1, keepdims=True) + RMS_EPS) * qn_ref[...]
    lat = down[:, MLA_Q_LORA:MLA_Q_LORA + MLA_KV_LORA]
    lat_ref[...] = lat * lax.rsqrt(jnp.mean(lat * lat, -1, keepdims=True) + RMS_EPS) * kvn_ref[...]
    cos, sinp, sinm = cos_ref[...], sinp_ref[...], sinm_ref[...]
    kr_ref[...] = _rope_tile(down[:, MLA_Q_LORA + MLA_KV_LORA:], cos, sinp, sinm)
    q = _dot(cq.astype(BF16), wuq_ref[...])
    for hd in range(MLA_HEADS):
        sl = slice(hd * LANES, (hd + 1) * LANES)
        q_ref[:, sl] = (_rope_tile(q[:, sl], cos, sinp, sinm) * (MLA_SCALE * LOG2E)).astype(BF16)


def _mla_proj(x, sc, sh, w, rope_tab, n_seq, seq_len):
    n = x.shape[0]
    tm = _row_tile(n_seq, seq_len, 512)
    tab_rows = rope_tab[0].shape[0]
    per = tab_rows // tm
    tab_spec = pl.BlockSpec((tm, LANES), lambda i: (i % per, 0))
    row = lambda w_: pl.BlockSpec((tm, w_), lambda i: (i, 0))
    return pl.pallas_call(
        _mla_proj_kernel,
        grid=(n // tm,),
        in_specs=[row(D_MODEL), _seq_spec(seq_len, tm), _seq_spec(seq_len, tm),
                  _full(w['w_in'].shape), _full((1, MLA_Q_LORA)), _full((1, MLA_KV_LORA)), _full(w['w_uq'].shape),
                  tab_spec, tab_spec, tab_spec],
        out_specs=[row(MLA_HEADS * LANES), row(MLA_KV_LORA), row(LANES)],
        out_shape=[jax.ShapeDtypeStruct((n, MLA_HEADS * LANES), BF16),
                   jax.ShapeDtypeStruct((n, MLA_KV_LORA), F32),
                   jax.ShapeDtypeStruct((n, LANES), F32)],
        compiler_params=_params("parallel"),
    )(x, sc, sh, w['w_in'], w['q_norm'], w['kv_norm'], w['w_uq'], *rope_tab)


def _kv_expand_kernel(lat_ref, kr_ref, wuk_ref, wuv_ref, one_ref, k_ref, v_ref):
    lat = lat_ref[...].astype(BF16)
    kn = _dot(lat, wuk_ref[...])
    kr = kr_ref[...]
    for hd in range(MLA_HEADS):
        sl = slice(hd * LANES, (hd + 1) * LANES)
        k_ref[:, sl] = (kn[:, sl] + kr).astype(BF16)
    v_ref[...] = (_dot(lat, wuv_ref[...]) + one_ref[...]).astype(BF16)


def _kv_expand(lat, kr, w):
    m = lat.shape[0]
    tm = _tile(m, 1024)
    wide = MLA_HEADS * LANES
    row = lambda w_: pl.BlockSpec((tm, w_), lambda i: (i, 0))
    return pl.pallas_call(
        _kv_expand_kernel,
        grid=(m // tm,),
        in_specs=[row(MLA_KV_LORA), row(LANES), _full((MLA_KV_LORA, wide)), _full((MLA_KV_LORA, wide)),
                  _full((1, wide))],
        out_specs=[row(wide), row(wide)],
        out_shape=[jax.ShapeDtypeStruct((m, wide), BF16)] * 2,
        compiler_params=_params("parallel"),
    )(lat, kr, w['w_uk'], w['w_uv'], w['v_one'])


def _attn_kernel(q_ref, k_ref, v_ref, o_ref, *, tq, tk, n_kv, q_off):
    i = pl.program_id(2)
    qpos0 = q_off + i * tq
    n_full = jnp.minimum(((qpos0 // CHUNK) + 1) * CHUNK // tk, n_kv)
    n_end = jnp.minimum((((qpos0 + tq - 1) // CHUNK + 1) * CHUNK + tk - 1) // tk, n_kv)
    lane = lax.broadcasted_iota(I32, (tq, LANES), 1)
    outs = []
    for hh in range(2):
        q = q_ref[0, :, hh * LANES:(hh + 1) * LANES]

        def step(j, carry, masked):
            m, acc = carry
            ks = pl.multiple_of(j * tk, 16)
            k = k_ref[0, pl.ds(ks, tk), hh * LANES:(hh + 1) * LANES]
            v = v_ref[0, pl.ds(ks, tk), hh * LANES:(hh + 1) * LANES]
            s = _dot_nt(q, k)
            if masked:
                qc = (qpos0 + lax.broadcasted_iota(I32, (tq, tk), 0)) // CHUNK
                kc = (ks + lax.broadcasted_iota(I32, (tq, tk), 1)) // CHUNK
                s = jnp.where(kc <= qc, s, -1e30)
            m_new = jnp.maximum(m, jnp.max(s, axis=-1, keepdims=True))
            p = jnp.exp2(s - m_new)
            acc = acc * jnp.exp2(m - m_new) + _dot(p.astype(BF16), v)
            return m_new, acc

        carry = (jnp.full((tq, 1), -1e30, F32), jnp.zeros((tq, LANES), F32))
        carry = lax.fori_loop(0, n_full, functools.partial(step, masked=False), carry)
        carry = lax.fori_loop(n_full, n_end, functools.partial(step, masked=True), carry)
        acc = carry[1]
        denom = jnp.sum(jnp.where(lane == MLA_V, acc, 0.0), axis=-1, keepdims=True)
        outs.append(acc / denom)
    o = jnp.where(lane < MLA_V, outs[0], pltpu.roll(outs[1], MLA_V, 1))
    o_ref[0] = o.astype(BF16)


def _attention(q, k, v, tq, tk):
    b, t_q, _ = q.shape
    t_k = k.shape[1]
    kern = functools.partial(_attn_kernel, tq=tq, tk=tk, n_kv=t_k // tk, q_off=t_k - t_q)
    pair = 2 * LANES
    return pl.pallas_call(
        kern,
        grid=(b, MLA_HEADS // 2, t_q // tq),
        in_specs=[pl.BlockSpec((1, tq, pair), lambda b_, p, i: (b_, i, p)),
                  pl.BlockSpec((1, t_k, pair), lambda b_, p, i: (b_, 0, p)),
                  pl.BlockSpec((1, t_k, pair), lambda b_, p, i: (b_, 0, p))],
        out_specs=pl.BlockSpec((1, tq, LANES), lambda b_, p, i: (b_, i, p)),
        out_shape=jax.ShapeDtypeStruct((b, t_q, MLA_HEADS * MLA_V), BF16),
        compiler_params=_params("parallel", "parallel", "arbitrary"),
    )(q, k, v)


def _layernorm(r, g, b):
    mu = jnp.mean(r, -1, keepdims=True)
    d = r - mu
    var = jnp.mean(d * d, -1, keepdims=True)
    return d * lax.rsqrt(var + LN_EPS) * g + b


def _mixer_out_kernel(a_ref, wo_ref, x_ref, gm_ref, lng_ref, lnb_ref, sc_ref, sh_ref, wr_ref, br_ref, up_ref,
                      xo_ref, h_ref, idx_ref, gate_ref, rank_ref, cnt_ref, run_ref):
    @pl.when(pl.program_id(0) == 0)
    def _():
        run_ref[...] = jnp.zeros_like(run_ref)

    y = _dot(a_ref[...], wo_ref[...])
    r = DEEPNORM_ALPHA * x_ref[...] + _rows_scale(y, 1.0 + gm_ref[...])
    xn = _layernorm(r, lng_ref[...], lnb_ref[...])
    xo_ref[...] = xn
    h = _modulate(xn, sc_ref[...], sh_ref[...])
    h_ref[...] = h.astype(BF16)

    logits = _dot3(wr_ref[...], h, _dot_nt) + br_ref[...]
    tm = logits.shape[1]
    eio = lax.broadcasted_iota(I32, (N_EXPERTS, tm), 0).astype(F32)
    sels, vals = [], []
    work = logits
    for k in range(TOP_K):
        m = jnp.max(work, axis=0, keepdims=True)
        ik = jnp.min(jnp.where(work == m, eio, float(N_EXPERTS)), axis=0, keepdims=True)
        sel = eio == ik
        work = jnp.where(sel, -jnp.inf, work)
        idx_ref[pl.ds(k, 1), :] = ik.astype(I32)
        sels.append(sel)
        vals.append(m)
    es = [jnp.exp(v - vals[0]) for v in vals]
    tot = es[0] + es[1] + es[2] + es[3]
    for k in range(TOP_K):
        gate_ref[pl.ds(k, 1), :] = es[k] / tot
    multi = sels[0] | sels[1] | sels[2] | sels[3]
    mh = jnp.where(multi, 1.0, 0.0)
    before = _dot(mh.astype(BF16), up_ref[...]) + run_ref[...]
    for k in range(TOP_K):
        rank_ref[pl.ds(k, 1), :] = jnp.sum(jnp.where(sels[k], before, 0.0), axis=0, keepdims=True).astype(I32)
    run = run_ref[...] + jnp.sum(mh, axis=1, keepdims=True)
    run_ref[...] = run
    cnt_ref[...] = run


def _mixer_out(a, w_o, x, gm, lng, lnb, sc, sh, w_r_t, b_r, n_seq, seq_len):
    n, kdim = a.shape
    tm = _row_tile(n_seq, seq_len, 512)
    upper = (jnp.arange(tm)[:, None] < jnp.arange(tm)[None, :]).astype(BF16)
    row = lambda w_: pl.BlockSpec((tm, w_), lambda i: (i, 0))
    col = pl.BlockSpec((TOP_K, tm), lambda i: (0, i))
    ss = _seq_spec(seq_len, tm)
    return pl.pallas_call(
        _mixer_out_kernel,
        grid=(n // tm,),
        in_specs=[row(kdim), _full(w_o.shape), row(D_MODEL), ss, _full((1, D_MODEL)), _full((1, D_MODEL)), ss, ss,
                  _full((N_EXPERTS, D_MODEL)), _full((N_EXPERTS, 1)), _full((tm, tm))],
        out_specs=[row(D_MODEL), row(D_MODEL), col, col, col, _full((N_EXPERTS, 1))],
        out_shape=[jax.ShapeDtypeStruct((n, D_MODEL), F32), jax.ShapeDtypeStruct((n, D_MODEL), BF16),
                   jax.ShapeDtypeStruct((TOP_K, n), I32), jax.ShapeDtypeStruct((TOP_K, n), F32),
                   jax.ShapeDtypeStruct((TOP_K, n), I32), jax.ShapeDtypeStruct((N_EXPERTS, 1), F32)],
        scratch_shapes=[pltpu.VMEM((N_EXPERTS, 1), F32)],
        compiler_params=_params("arbitrary"),
    )(a, w_o, x, gm, lng, lnb, sc, sh, w_r_t, b_r, upper)


def _expert_kernel(be_ref, na_ref, x_ref, wgu_ref, bgu_ref, wd_ref, bd_ref, y_ref):
    @pl.when(pl.program_id(0) < na_ref[0])
    def _():
        gu = _dot(x_ref[...], wgu_ref[0]) + bgu_ref[0]
        gate = jnp.minimum(gu[:, :D_FF], SWIGLU_LIMIT)
        up = jnp.clip(gu[:, D_FF:], -SWIGLU_LIMIT, SWIGLU_LIMIT)
        act = (up + 1.0) * gate * _sigmoid(SWIGLU_ALPHA * gate)
        y_ref[...] = _dot(act.astype(BF16), wd_ref[0]) + bd_ref[0]

    @pl.when(pl.program_id(0) >= na_ref[0])
    def _():
        y_ref[...] = jnp.zeros_like(y_ref)


def _experts(x_rows, blk_e, n_act, w_gu, b_gu, w_down, b_down):
    p = x_rows.shape[0]
    grid_spec = pltpu.PrefetchScalarGridSpec(
        num_scalar_prefetch=2,
        grid=(p // MOE_ROWS,),
        in_specs=[pl.BlockSpec((MOE_ROWS, D_MODEL), lambda i, be, na: (i, 0)),
                  pl.BlockSpec((1, D_MODEL, 2 * D_FF), lambda i, be, na: (be[i], 0, 0)),
                  pl.BlockSpec((1, 1, 2 * D_FF), lambda i, be, na: (be[i], 0, 0)),
                  pl.BlockSpec((1, D_FF, D_MODEL), lambda i, be, na: (be[i], 0, 0)),
                  pl.BlockSpec((1, 1, D_MODEL), lambda i, be, na: (be[i], 0, 0))],
        out_specs=pl.BlockSpec((MOE_ROWS, D_MODEL), lambda i, be, na: (i, 0)),
    )
    return pl.pallas_call(
        _expert_kernel,
        grid_spec=grid_spec,
        out_shape=jax.ShapeDtypeStruct((p, D_MODEL), F32),
        compiler_params=_params("arbitrary"),
    )(blk_e, n_act, x_rows, w_gu, b_gu, w_down, b_down)


def _combine_kernel(y_ref, g_ref, x_ref, gf_ref, lng_ref, lnb_ref, o_ref):
    g = g_ref[...]
    y = y_ref[0] * g[:, 0:1]
    for k in range(1, TOP_K):
        y = y + y_ref[k] * g[:, k:k + 1]
    r = DEEPNORM_ALPHA * x_ref[...] + _rows_scale(y, 1.0 + gf_ref[...])
    o_ref[...] = _layernorm(r, lng_ref[...], lnb_ref[...])


def _combine(y4, gates, x, gf, lng, lnb, n_seq, seq_len):
    n = x.shape[0]
    tm = _row_tile(n_seq, seq_len, 512)
    row = pl.BlockSpec((tm, D_MODEL), lambda i: (i, 0))
    return pl.pallas_call(
        _combine_kernel,
        grid=(n // tm,),
        in_specs=[pl.BlockSpec((TOP_K, tm, D_MODEL), lambda i: (0, i, 0)),
                  pl.BlockSpec((tm, TOP_K), lambda i: (i, 0)), row, _seq_spec(seq_len, tm),
                  _full((1, D_MODEL)), _full((1, D_MODEL))],
        out_specs=row,
        out_shape=jax.ShapeDtypeStruct((n, D_MODEL), F32),
        compiler_params=_params("parallel"),
    )(y4, gates, x, gf, lng, lnb)


def _moe_and_norm(x, h, idx_t, gate_t, rank_t, cnt, gf, lng, lnb, w, n_seq, seq_len):
    n = x.shape[0]
    a = n * TOP_K
    counts = cnt[:, 0].astype(I32)
    padded = (counts + MOE_ROWS - 1) // MOE_ROWS * MOE_ROWS
    pad_end = jnp.cumsum(padded)
    pad_start = pad_end - padded
    dest_t = pad_start[idx_t] + rank_t
    nb = (a + N_EXPERTS * (MOE_ROWS - 1) + MOE_ROWS - 1) // MOE_ROWS
    p = nb * MOE_ROWS
    tok = jnp.broadcast_to(jnp.arange(n, dtype=I32)[None], (TOP_K, n))
    row_tok = jnp.zeros((p,), I32).at[dest_t.reshape(-1)].set(tok.reshape(-1))
    x_rows = jnp.take(h, row_tok, axis=0)
    blk_e = jnp.minimum(jnp.searchsorted(pad_end, jnp.arange(nb, dtype=I32) * MOE_ROWS, side='right'),
                        N_EXPERTS - 1).astype(I32)
    n_act = (pad_end[-1:] // MOE_ROWS).astype(I32)
    y_rows = _experts(x_rows, blk_e, n_act, w['w_gu'], w['b_gu'], w['w_down'], w['b_down'])
    y4 = jnp.take(y_rows, dest_t, axis=0)
    return _combine(y4, gate_t.T, x, gf, lng, lnb, n_seq, seq_len)


def _mod_matmul_kernel(x_ref, sc_ref, sh_ref, w_ref, o_ref):
    h = _modulate(x_ref[...], sc_ref[...], sh_ref[...]).astype(BF16)
    o_ref[...] = _dot(h, w_ref[...]).astype(o_ref.dtype)


def _mod_matmul(x, sc, sh, w, out_dtype, n_seq, seq_len):
    n = x.shape[0]
    n_out = w.shape[1]
    tm = _row_tile(n_seq, seq_len, 1024)
    tn = _tile(n_out, 1024, LANES)
    return pl.pallas_call(
        _mod_matmul_kernel,
        grid=(n // tm, n_out // tn),
        in_specs=[pl.BlockSpec((tm, D_MODEL), lambda i, j: (i, 0)), _seq_spec(seq_len, tm), _seq_spec(seq_len, tm),
                  pl.BlockSpec((D_MODEL, tn), lambda i, j: (0, j))],
        out_specs=pl.BlockSpec((tm, tn), lambda i, j: (i, j)),
        out_shape=jax.ShapeDtypeStruct((n, n_out), out_dtype),
        compiler_params=_params("parallel", "arbitrary"),
    )(x, sc, sh, w)


def _gdn_gates_kernel(x_ref, sc_ref, sh_ref, wab_ref, wabt_ref, alog_ref, dtb_ref, alogt_ref, dtbt_ref,
                      tri_ref, trit_ref, gc_ref, beta_ref, gct_ref):
    h = _modulate(x_ref[...], sc_ref[...], sh_ref[...]).astype(BF16)
    hv = GDN_V_HEADS
    ab = _dot(h, wab_ref[...])
    beta_ref[...] = _sigmoid(ab[:, :hv])
    g = -jnp.exp(alog_ref[...]) * _softplus(ab[:, hv:] + dtb_ref[...])
    gc_ref[...] = _dot_exact_rhs_lhs(tri_ref[...], g)
    abt = _dot_nt(wabt_ref[...], h)
    gt = -jnp.exp(alogt_ref[...]) * _softplus(abt + dtbt_ref[...])
    gct_ref[...] = _dot_exact_rhs(gt, trit_ref[...])


def _dot_exact_rhs_lhs(a01, b):
    b1 = b.astype(BF16)
    r1 = b - b1.astype(F32)
    b2 = r1.astype(BF16)
    b3 = (r1 - b2.astype(F32)).astype(BF16)
    return _dot(a01, b1) + (_dot(a01, b2) + _dot(a01, b3))


def _gdn_gates(x, sc, sh, w, n_seq, seq_len, chunk):
    n = x.shape[0]
    tm = _row_tile(n_seq, seq_len, 512)
    hv = GDN_V_HEADS
    r = jnp.arange(tm)
    tri = ((r[:, None] // chunk == r[None, :] // chunk) & (r[None, :] <= r[:, None])).astype(BF16)
    row = lambda w_: pl.BlockSpec((tm, w_), lambda i: (i, 0))
    return pl.pallas_call(
        _gdn_gates_kernel,
        grid=(n // tm,),
        in_specs=[row(D_MODEL), _seq_spec(seq_len, tm), _seq_spec(seq_len, tm),
                  _full((D_MODEL, 2 * hv)), _full((hv, D_MODEL)), _full((1, hv)), _full((1, hv)),
                  _full((hv, 1)), _full((hv, 1)), _full((tm, tm)), _full((tm, tm))],
        out_specs=[row(hv), row(hv), pl.BlockSpec((hv, tm), lambda i: (0, i))],
        out_shape=[jax.ShapeDtypeStruct((n, hv), F32), jax.ShapeDtypeStruct((n, hv), F32),
                   jax.ShapeDtypeStruct((hv, n), F32)],
        compiler_params=_params("parallel"),
    )(x, sc, sh, w['w_ab'], w['w_a_t'], w['a_log'], w['dt_bias'], w['a_log'].T, w['dt_bias'].T, tri, tri.T)


def _conv_kernel(x_ref, prev_ref, st_ref, w_ref, o_ref, *, per, norm, n_q_blocks):
    i = pl.program_id(0)
    j = pl.program_id(1)
    x = x_ref[...]
    tm = x.shape[0]
    halo = jnp.where(i % per == 0, st_ref[0], prev_ref[...])
    w = w_ref[...]
    rows = lax.broadcasted_iota(I32, (HALO, x.shape[1]), 0)
    acc = x * w[GDN_CONV_W - 1:GDN_CONV_W]
    for s in range(1, GDN_CONV_W):
        xs = pltpu.roll(x, s, 0)
        head = jnp.where(rows < s, pltpu.roll(halo, s, 0), xs[:HALO])
        xs = jnp.concatenate([head, xs[HALO:]], axis=0) if tm > HALO else head
        acc = acc + xs * w[GDN_CONV_W - 1 - s:GDN_CONV_W - s]
    y = _silu(acc)
    if norm:
        scale = jnp.where(j < n_q_blocks, GDN_DK ** -0.5, 1.0)
        for hd in range(y.shape[1] // GDN_DK):
            sl = slice(hd * GDN_DK, (hd + 1) * GDN_DK)
            yh = y[:, sl]
            o_ref[:, sl] = yh * (lax.rsqrt(jnp.sum(yh * yh, -1, keepdims=True) + L2_EPS) * scale)
    else:
        o_ref[...] = y


def _gdn_conv(qkv, state8, conv_w, n_seq, seq_len, col0, width, norm):
    n = qkv.shape[0]
    tm = _tile(seq_len, 512)
    per = seq_len // tm
    tn = 512
    c0 = col0 // tn
    kern = functools.partial(_conv_kernel, per=per, norm=norm, n_q_blocks=GDN_QK_DIM // tn)
    hb = tm // HALO
    return pl.pallas_call(
        kern,
        grid=(n // tm, width // tn),
        in_specs=[pl.BlockSpec((tm, tn), lambda i, j: (i, c0 + j)),
                  pl.BlockSpec((HALO, tn), lambda i, j: (jnp.maximum(i * hb - 1, 0), c0 + j)),
                  pl.BlockSpec((1, HALO, tn), lambda i, j: (i // per, 0, c0 + j)),
                  pl.BlockSpec((GDN_CONV_W, tn), lambda i, j: (0, c0 + j))],
        out_specs=pl.BlockSpec((tm, tn), lambda i, j: (i, j)),
        out_shape=jax.ShapeDtypeStruct((n, width), F32),
        compiler_params=_params("parallel", "parallel"),
    )(qkv, qkv, state8, conv_w)


def _inv_unit_lower(lmat, c):
    row = lax.broadcasted_iota(I32, (c, c), 0)
    col = lax.broadcasted_iota(I32, (c, c), 1)
    eye = jnp.where(row == col, 1.0, 0.0)
    ld = jnp.where(row // INV_BASE == col // INV_BASE, lmat, 0.0)
    t = eye - ld
    pw = ld
    size = 2
    while size < INV_BASE:
        pw = _dot3(pw, pw)
        t = t + _dot3(t, pw)
        size *= 2
    size = INV_BASE
    while size < c:
        off = (row // (2 * size) == col // (2 * size)) & (row // size % 2 == 1) & (col // size % 2 == 0)
        t = t - _dot3(_dot3(t, jnp.where(off, lmat, 0.0)), t)
        size *= 2
    return t


def _gdn_core_kernel(q_ref, k_ref, v_ref, z_ref, gc_ref, beta_ref, gct_ref, s0_ref, nw_ref, o_ref, s_ref, *, c, hb):
    @pl.when(pl.program_id(2) == 0)
    def _():
        s_ref[...] = s0_ref[...]

    row = lax.broadcasted_iota(I32, (c, c), 0)
    col = lax.broadcasted_iota(I32, (c, c), 1)
    lower = col <= row
    strict = col < row
    for kh in range(hb // 2):
        qn = q_ref[0, :, kh * GDN_DK:(kh + 1) * GDN_DK]
        kn = k_ref[0, :, kh * GDN_DK:(kh + 1) * GDN_DK]
        kb = kn.astype(BF16)
        kk = _dot_nt(kb, kb)
        qk = _dot_nt(qn.astype(BF16), kb)
        for hh in range(2 * kh, 2 * kh + 2):
            gcc = gc_ref[0, 0, :, hh:hh + 1]
            gcr = gct_ref[0, 0, hh:hh + 1, :]
            bc = beta_ref[0, 0, :, hh:hh + 1]
            decay = jnp.where(lower, jnp.exp(jnp.minimum(gcc - gcr, 0.0)), 0.0)
            lmat = jnp.where(strict, kk * decay, 0.0) * bc
            t = _inv_unit_lower(lmat, c)
            egc = jnp.exp(gcc)
            v = v_ref[0, :, hh * GDN_DV:(hh + 1) * GDN_DV]
            rhs = jnp.concatenate([v * bc, kn * (bc * egc)], axis=1)
            uw = _dot(t.astype(BF16), rhs.astype(BF16))
            s = s_ref[0, hh]
            sb = s.astype(BF16)
            v_new = uw[:, :GDN_DV] - _dot(uw[:, GDN_DV:].astype(BF16), sb)
            vb = v_new.astype(BF16)
            o = _dot((qn * egc).astype(BF16), sb) + _dot((qk * decay).astype(BF16), vb)
            glast = gcc[c - 1:c, :]
            kd = kn * jnp.exp(glast - gcc)
            s_ref[0, hh] = s * jnp.exp(glast) + _dot_tn(kd.astype(BF16), vb)
            z = z_ref[0, :, hh * GDN_DV:(hh + 1) * GDN_DV].astype(F32)
            on = o * lax.rsqrt(jnp.mean(o * o, -1, keepdims=True) + RMS_EPS) * nw_ref[...]
            o_ref[0, :, hh * GDN_DV:(hh + 1) * GDN_DV] = (on * _silu(z)).astype(BF16)


def _gdn_core(qk, v, z, gc, beta, gct, s0, norm_w, c, hb=2):
    b, t, _ = v.shape
    ng = GDN_V_HEADS // hb
    kw = hb // 2 * GDN_DK
    vw = hb * GDN_DV
    gc_g = gc.reshape(b, t, ng, hb).transpose(0, 2, 1, 3)
    beta_g = beta.reshape(b, t, ng, hb).transpose(0, 2, 1, 3)
    gct_g = gct.reshape(b, ng, hb, t)
    kern = functools.partial(_gdn_core_kernel, c=c, hb=hb)
    nkb = GDN_QK_DIM // kw
    return pl.pallas_call(
        kern,
        grid=(b, ng, t // c),
        in_specs=[pl.BlockSpec((1, c, kw), lambda b_, g, i: (b_, i, g)),
                  pl.BlockSpec((1, c, kw), lambda b_, g, i: (b_, i, nkb + g)),
                  pl.BlockSpec((1, c, vw), lambda b_, g, i: (b_, i, g)),
                  pl.BlockSpec((1, c, vw), lambda b_, g, i: (b_, i, g)),
                  pl.BlockSpec((1, 1, c, hb), lambda b_, g, i: (b_, g, i, 0)),
                  pl.BlockSpec((1, 1, c, hb), lambda b_, g, i: (b_, g, i, 0)),
                  pl.BlockSpec((1, 1, hb, c), lambda b_, g, i: (b_, g, 0, i)),
                  pl.BlockSpec((1, hb, GDN_DK, GDN_DV), lambda b_, g, i: (b_, g, 0, 0)),
                  _full((1, GDN_DV))],
        out_specs=[pl.BlockSpec((1, c, vw), lambda b_, g, i: (b_, i, g)),
                   pl.BlockSpec((1, hb, GDN_DK, GDN_DV), lambda b_, g, i: (b_, g, 0, 0))],
        out_shape=[jax.ShapeDtypeStruct((b, t, GDN_V_DIM), BF16),
                   jax.ShapeDtypeStruct((b, GDN_V_HEADS, GDN_DK, GDN_DV), F32)],
        compiler_params=_params("parallel", "parallel", "arbitrary"),
    )(qk, qk, v, z, gc_g, beta_g, gct_g, s0, norm_w)


def _prep_weights(p):
    f = {}
    pad_heads = lambda w_, dh: jnp.pad(w_.reshape(w_.shape[0], w_.shape[1], MLA_HEADS, dh),
                                       ((0, 0), (0, 0), (0, 0), (0, LANES - dh))).reshape(
                                           w_.shape[0], w_.shape[1], MLA_HEADS * LANES)
    w_in = p['mla_w_in']
    nl = w_in.shape[0]
    lo = MLA_Q_LORA + MLA_KV_LORA
    z = lambda k: jnp.zeros((nl, D_MODEL, k), F32)
    f['mla_w_in'] = jnp.concatenate([w_in[..., :lo], z(MLA_NOPE), w_in[..., lo:], z(LANES - MLA_QK)], -1).astype(BF16)
    f['mla_w_uq'] = pad_heads(p['mla_w_uq'], MLA_QK).astype(BF16)
    f['mla_w_uk'] = pad_heads(p['mla_w_uk'], MLA_NOPE).astype(BF16)
    f['mla_w_uv'] = pad_heads(p['mla_w_uv'], MLA_V).astype(BF16)
    f['mla_w_o'] = p['mla_w_o'].astype(BF16)
    f['v_one'] = jnp.tile((jnp.arange(LANES) == MLA_V).astype(F32), MLA_HEADS)[None]
    g_in = p['gdn_w_in']
    f['gdn_w_qkv'] = g_in[..., :GDN_CONV_DIM].astype(BF16)
    f['gdn_w_z'] = g_in[..., GDN_CONV_DIM:GDN_CONV_DIM + GDN_V_DIM].astype(BF16)
    f['gdn_w_ab'] = g_in[..., GDN_CONV_DIM + GDN_V_DIM:].astype(BF16)
    f['gdn_w_a_t'] = jnp.swapaxes(g_in[..., GDN_CONV_DIM + GDN_V_DIM + GDN_V_HEADS:], 1, 2).astype(BF16)
    f['gdn_w_o'] = p['gdn_w_o'].astype(BF16)
    f['moe_w_r_t'] = jnp.swapaxes(p['moe_w_router'], 1, 2)
    f['moe_w_gu'] = p['moe_w_gu'].astype(BF16)
    f['moe_w_down'] = p['moe_w_down'].astype(BF16)
    return f


def _rope_tables(pos):
    half = MLA_ROPE // 2
    inv_freq = ROPE_BASE ** (-jnp.arange(half, dtype=F32) / half)
    ang = pos.astype(F32)[:, None] * inv_freq[None, :]
    cos, sin = jnp.cos(ang), jnp.sin(ang)
    t = pos.shape[0]
    z = lambda k: jnp.zeros((t, k), F32)
    cos_t = jnp.concatenate([jnp.ones((t, MLA_NOPE), F32), cos, cos, z(LANES - MLA_QK)], -1)
    sinp_t = jnp.concatenate([z(MLA_NOPE + half), sin, z(LANES - MLA_QK)], -1)
    sinm_t = jnp.concatenate([z(MLA_NOPE), -sin, z(half + LANES - MLA_QK)], -1)
    return cos_t, sinp_t, sinm_t


def _trunk(x, c, pos, past, p, f):
    bsz, t, _ = x.shape
    n = bsz * t
    x = x.reshape(n, D_MODEL)
    mod = _cond(c, p['w_cond'], p['b_cond']).reshape(DEPTH, bsz, 6, 1, D_MODEL)
    tm_rope = _row_tile(bsz, t, 512)
    tabs = _rope_tables(pos)
    if tm_rope > t:
        tabs = tuple(jnp.tile(tb, (tm_rope // t, 1)) for tb in tabs)
    lats, krs, convs, ssms = [], [], [], []
    for layer in range(DEPTH):
        sh_m, sc_m, g_m, sh_f, sc_f, g_f = [mod[layer, :, i] for i in range(6)]
        j = layer // 2
        if layer % 2 == 0:
            w = dict(w_in=f['mla_w_in'][j], q_norm=p['mla_q_norm'][j][None], kv_norm=p['mla_kv_norm'][j][None],
                     w_uq=f['mla_w_uq'][j], w_uk=f['mla_w_uk'][j], w_uv=f['mla_w_uv'][j], v_one=f['v_one'])
            q, lat, krt = _mla_proj(x, sc_m, sh_m, w, tabs, bsz, t)
            lats.append(lat.reshape(bsz, t, MLA_KV_LORA))
            krs.append(krt[:, MLA_NOPE:MLA_QK].reshape(bsz, t, MLA_ROPE))
            if past is None:
                lat_all, kr_all, t_k = lat, krt, t
            else:
                p_lat, p_kr = past[0][j], past[1][j]
                t_k = p_lat.shape[1] + t
                p_krt = jnp.pad(p_kr, ((0, 0), (0, 0), (MLA_NOPE, LANES - MLA_QK)))
                lat_all = jnp.concatenate([p_lat, lat.reshape(bsz, t, -1)], 1).reshape(bsz * t_k, MLA_KV_LORA)
                kr_all = jnp.concatenate([p_krt, krt.reshape(bsz, t, -1)], 1).reshape(bsz * t_k, LANES)
            k, v = _kv_expand(lat_all, kr_all, w)
            wide = MLA_HEADS * LANES
            tq = _tile(t, 512)
            tk = tq if past is None else t_k
            ctx = _attention(q.reshape(bsz, t, wide), k.reshape(bsz, t_k, wide), v.reshape(bsz, t_k, wide), tq, tk)
            a, w_o = ctx.reshape(n, MLA_HEADS * MLA_V), f['mla_w_o'][j]
        else:
            chunk = _tile(t, 128)
            w = dict(w_ab=f['gdn_w_ab'][j], w_a_t=f['gdn_w_a_t'][j], a_log=p['gdn_a_log'][j][None],
                     dt_bias=p['gdn_dt_bias'][j][None])
            qkv = _mod_matmul(x, sc_m, sh_m, f['gdn_w_qkv'][j], F32, bsz, t)
            z = _mod_matmul(x, sc_m, sh_m, f['gdn_w_z'][j], BF16, bsz, t)
            gc, beta, gct = _gdn_gates(x, sc_m, sh_m, w, bsz, t, chunk)
            if past is None:
                conv_state = jnp.zeros((bsz, GDN_CONV_W - 1, GDN_CONV_DIM), F32)
                s0 = jnp.zeros((bsz, GDN_V_HEADS, GDN_DK, GDN_DV), F32)
            else:
                conv_state, s0 = past[2][j], past[3][j]
            state8 = jnp.pad(conv_state, ((0, 0), (HALO - (GDN_CONV_W - 1), 0), (0, 0)))
            qkv3 = qkv.reshape(bsz, t, GDN_CONV_DIM)
            convs.append(qkv3[:, -(GDN_CONV_W - 1):])
            cw = p['gdn_conv_w'][j]
            qk_n = _gdn_conv(qkv, state8, cw, bsz, t, 0, 2 * GDN_QK_DIM, True)
            v_c = _gdn_conv(qkv, state8, cw, bsz, t, 2 * GDN_QK_DIM, GDN_V_DIM, False)
            o, s_new = _gdn_core(qk_n.reshape(bsz, t, -1), v_c.reshape(bsz, t, -1), z.reshape(bsz, t, -1),
                                 gc.reshape(bsz, t, -1), beta.reshape(bsz, t, -1),
                                 gct.reshape(GDN_V_HEADS, bsz, t).transpose(1, 0, 2), s0,
                                 p['gdn_norm'][j][None], chunk)
            ssms.append(s_new)
            a, w_o = o.reshape(n, GDN_V_DIM), f['gdn_w_o'][j]
        x, h, idx_t, gate_t, rank_t, cnt = _mixer_out(
            a, w_o, x, g_m, p['ln1_g'][layer][None], p['ln1_b'][layer][None], sc_f, sh_f,
            f['moe_w_r_t'][layer], p['moe_b_router'][layer][:, None], bsz, t)
        wm = dict(w_gu=f['moe_w_gu'][layer], b_gu=p['moe_b_gu'][layer][:, None], w_down=f['moe_w_down'][layer],
                  b_down=p['moe_b_down'][layer][:, None])
        x = _moe_and_norm(x, h, idx_t, gate_t, rank_t, cnt, g_f, p['ln2_g'][layer][None], p['ln2_b'][layer][None],
                          wm, bsz, t)
    return x.reshape(bsz, t, D_MODEL), jnp.stack(lats), jnp.stack(krs), jnp.stack(convs), jnp.stack(ssms)


def kernel(x_prompt, x_sample, c_prompt, c_sample, cache_mla_latent, cache_mla_krope, state_gdn_conv, state_gdn_ssm, w_cond, b_cond, ln1_g, ln1_b, ln2_g, ln2_b, mla_w_in, mla_q_norm, mla_kv_norm, mla_w_uq, mla_w_uk, mla_w_uv, mla_w_o, gdn_w_in, gdn_conv_w, gdn_a_log, gdn_dt_bias, gdn_norm, gdn_w_o, moe_w_router, moe_b_router, moe_w_gu, moe_b_gu, moe_w_down, moe_b_down):
    p = dict(w_cond=w_cond, b_cond=b_cond, ln1_g=ln1_g, ln1_b=ln1_b, ln2_g=ln2_g, ln2_b=ln2_b,
             mla_w_in=mla_w_in, mla_q_norm=mla_q_norm, mla_kv_norm=mla_kv_norm, mla_w_uq=mla_w_uq,
             mla_w_uk=mla_w_uk, mla_w_uv=mla_w_uv, mla_w_o=mla_w_o, gdn_w_in=gdn_w_in,
             gdn_conv_w=gdn_conv_w, gdn_a_log=gdn_a_log, gdn_dt_bias=gdn_dt_bias, gdn_norm=gdn_norm,
             gdn_w_o=gdn_w_o, moe_w_router=moe_w_router, moe_b_router=moe_b_router, moe_w_gu=moe_w_gu,
             moe_b_gu=moe_b_gu, moe_w_down=moe_w_down, moe_b_down=moe_b_down)
    f = _prep_weights(p)
    past_len = cache_mla_latent.shape[2]
    assert past_len % CHUNK == 0 and x_sample.shape[1] <= CHUNK
    pos_p = jnp.arange(x_prompt.shape[1], dtype=I32)
    pos_s = past_len + jnp.arange(x_sample.shape[1], dtype=I32)
    y_p, p_lat, p_kr, p_conv, p_ssm = _trunk(x_prompt, c_prompt, pos_p, None, p, f)
    past = (cache_mla_latent, cache_mla_krope, state_gdn_conv, state_gdn_ssm)
    y_s, s_lat, s_kr, s_conv, s_ssm = _trunk(x_sample, c_sample, pos_s, past, p, f)
    return (y_p, y_s, p_lat, p_kr, p_conv, p_ssm, s_lat, s_kr, s_conv, s_ssm)
```

```python
import functools
import math

import jax
import jax.numpy as jnp
from jax import lax
from jax.experimental import pallas as pl
from jax.experimental.pallas import tpu as pltpu

F32 = jnp.float32
BF16 = jnp.bfloat16
I32 = jnp.int32

D_MODEL = 1024
DEPTH = 4
CHUNK = 64
MLA_HEADS = 16
MLA_Q_LORA = 768
MLA_KV_LORA = 256
MLA_NOPE = 64
MLA_ROPE = 32
MLA_V = 64
MLA_QK = MLA_NOPE + MLA_ROPE
MLA_SCALE = MLA_QK ** -0.5
ROPE_BASE = 10000.0
GDN_QK_HEADS = 8
GDN_V_HEADS = 16
GDN_DK = 128
GDN_DV = 128
GDN_QK_DIM = GDN_QK_HEADS * GDN_DK
GDN_V_DIM = GDN_V_HEADS * GDN_DV
GDN_CONV_DIM = 2 * GDN_QK_DIM + GDN_V_DIM
GDN_CONV_W = 4
N_EXPERTS = 32
TOP_K = 4
D_FF = D_MODEL
SWIGLU_ALPHA = 1.702
SWIGLU_LIMIT = 7.0
DEEPNORM_ALPHA = (2 * DEPTH) ** 0.25
LN_EPS = 1e-5
RMS_EPS = 1e-6
L2_EPS = 1e-6

LANES = 128
HALO = 8
LOG2E = 1.4426950408889634
MOE_ROWS = 256
INV_BASE = 16
VMEM_LIMIT = 48 * 1024 * 1024


def _params(*sem):
    return pltpu.CompilerParams(dimension_semantics=sem, vmem_limit_bytes=VMEM_LIMIT)


def _tile(n, pref, mult=8):
    t = min(pref, n)
    while t >= mult:
        if n % t == 0 and t % mult == 0:
            return t
        t -= 1
    return n


def _dot(a, b):
    return jnp.dot(a, b, preferred_element_type=F32)


def _dot_nt(a, b):
    return lax.dot_general(a, b, (((1,), (1,)), ((), ())), preferred_element_type=F32)


def _dot_tn(a, b):
    return lax.dot_general(a, b, (((0,), (0,)), ((), ())), preferred_element_type=F32)


def _dotb(a, b):
    return _dot(a.astype(BF16), b.astype(BF16))


def _split(a):
    hi = a.astype(BF16)
    lo = (a - hi.astype(F32)).astype(BF16)
    return hi, lo


def _dot3(a, b, dot=_dot):
    ah, al = _split(a)
    bh, bl = _split(b)
    return dot(ah, bh) + (dot(ah, bl) + dot(al, bh))


def _dot_exact_rhs(a, b01, dot=_dot):
    a1 = a.astype(BF16)
    r1 = a - a1.astype(F32)
    a2 = r1.astype(BF16)
    a3 = (r1 - a2.astype(F32)).astype(BF16)
    return dot(a1, b01) + (dot(a2, b01) + dot(a3, b01))


def _sigmoid(x):
    return 1.0 / (1.0 + jnp.exp(-x))


def _silu(x):
    return x * _sigmoid(x)


def _softplus(x):
    return jnp.maximum(x, 0.0) + jnp.log(1.0 + jnp.exp(-jnp.abs(x)))


def _rows_scale(x, s):
    spt = s.shape[0]
    if spt == 1:
        return x * s[0]
    tm, d = x.shape
    return (x.reshape(spt, tm // spt, d) * s).reshape(tm, d)


def _rows_add(x, s):
    spt = s.shape[0]
    if spt == 1:
        return x + s[0]
    tm, d = x.shape
    return (x.reshape(spt, tm // spt, d) + s).reshape(tm, d)


def _modulate(x, sc, sh):
    return _rows_add(_rows_scale(x, 1.0 + sc), sh)


def _seq_spec(seq_len, tm, d=D_MODEL):
    if tm <= seq_len:
        per = seq_len // tm
        return pl.BlockSpec((1, 1, d), lambda i, *_: (i // per, 0, 0))
    return pl.BlockSpec((tm // seq_len, 1, d), lambda i, *_: (i, 0, 0))


def _row_tile(n_seq, seq_len, pref):
    if seq_len >= pref:
        return _tile(seq_len, pref)
    spt = _tile(n_seq, max(pref // seq_len, 1), mult=1)
    return spt * seq_len


def _full(shape):
    nd = len(shape)
    return pl.BlockSpec(shape, lambda *_: (0,) * nd)


def _cond_kernel(c_ref, w_ref, b_ref, o_ref):
    c = c_ref[...]
    o_ref[0] = _dot3(_silu(c), w_ref[0]) + b_ref[0]


def _cond(c, w_cond, b_cond):
    n_seq = c.shape[0]
    c = jnp.pad(c, ((0, -n_seq % HALO), (0, 0)))
    s = c.shape[0]
    n_out = w_cond.shape[-1]
    tn = _tile(n_out, 1536, LANES)
    return pl.pallas_call(
        _cond_kernel,
        grid=(DEPTH, n_out // tn),
        in_specs=[pl.BlockSpec((s, D_MODEL), lambda l, j: (0, 0)),
                  pl.BlockSpec((1, D_MODEL, tn), lambda l, j: (l, 0, j)),
                  pl.BlockSpec((1, 1, tn), lambda l, j: (l, 0, j))],
        out_specs=pl.BlockSpec((1, s, tn), lambda l, j: (l, 0, j)),
        out_shape=jax.ShapeDtypeStruct((DEPTH, s, n_out), F32),
        compiler_params=_params("parallel", "parallel"),
    )(c, w_cond, b_cond.reshape(DEPTH, 1, n_out))[:, :n_seq]


def _rope_tile(x, cos, sinp, sinm):
    half = MLA_ROPE // 2
    return x * cos + pltpu.roll(x, half, 1) * sinp + pltpu.roll(x, LANES - half, 1) * sinm


def _mla_proj_kernel(x_ref, sc_ref, sh_ref, win_ref, qn_ref, kvn_ref, wuq_ref, cos_ref, sinp_ref, sinm_ref,
                     q_ref, lat_ref, kr_ref):
    h = _modulate(x_ref[...], sc_ref[...], sh_ref[...]).astype(BF16)
    down = _dot(h, win_ref[...])
    cq = down[:, :MLA_Q_LORA]
    cq = cq * lax.rsqrt(jnp.mean(cq * cq, -1, keepdims=True) + RMS_EPS) * qn_ref[...]
    lat = down[:, MLA_Q_LORA:MLA_Q_LORA + MLA_KV_LORA]
    lat_ref[...] = lat * lax.rsqrt(jnp.mean(lat * lat, -1, keepdims=True) + RMS_EPS) * kvn_ref[...]
    cos, sinp, sinm = cos_ref[...], sinp_ref[...], sinm_ref[...]
    kr_ref[...] = _rope_tile(down[:, MLA_Q_LORA + MLA_KV_LORA:], cos, sinp, sinm)
    q = _dot(cq.astype(BF16), wuq_ref[...])
    for hd in range(MLA_HEADS):
        sl = slice(hd * LANES, (hd + 1) * LANES)
        q_ref[:, sl] = (_rope_tile(q[:, sl], cos, sinp, sinm) * (MLA_SCALE * LOG2E)).astype(BF16)


def _mla_proj(x, sc, sh, w, rope_tab, n_seq, seq_len):
    n = x.shape[0]
    tm = _row_tile(n_seq, seq_len, 512)
    tab_rows = rope_tab[0].shape[0]
    per = tab_rows // tm
    tab_spec = pl.BlockSpec((tm, LANES), lambda i: (i % per, 0))
    row = lambda w_: pl.BlockSpec((tm, w_), lambda i: (i, 0))
    return pl.pallas_call(
        _mla_proj_kernel,
        grid=(n // tm,),
        in_specs=[row(D_MODEL), _seq_spec(seq_len, tm), _seq_spec(seq_len, tm),
                  _full(w['w_in'].shape), _full((1, MLA_Q_LORA)), _full((1, MLA_KV_LORA)), _full(w['w_uq'].shape),
                  tab_spec, tab_spec, tab_spec],
        out_specs=[row(MLA_HEADS * LANES), row(MLA_KV_LORA), row(LANES)],
        out_shape=[jax.ShapeDtypeStruct((n, MLA_HEADS * LANES), BF16),
                   jax.ShapeDtypeStruct((n, MLA_KV_LORA), F32),
                   jax.ShapeDtypeStruct((n, LANES), F32)],
        compiler_params=_params("parallel"),
    )(x, sc, sh, w['w_in'], w['q_norm'], w['kv_norm'], w['w_uq'], *rope_tab)


def _kv_expand_kernel(lat_ref, kr_ref, wuk_ref, wuv_ref, one_ref, k_ref, v_ref):
    lat = lat_ref[...].astype(BF16)
    kn = _dot(lat, wuk_ref[...])
    kr = kr_ref[...]
    for hd in range(MLA_HEADS):
        sl = slice(hd * LANES, (hd + 1) * LANES)
        k_ref[:, sl] = (kn[:, sl] + kr).astype(BF16)
    v_ref[...] = (_dot(lat, wuv_ref[...]) + one_ref[...]).astype(BF16)


def _kv_expand(lat, kr, w):
    m = lat.shape[0]
    tm = _tile(m, 1024)
    wide = MLA_HEADS * LANES
    row = lambda w_: pl.BlockSpec((tm, w_), lambda i: (i, 0))
    return pl.pallas_call(
        _kv_expand_kernel,
        grid=(m // tm,),
        in_specs=[row(MLA_KV_LORA), row(LANES), _full((MLA_KV_LORA, wide)), _full((MLA_KV_LORA, wide)),
                  _full((1, wide))],
        out_specs=[row(wide), row(wide)],
        out_shape=[jax.ShapeDtypeStruct((m, wide), BF16)] * 2,
        compiler_params=_params("parallel"),
    )(lat, kr, w['w_uk'], w['w_uv'], w['v_one'])


def _attn_kernel(q_ref, k_ref, v_ref, o_ref, *, tq, tk, n_kv, q_off):
    i = pl.program_id(2)
    qpos0 = q_off + i * tq
    n_full = jnp.minimum(((qpos0 // CHUNK) + 1) * CHUNK // tk, n_kv)
    n_end = jnp.minimum((((qpos0 + tq - 1) // CHUNK + 1) * CHUNK + tk - 1) // tk, n_kv)
    lane = lax.broadcasted_iota(I32, (tq, LANES), 1)
    heads = [slice(hh * LANES, (hh + 1) * LANES) for hh in range(2)]
    qs = [q_ref[0, :, sl] for sl in heads]

    def step(j, carry, masked):
        ks = pl.multiple_of(j * tk, 16)
        if masked:
            qc = (qpos0 + lax.broadcasted_iota(I32, (tq, tk), 0)) // CHUNK
            kc = (ks + lax.broadcasted_iota(I32, (tq, tk), 1)) // CHUNK
            visible = kc <= qc
        out = []
        for (m, acc), q, sl in zip(carry, qs, heads):
            s = _dot_nt(q, k_ref[0, pl.ds(ks, tk), sl])
            if masked:
                s = jnp.where(visible, s, -1e30)
            m_new = jnp.maximum(m, jnp.max(s, axis=-1, keepdims=True))
            p = jnp.exp2(s - m_new)
            acc = acc * jnp.exp2(m - m_new) + _dot(p.astype(BF16), v_ref[0, pl.ds(ks, tk), sl])
            out.append((m_new, acc))
        return tuple(out)

    carry = ((jnp.full((tq, 1), -1e30, F32), jnp.zeros((tq, LANES), F32)),) * 2
    carry = lax.fori_loop(0, n_full, functools.partial(step, masked=False), carry)
    carry = lax.fori_loop(n_full, n_end, functools.partial(step, masked=True), carry)
    outs = [acc / jnp.sum(jnp.where(lane == MLA_V, acc, 0.0), axis=-1, keepdims=True) for _, acc in carry]
    o = jnp.where(lane < MLA_V, outs[0], pltpu.roll(outs[1], MLA_V, 1))
    o_ref[0] = o.astype(BF16)


def _attention(q, k, v, tq, tk):
    b, t_q, _ = q.shape
    t_k = k.shape[1]
    kern = functools.partial(_attn_kernel, tq=tq, tk=tk, n_kv=t_k // tk, q_off=t_k - t_q)
    pair = 2 * LANES
    return pl.pallas_call(
        kern,
        grid=(b, MLA_HEADS // 2, t_q // tq),
        in_specs=[pl.BlockSpec((1, tq, pair), lambda b_, p, i: (b_, i, p)),
                  pl.BlockSpec((1, t_k, pair), lambda b_, p, i: (b_, 0, p)),
                  pl.BlockSpec((1, t_k, pair), lambda b_, p, i: (b_, 0, p))],
        out_specs=pl.BlockSpec((1, tq, LANES), lambda b_, p, i: (b_, i, p)),
        out_shape=jax.ShapeDtypeStruct((b, t_q, MLA_HEADS * MLA_V), BF16),
        compiler_params=_params("parallel", "parallel", "arbitrary"),
    )(q, k, v)


def _layernorm(r, g, b):
    mu = jnp.mean(r, -1, keepdims=True)
    d = r - mu
    var = jnp.mean(d * d, -1, keepdims=True)
    return d * lax.rsqrt(var + LN_EPS) * g + b


def _mixer_out_kernel(a_ref, wo_ref, x_ref, gm_ref, lng_ref, lnb_ref, sc_ref, sh_ref, wr_ref, br_ref, up_ref,
                      xo_ref, h_ref, idx_ref, gate_ref, rank_ref, cnt_ref, run_ref):
    @pl.when(pl.program_id(0) == 0)
    def _():
        run_ref[...] = jnp.zeros_like(run_ref)

    y = _dot(a_ref[...], wo_ref[...])
    r = DEEPNORM_ALPHA * x_ref[...] + _rows_scale(y, 1.0 + gm_ref[...])
    xn = _layernorm(r, lng_ref[...], lnb_ref[...])
    xo_ref[...] = xn
    h = _modulate(xn, sc_ref[...], sh_ref[...])
    h_ref[...] = h.astype(BF16)

    logits = _dot3(wr_ref[...], h, _dot_nt) + br_ref[...]
    tm = logits.shape[1]
    eio = lax.broadcasted_iota(I32, (N_EXPERTS, tm), 0).astype(F32)
    sels, vals = [], []
    work = logits
    for k in range(TOP_K):
        m = jnp.max(work, axis=0, keepdims=True)
        ik = jnp.min(jnp.where(work == m, eio, float(N_EXPERTS)), axis=0, keepdims=True)
        sel = eio == ik
        work = jnp.where(sel, -jnp.inf, work)
        idx_ref[pl.ds(k, 1), :] = ik.astype(I32)
        sels.append(sel)
        vals.append(m)
    es = [jnp.exp(v - vals[0]) for v in vals]
    tot = es[0] + es[1] + es[2] + es[3]
    for k in range(TOP_K):
        gate_ref[pl.ds(k, 1), :] = es[k] / tot
    multi = sels[0] | sels[1] | sels[2] | sels[3]
    mh = jnp.where(multi, 1.0, 0.0)
    before = _dot(mh.astype(BF16), up_ref[...]) + run_ref[...]
    for k in range(TOP_K):
        rank_ref[pl.ds(k, 1), :] = jnp.sum(jnp.where(sels[k], before, 0.0), axis=0, keepdims=True).astype(I32)
    run = run_ref[...] + jnp.sum(mh, axis=1, keepdims=True)
    run_ref[...] = run
    cnt_ref[...] = run


def _mixer_out(a, w_o, x, gm, lng, lnb, sc, sh, w_r_t, b_r, n_seq, seq_len):
    n, kdim = a.shape
    tm = _row_tile(n_seq, seq_len, 512)
    upper = (jnp.arange(tm)[:, None] < jnp.arange(tm)[None, :]).astype(BF16)
    row = lambda w_: pl.BlockSpec((tm, w_), lambda i: (i, 0))
    col = pl.BlockSpec((TOP_K, tm), lambda i: (0, i))
    ss = _seq_spec(seq_len, tm)
    return pl.pallas_call(
        _mixer_out_kernel,
        grid=(n // tm,),
        in_specs=[row(kdim), _full(w_o.shape), row(D_MODEL), ss, _full((1, D_MODEL)), _full((1, D_MODEL)), ss, ss,
                  _full((N_EXPERTS, D_MODEL)), _full((N_EXPERTS, 1)), _full((tm, tm))],
        out_specs=[row(D_MODEL), row(D_MODEL), col, col, col, _full((N_EXPERTS, 1))],
        out_shape=[jax.ShapeDtypeStruct((n, D_MODEL), F32), jax.ShapeDtypeStruct((n, D_MODEL), BF16),
                   jax.ShapeDtypeStruct((TOP_K, n), I32), jax.ShapeDtypeStruct((TOP_K, n), F32),
                   jax.ShapeDtypeStruct((TOP_K, n), I32), jax.ShapeDtypeStruct((N_EXPERTS, 1), F32)],
        scratch_shapes=[pltpu.VMEM((N_EXPERTS, 1), F32)],
        compiler_params=_params("arbitrary"),
    )(a, w_o, x, gm, lng, lnb, sc, sh, w_r_t, b_r, upper)


def _expert_kernel(be_ref, na_ref, x_ref, wgu_ref, bgu_ref, wd_ref, bd_ref, y_ref, wgu_s, wd_s):
    i = pl.program_id(0)

    @pl.when((i == 0) | (be_ref[i] != be_ref[jnp.maximum(i - 1, 0)]))
    def _():
        wgu_s[...] = wgu_ref[0].astype(BF16)
        wd_s[...] = wd_ref[0].astype(BF16)

    @pl.when(i < na_ref[0])
    def _():
        gu = _dot(x_ref[...], wgu_s[...]) + bgu_ref[0]
        gate = jnp.minimum(gu[:, :D_FF], SWIGLU_LIMIT)
        up = jnp.clip(gu[:, D_FF:], -SWIGLU_LIMIT, SWIGLU_LIMIT)
        act = (up + 1.0) * gate * _sigmoid(SWIGLU_ALPHA * gate)
        y_ref[...] = _dot(act.astype(BF16), wd_s[...]) + bd_ref[0]

    @pl.when(i >= na_ref[0])
    def _():
        y_ref[...] = jnp.zeros_like(y_ref)


def _experts(x_rows, blk_e, n_act, w_gu, b_gu, w_down, b_down, layer):
    p = x_rows.shape[0]
    grid_spec = pltpu.PrefetchScalarGridSpec(
        num_scalar_prefetch=2,
        grid=(p // MOE_ROWS,),
        in_specs=[pl.BlockSpec((MOE_ROWS, D_MODEL), lambda i, be, na: (i, 0)),
                  pl.BlockSpec((None, 1, D_MODEL, 2 * D_FF), lambda i, be, na: (layer, be[i], 0, 0)),
                  pl.BlockSpec((1, 1, 2 * D_FF), lambda i, be, na: (be[i], 0, 0)),
                  pl.BlockSpec((None, 1, D_FF, D_MODEL), lambda i, be, na: (layer, be[i], 0, 0)),
                  pl.BlockSpec((1, 1, D_MODEL), lambda i, be, na: (be[i], 0, 0))],
        out_specs=pl.BlockSpec((MOE_ROWS, D_MODEL), lambda i, be, na: (i, 0)),
        scratch_shapes=[pltpu.VMEM((D_MODEL, 2 * D_FF), BF16), pltpu.VMEM((D_FF, D_MODEL), BF16)],
    )
    return pl.pallas_call(
        _expert_kernel,
        grid_spec=grid_spec,
        out_shape=jax.ShapeDtypeStruct((p, D_MODEL), F32),
        compiler_params=_params("arbitrary"),
    )(blk_e, n_act, x_rows, w_gu, b_gu, w_down, b_down)


def _combine_kernel(y_ref, g_ref, x_ref, gf_ref, lng_ref, lnb_ref, o_ref):
    g = g_ref[...]
    y = y_ref[0] * g[:, 0:1]
    for k in range(1, TOP_K):
        y = y + y_ref[k] * g[:, k:k + 1]
    r = DEEPNORM_ALPHA * x_ref[...] + _rows_scale(y, 1.0 + gf_ref[...])
    o_ref[...] = _layernorm(r, lng_ref[...], lnb_ref[...])


def _combine(y4, gates, x, gf, lng, lnb, n_seq, seq_len):
    n = x.shape[0]
    tm = _row_tile(n_seq, seq_len, 512)
    row = pl.BlockSpec((tm, D_MODEL), lambda i: (i, 0))
    return pl.pallas_call(
        _combine_kernel,
        grid=(n // tm,),
        in_specs=[pl.BlockSpec((TOP_K, tm, D_MODEL), lambda i: (0, i, 0)),
                  pl.BlockSpec((tm, TOP_K), lambda i: (i, 0)), row, _seq_spec(seq_len, tm),
                  _full((1, D_MODEL)), _full((1, D_MODEL))],
        out_specs=row,
        out_shape=jax.ShapeDtypeStruct((n, D_MODEL), F32),
        compiler_params=_params("parallel"),
    )(y4, gates, x, gf, lng, lnb)


def _moe_and_norm(x, h, idx_t, gate_t, rank_t, cnt, gf, lng, lnb, w, n_seq, seq_len):
    n = x.shape[0]
    a = n * TOP_K
    counts = cnt[:, 0].astype(I32)
    padded = (counts + MOE_ROWS - 1) // MOE_ROWS * MOE_ROWS
    pad_end = jnp.cumsum(padded)
    pad_start = pad_end - padded
    dest_t = pad_start[idx_t] + rank_t
    nb = (a + N_EXPERTS * (MOE_ROWS - 1) + MOE_ROWS - 1) // MOE_ROWS
    p = nb * MOE_ROWS
    tok = jnp.broadcast_to(jnp.arange(n, dtype=I32)[None], (TOP_K, n))
    row_tok = jnp.zeros((p,), I32).at[dest_t.reshape(-1)].set(
        tok.reshape(-1), mode='promise_in_bounds', unique_indices=True)
    x_rows = h.at[row_tok].get(mode='promise_in_bounds')
    blk_row = jnp.arange(nb, dtype=I32) * MOE_ROWS
    blk_e = jnp.minimum(jnp.sum(pad_end[None, :] <= blk_row[:, None], axis=1), N_EXPERTS - 1).astype(I32)
    n_act = (pad_end[-1:] // MOE_ROWS).astype(I32)
    y_rows = _experts(x_rows, blk_e, n_act, w['w_gu'], w['b_gu'], w['w_down'], w['b_down'], w['layer'])
    y4 = y_rows.at[dest_t].get(mode='promise_in_bounds')
    return _combine(y4, gate_t.T, x, gf, lng, lnb, n_seq, seq_len)


def _mod_matmul_kernel(x_ref, sc_ref, sh_ref, w_ref, o_ref):
    h = _modulate(x_ref[...], sc_ref[...], sh_ref[...]).astype(BF16)
    o_ref[...] = _dot(h, w_ref[...]).astype(o_ref.dtype)


def _mod_matmul(x, sc, sh, w, out_dtype, n_seq, seq_len):
    n = x.shape[0]
    n_out = w.shape[1]
    tm = _row_tile(n_seq, seq_len, 1024)
    tn = _tile(n_out, 1024, LANES)
    return pl.pallas_call(
        _mod_matmul_kernel,
        grid=(n // tm, n_out // tn),
        in_specs=[pl.BlockSpec((tm, D_MODEL), lambda i, j: (i, 0)), _seq_spec(seq_len, tm), _seq_spec(seq_len, tm),
                  pl.BlockSpec((D_MODEL, tn), lambda i, j: (0, j))],
        out_specs=pl.BlockSpec((tm, tn), lambda i, j: (i, j)),
        out_shape=jax.ShapeDtypeStruct((n, n_out), out_dtype),
        compiler_params=_params("parallel", "arbitrary"),
    )(x, sc, sh, w)


def _gdn_gates_kernel(x_ref, sc_ref, sh_ref, wab_ref, wabt_ref, alog_ref, dtb_ref, alogt_ref, dtbt_ref,
                      tri_ref, trit_ref, gc_ref, beta_ref, gct_ref):
    h = _modulate(x_ref[...], sc_ref[...], sh_ref[...]).astype(BF16)
    hv = GDN_V_HEADS
    ab = _dot(h, wab_ref[...])
    beta_ref[...] = _sigmoid(ab[:, :hv])
    g = -jnp.exp(alog_ref[...]) * _softplus(ab[:, hv:] + dtb_ref[...])
    gc_ref[...] = _dot_exact_rhs_lhs(tri_ref[...], g)
    abt = _dot_nt(wabt_ref[...], h)
    gt = -jnp.exp(alogt_ref[...]) * _softplus(abt + dtbt_ref[...])
    gct_ref[...] = _dot_exact_rhs(gt, trit_ref[...])


def _dot_exact_rhs_lhs(a01, b):
    b1 = b.astype(BF16)
    r1 = b - b1.astype(F32)
    b2 = r1.astype(BF16)
    b3 = (r1 - b2.astype(F32)).astype(BF16)
    return _dot(a01, b1) + (_dot(a01, b2) + _dot(a01, b3))


def _gdn_gates(x, sc, sh, w, n_seq, seq_len, chunk):
    n = x.shape[0]
    tm = _row_tile(n_seq, seq_len, 512)
    hv = GDN_V_HEADS
    r = jnp.arange(tm)
    tri = ((r[:, None] // chunk == r[None, :] // chunk) & (r[None, :] <= r[:, None])).astype(BF16)
    row = lambda w_: pl.BlockSpec((tm, w_), lambda i: (i, 0))
    return pl.pallas_call(
        _gdn_gates_kernel,
        grid=(n // tm,),
        in_specs=[row(D_MODEL), _seq_spec(seq_len, tm), _seq_spec(seq_len, tm),
                  _full((D_MODEL, 2 * hv)), _full((hv, D_MODEL)), _full((1, hv)), _full((1, hv)),
                  _full((hv, 1)), _full((hv, 1)), _full((tm, tm)), _full((tm, tm))],
        out_specs=[row(hv), row(hv), pl.BlockSpec((hv, tm), lambda i: (0, i))],
        out_shape=[jax.ShapeDtypeStruct((n, hv), F32), jax.ShapeDtypeStruct((n, hv), F32),
                   jax.ShapeDtypeStruct((hv, n), F32)],
        compiler_params=_params("parallel"),
    )(x, sc, sh, w['w_ab'], w['w_a_t'], w['a_log'], w['dt_bias'], w['a_log'].T, w['dt_bias'].T, tri, tri.T)


def _conv_kernel(x_ref, prev_ref, st_ref, w_ref, o_ref, *, per, norm, n_q_blocks):
    i = pl.program_id(0)
    j = pl.program_id(1)
    x = x_ref[...]
    tm = x.shape[0]
    halo = jnp.where(i % per == 0, st_ref[0], prev_ref[...])
    w = w_ref[...]
    rows = lax.broadcasted_iota(I32, (HALO, x.shape[1]), 0)
    acc = x * w[GDN_CONV_W - 1:GDN_CONV_W]
    for s in range(1, GDN_CONV_W):
        xs = pltpu.roll(x, s, 0)
        head = jnp.where(rows < s, pltpu.roll(halo, s, 0), xs[:HALO])
        xs = jnp.concatenate([head, xs[HALO:]], axis=0) if tm > HALO else head
        acc = acc + xs * w[GDN_CONV_W - 1 - s:GDN_CONV_W - s]
    y = _silu(acc)
    if norm:
        scale = jnp.where(j < n_q_blocks, GDN_DK ** -0.5, 1.0)
        for hd in range(y.shape[1] // GDN_DK):
            sl = slice(hd * GDN_DK, (hd + 1) * GDN_DK)
            yh = y[:, sl]
            o_ref[:, sl] = yh * (lax.rsqrt(jnp.sum(yh * yh, -1, keepdims=True) + L2_EPS) * scale)
    else:
        o_ref[...] = y


def _gdn_conv(qkv, state8, conv_w, n_seq, seq_len, col0, width, norm):
    n = qkv.shape[0]
    tm = _tile(seq_len, 512)
    per = seq_len // tm
    tn = 512
    c0 = col0 // tn
    kern = functools.partial(_conv_kernel, per=per, norm=norm, n_q_blocks=GDN_QK_DIM // tn)
    hb = tm // HALO
    return pl.pallas_call(
        kern,
        grid=(n // tm, width // tn),
        in_specs=[pl.BlockSpec((tm, tn), lambda i, j: (i, c0 + j)),
                  pl.BlockSpec((HALO, tn), lambda i, j: (jnp.maximum(i * hb - 1, 0), c0 + j)),
                  pl.BlockSpec((1, HALO, tn), lambda i, j: (i // per, 0, c0 + j)),
                  pl.BlockSpec((GDN_CONV_W, tn), lambda i, j: (0, c0 + j))],
        out_specs=pl.BlockSpec((tm, tn), lambda i, j: (i, j)),
        out_shape=jax.ShapeDtypeStruct((n, width), F32),
        compiler_params=_params("parallel", "parallel"),
    )(qkv, qkv, state8, conv_w)


def _inv_masks(c):
    row = lax.broadcasted_iota(I32, (c, c), 0)
    col = lax.broadcasted_iota(I32, (c, c), 1)
    eye = jnp.where(row == col, 1.0, 0.0)
    diag = row // INV_BASE == col // INV_BASE
    offs = []
    size = INV_BASE
    while size < c:
        offs.append((row // (2 * size) == col // (2 * size)) & (row // size % 2 == 1) & (col // size % 2 == 0))
        size *= 2
    return eye, diag, offs


def _inv_unit_lower(lmats, masks):
    eye, diag, offs = masks
    pws = [jnp.where(diag, lm, 0.0) for lm in lmats]
    ts = [eye - pw for pw in pws]
    size = 2
    while size < INV_BASE:
        pws = [_dotb(pw, pw) for pw in pws]
        ts = [t + _dotb(t, pw) for t, pw in zip(ts, pws)]
        size *= 2
    for off in offs:
        tbs = [_dotb(t, jnp.where(off, lm, 0.0)) for t, lm in zip(ts, lmats)]
        ts = [t - _dotb(tb, t) for t, tb in zip(ts, tbs)]
    return ts


def _gdn_core_kernel(q_ref, k_ref, v_ref, z_ref, gc_ref, beta_ref, gct_ref, s0_ref, nw_ref, o_ref, s_ref, *, c, hb):
    @pl.when(pl.program_id(2) == 0)
    def _():
        s_ref[...] = s0_ref[...]

    row = lax.broadcasted_iota(I32, (c, c), 0)
    col = lax.broadcasted_iota(I32, (c, c), 1)
    lower = col <= row
    strict = col < row
    masks = _inv_masks(c)
    heads = range(hb)
    ksl = [slice(kh * GDN_DK, (kh + 1) * GDN_DK) for kh in range(hb // 2)]
    vsl = [slice(h * GDN_DV, (h + 1) * GDN_DV) for h in heads]
    qn = [q_ref[0, :, sl] for sl in ksl]
    kn = [k_ref[0, :, sl] for sl in ksl]
    kb = [k.astype(BF16) for k in kn]
    kk = [_dot_nt(k, k) for k in kb]
    qk = [_dot_nt(q.astype(BF16), k) for q, k in zip(qn, kb)]
    gcc = [gc_ref[0, 0, :, h:h + 1] for h in heads]
    bc = [beta_ref[0, 0, :, h:h + 1] for h in heads]
    decay = [jnp.where(lower, jnp.exp(jnp.minimum(gcc[h] - gct_ref[0, 0, h:h + 1, :], 0.0)), 0.0) for h in heads]
    lmat = [jnp.where(strict, kk[h // 2] * decay[h], 0.0) * bc[h] for h in heads]
    ts = _inv_unit_lower(lmat, masks)
    egc = [jnp.exp(g) for g in gcc]
    rhs = [jnp.concatenate([v_ref[0, :, vsl[h]] * bc[h], kn[h // 2] * (bc[h] * egc[h])], axis=1).astype(BF16)
           for h in heads]
    uw = [_dot(ts[h].astype(BF16), rhs[h]) for h in heads]
    s_old = [s_ref[0, h] for h in heads]
    sb = [s.astype(BF16) for s in s_old]
    ws = [_dot(uw[h][:, GDN_DV:].astype(BF16), sb[h]) for h in heads]
    qs = [_dot((qn[h // 2] * egc[h]).astype(BF16), sb[h]) for h in heads]
    vb = [(uw[h][:, :GDN_DV] - ws[h]).astype(BF16) for h in heads]
    o = [qs[h] + _dot((qk[h // 2] * decay[h]).astype(BF16), vb[h]) for h in heads]
    glast = [g[c - 1:c, :] for g in gcc]
    kd = [(kn[h // 2] * jnp.exp(glast[h] - gcc[h])).astype(BF16) for h in heads]
    s_new = [s_old[h] * jnp.exp(glast[h]) + _dot_tn(kd[h], vb[h]) for h in heads]
    for h in heads:
        s_ref[0, h] = s_new[h]
        z = z_ref[0, :, vsl[h]].astype(F32)
        on = o[h] * lax.rsqrt(jnp.mean(o[h] * o[h], -1, keepdims=True) + RMS_EPS) * nw_ref[...]
        o_ref[0, :, vsl[h]] = (on * _silu(z)).astype(BF16)


def _gdn_core(qk, v, z, gc, beta, gct, s0, norm_w, c, hb=GDN_V_HEADS):
    b, t, _ = v.shape
    ng = GDN_V_HEADS // hb
    kw = hb // 2 * GDN_DK
    vw = hb * GDN_DV
    gc_g = gc.reshape(b, t, ng, hb).transpose(0, 2, 1, 3)
    beta_g = beta.reshape(b, t, ng, hb).transpose(0, 2, 1, 3)
    gct_g = gct.reshape(b, ng, hb, t)
    kern = functools.partial(_gdn_core_kernel, c=c, hb=hb)
    nkb = GDN_QK_DIM // kw
    return pl.pallas_call(
        kern,
        grid=(b, ng, t // c),
        in_specs=[pl.BlockSpec((1, c, kw), lambda b_, g, i: (b_, i, g)),
                  pl.BlockSpec((1, c, kw), lambda b_, g, i: (b_, i, nkb + g)),
                  pl.BlockSpec((1, c, vw), lambda b_, g, i: (b_, i, g)),
                  pl.BlockSpec((1, c, vw), lambda b_, g, i: (b_, i, g)),
                  pl.BlockSpec((1, 1, c, hb), lambda b_, g, i: (b_, g, i, 0)),
                  pl.BlockSpec((1, 1, c, hb), lambda b_, g, i: (b_, g, i, 0)),
                  pl.BlockSpec((1, 1, hb, c), lambda b_, g, i: (b_, g, 0, i)),
                  pl.BlockSpec((1, hb, GDN_DK, GDN_DV), lambda b_, g, i: (b_, g, 0, 0)),
                  _full((1, GDN_DV))],
        out_specs=[pl.BlockSpec((1, c, vw), lambda b_, g, i: (b_, i, g)),
                   pl.BlockSpec((1, hb, GDN_DK, GDN_DV), lambda b_, g, i: (b_, g, 0, 0))],
        out_shape=[jax.ShapeDtypeStruct((b, t, GDN_V_DIM), BF16),
                   jax.ShapeDtypeStruct((b, GDN_V_HEADS, GDN_DK, GDN_DV), F32)],
        compiler_params=_params("parallel", "parallel", "arbitrary"),
    )(qk, qk, v, z, gc_g, beta_g, gct_g, s0, norm_w)


def _prep_weights(p):
    f = {}
    pad_heads = lambda w_, dh: jnp.pad(w_.reshape(w_.shape[0], w_.shape[1], MLA_HEADS, dh),
                                       ((0, 0), (0, 0), (0, 0), (0, LANES - dh))).reshape(
                                           w_.shape[0], w_.shape[1], MLA_HEADS * LANES)
    w_in = p['mla_w_in']
    nl = w_in.shape[0]
    lo = MLA_Q_LORA + MLA_KV_LORA
    z = lambda k: jnp.zeros((nl, D_MODEL, k), F32)
    f['mla_w_in'] = jnp.concatenate([w_in[..., :lo], z(MLA_NOPE), w_in[..., lo:], z(LANES - MLA_QK)], -1).astype(BF16)
    f['mla_w_uq'] = pad_heads(p['mla_w_uq'], MLA_QK).astype(BF16)
    f['mla_w_uk'] = pad_heads(p['mla_w_uk'], MLA_NOPE).astype(BF16)
    f['mla_w_uv'] = pad_heads(p['mla_w_uv'], MLA_V).astype(BF16)
    f['mla_w_o'] = p['mla_w_o'].astype(BF16)
    f['v_one'] = jnp.tile((jnp.arange(LANES) == MLA_V).astype(F32), MLA_HEADS)[None]
    g_in = p['gdn_w_in']
    f['gdn_w_qkv'] = g_in[..., :GDN_CONV_DIM].astype(BF16)
    f['gdn_w_z'] = g_in[..., GDN_CONV_DIM:GDN_CONV_DIM + GDN_V_DIM].astype(BF16)
    f['gdn_w_ab'] = g_in[..., GDN_CONV_DIM + GDN_V_DIM:].astype(BF16)
    f['gdn_w_a_t'] = jnp.swapaxes(g_in[..., GDN_CONV_DIM + GDN_V_DIM + GDN_V_HEADS:], 1, 2).astype(BF16)
    f['gdn_w_o'] = p['gdn_w_o'].astype(BF16)
    f['moe_w_r_t'] = jnp.swapaxes(p['moe_w_router'], 1, 2)
    return f


def _rope_tables(pos):
    half = MLA_ROPE // 2
    inv_freq = ROPE_BASE ** (-jnp.arange(half, dtype=F32) / half)
    ang = pos.astype(F32)[:, None] * inv_freq[None, :]
    cos, sin = jnp.cos(ang), jnp.sin(ang)
    t = pos.shape[0]
    z = lambda k: jnp.zeros((t, k), F32)
    cos_t = jnp.concatenate([jnp.ones((t, MLA_NOPE), F32), cos, cos, z(LANES - MLA_QK)], -1)
    sinp_t = jnp.concatenate([z(MLA_NOPE + half), sin, z(LANES - MLA_QK)], -1)
    sinm_t = jnp.concatenate([z(MLA_NOPE), -sin, z(half + LANES - MLA_QK)], -1)
    return cos_t, sinp_t, sinm_t


def _trunk(x, c, pos, past, p, f):
    bsz, t, _ = x.shape
    n = bsz * t
    x = x.reshape(n, D_MODEL)
    mod = _cond(c, p['w_cond'], p['b_cond']).reshape(DEPTH, bsz, 6, 1, D_MODEL)
    tm_rope = _row_tile(bsz, t, 512)
    tabs = _rope_tables(pos)
    if tm_rope > t:
        tabs = tuple(jnp.tile(tb, (tm_rope // t, 1)) for tb in tabs)
    lats, krs, convs, ssms = [], [], [], []
    for layer in range(DEPTH):
        sh_m, sc_m, g_m, sh_f, sc_f, g_f = [mod[layer, :, i] for i in range(6)]
        j = layer // 2
        if layer % 2 == 0:
            w = dict(w_in=f['mla_w_in'][j], q_norm=p['mla_q_norm'][j][None], kv_norm=p['mla_kv_norm'][j][None],
                     w_uq=f['mla_w_uq'][j], w_uk=f['mla_w_uk'][j], w_uv=f['mla_w_uv'][j], v_one=f['v_one'])
            q, lat, krt = _mla_proj(x, sc_m, sh_m, w, tabs, bsz, t)
            lats.append(lat.reshape(bsz, t, MLA_KV_LORA))
            krs.append(krt[:, MLA_NOPE:MLA_QK].reshape(bsz, t, MLA_ROPE))
            if past is None:
                lat_all, kr_all, t_k = lat, krt, t
            else:
                p_lat, p_kr = past[0][j], past[1][j]
                t_k = p_lat.shape[1] + t
                p_krt = jnp.pad(p_kr, ((0, 0), (0, 0), (MLA_NOPE, LANES - MLA_QK)))
                lat_all = jnp.concatenate([p_lat, lat.reshape(bsz, t, -1)], 1).reshape(bsz * t_k, MLA_KV_LORA)
                kr_all = jnp.concatenate([p_krt, krt.reshape(bsz, t, -1)], 1).reshape(bsz * t_k, LANES)
            k, v = _kv_expand(lat_all, kr_all, w)
            wide = MLA_HEADS * LANES
            tq = _tile(t, 512)
            tk = tq if past is None else t_k
            ctx = _attention(q.reshape(bsz, t, wide), k.reshape(bsz, t_k, wide), v.reshape(bsz, t_k, wide), tq, tk)
            a, w_o = ctx.reshape(n, MLA_HEADS * MLA_V), f['mla_w_o'][j]
        else:
            chunk = _tile(t, 128)
            w = dict(w_ab=f['gdn_w_ab'][j], w_a_t=f['gdn_w_a_t'][j], a_log=p['gdn_a_log'][j][None],
                     dt_bias=p['gdn_dt_bias'][j][None])
            qkv = _mod_matmul(x, sc_m, sh_m, f['gdn_w_qkv'][j], F32, bsz, t)
            z = _mod_matmul(x, sc_m, sh_m, f['gdn_w_z'][j], BF16, bsz, t)
            gc, beta, gct = _gdn_gates(x, sc_m, sh_m, w, bsz, t, chunk)
            if past is None:
                conv_state = jnp.zeros((bsz, GDN_CONV_W - 1, GDN_CONV_DIM), F32)
                s0 = jnp.zeros((bsz, GDN_V_HEADS, GDN_DK, GDN_DV), F32)
            else:
                conv_state, s0 = past[2][j], past[3][j]
            state8 = jnp.pad(conv_state, ((0, 0), (HALO - (GDN_CONV_W - 1), 0), (0, 0)))
            qkv3 = qkv.reshape(bsz, t, GDN_CONV_DIM)
            convs.append(qkv3[:, -(GDN_CONV_W - 1):])
            cw = p['gdn_conv_w'][j]
            qk_n = _gdn_conv(qkv, state8, cw, bsz, t, 0, 2 * GDN_QK_DIM, True)
            v_c = _gdn_conv(qkv, state8, cw, bsz, t, 2 * GDN_QK_DIM, GDN_V_DIM, False)
            o, s_new = _gdn_core(qk_n.reshape(bsz, t, -1), v_c.reshape(bsz, t, -1), z.reshape(bsz, t, -1),
                                 gc.reshape(bsz, t, -1), beta.reshape(bsz, t, -1),
                                 gct.reshape(GDN_V_HEADS, bsz, t).transpose(1, 0, 2), s0,
                                 p['gdn_norm'][j][None], chunk)
            ssms.append(s_new)
            a, w_o = o.reshape(n, GDN_V_DIM), f['gdn_w_o'][j]
        x, h, idx_t, gate_t, rank_t, cnt = _mixer_out(
            a, w_o, x, g_m, p['ln1_g'][layer][None], p['ln1_b'][layer][None], sc_f, sh_f,
            f['moe_w_r_t'][layer], p['moe_b_router'][layer][:, None], bsz, t)
        wm = dict(w_gu=p['moe_w_gu'], b_gu=p['moe_b_gu'][layer][:, None], w_down=p['moe_w_down'],
                  b_down=p['moe_b_down'][layer][:, None], layer=layer)
        x = _moe_and_norm(x, h, idx_t, gate_t, rank_t, cnt, g_f, p['ln2_g'][layer][None], p['ln2_b'][layer][None],
                          wm, bsz, t)
    return x.reshape(bsz, t, D_MODEL), jnp.stack(lats), jnp.stack(krs), jnp.stack(convs), jnp.stack(ssms)


def kernel(x_prompt, x_sample, c_prompt, c_sample, cache_mla_latent, cache_mla_krope, state_gdn_conv, state_gdn_ssm, w_cond, b_cond, ln1_g, ln1_b, ln2_g, ln2_b, mla_w_in, mla_q_norm, mla_kv_norm, mla_w_uq, mla_w_uk, mla_w_uv, mla_w_o, gdn_w_in, gdn_conv_w, gdn_a_log, gdn_dt_bias, gdn_norm, gdn_w_o, moe_w_router, moe_b_router, moe_w_gu, moe_b_gu, moe_w_down, moe_b_down):
    p = dict(w_cond=w_cond, b_cond=b_cond, ln1_g=ln1_g, ln1_b=ln1_b, ln2_g=ln2_g, ln2_b=ln2_b,
             mla_w_in=mla_w_in, mla_q_norm=mla_q_norm, mla_kv_norm=mla_kv_norm, mla_w_uq=mla_w_uq,
             mla_w_uk=mla_w_uk, mla_w_uv=mla_w_uv, mla_w_o=mla_w_o, gdn_w_in=gdn_w_in,
             gdn_conv_w=gdn_conv_w, gdn_a_log=gdn_a_log, gdn_dt_bias=gdn_dt_bias, gdn_norm=gdn_norm,
             gdn_w_o=gdn_w_o, moe_w_router=moe_w_router, moe_b_router=moe_b_router, moe_w_gu=moe_w_gu,
             moe_b_gu=moe_b_gu, moe_w_down=moe_w_down, moe_b_down=moe_b_down)
    f = _prep_weights(p)
    past_len = cache_mla_latent.shape[2]
    assert past_len % CHUNK == 0 and x_sample.shape[1] <= CHUNK
    pos_p = jnp.arange(x_prompt.shape[1], dtype=I32)
    pos_s = past_len + jnp.arange(x_sample.shape[1], dtype=I32)
    y_p, p_lat, p_kr, p_conv, p_ssm = _trunk(x_prompt, c_prompt, pos_p, None, p, f)
    past = (cache_mla_latent, cache_mla_krope, state_gdn_conv, state_gdn_ssm)
    y_s, s_lat, s_kr, s_conv, s_ssm = _trunk(x_sample, c_sample, pos_s, past, p, f)
    return (y_p, y_s, p_lat, p_kr, p_conv, p_ssm, s_lat, s_kr, s_conv, s_ssm)
```

```python
import functools
import math

import jax
import jax.numpy as jnp
from jax import lax
from jax.experimental import pallas as pl
from jax.experimental.pallas import tpu as pltpu

F32 = jnp.float32
BF16 = jnp.bfloat16
I32 = jnp.int32

D_MODEL = 1024
DEPTH = 4
CHUNK = 64
MLA_HEADS = 16
MLA_Q_LORA = 768
MLA_KV_LORA = 256
MLA_NOPE = 64
MLA_ROPE = 32
MLA_V = 64
MLA_QK = MLA_NOPE + MLA_ROPE
MLA_SCALE = MLA_QK ** -0.5
ROPE_BASE = 10000.0
GDN_QK_HEADS = 8
GDN_V_HEADS = 16
GDN_DK = 128
GDN_DV = 128
GDN_QK_DIM = GDN_QK_HEADS * GDN_DK
GDN_V_DIM = GDN_V_HEADS * GDN_DV
GDN_CONV_DIM = 2 * GDN_QK_DIM + GDN_V_DIM
GDN_CONV_W = 4
N_EXPERTS = 32
TOP_K = 4
D_FF = D_MODEL
SWIGLU_ALPHA = 1.702
SWIGLU_LIMIT = 7.0
DEEPNORM_ALPHA = (2 * DEPTH) ** 0.25
LN_EPS = 1e-5
RMS_EPS = 1e-6
L2_EPS = 1e-6

LANES = 128
HALO = 8
LOG2E = 1.4426950408889634
MOE_ROWS = 256
ROUTE_TILE = 512
SEG = 16
INV_BASE = 16
VMEM_LIMIT = 48 * 1024 * 1024


def _params(*sem):
    return pltpu.CompilerParams(dimension_semantics=sem, vmem_limit_bytes=VMEM_LIMIT)


def _tile(n, pref, mult=8):
    t = min(pref, n)
    while t >= mult:
        if n % t == 0 and t % mult == 0:
            return t
        t -= 1
    return n


def _dot(a, b):
    return jnp.dot(a, b, preferred_element_type=F32)


def _dot_nt(a, b):
    return lax.dot_general(a, b, (((1,), (1,)), ((), ())), preferred_element_type=F32)


def _dot_tn(a, b):
    return lax.dot_general(a, b, (((0,), (0,)), ((), ())), preferred_element_type=F32)


def _dotb(a, b):
    return _dot(a.astype(BF16), b.astype(BF16))


def _split(a):
    hi = a.astype(BF16)
    lo = (a - hi.astype(F32)).astype(BF16)
    return hi, lo


def _dot3(a, b, dot=_dot):
    ah, al = _split(a)
    bh, bl = _split(b)
    return dot(ah, bh) + (dot(ah, bl) + dot(al, bh))


def _dot_exact_rhs(a, b01, dot=_dot):
    a1 = a.astype(BF16)
    r1 = a - a1.astype(F32)
    a2 = r1.astype(BF16)
    a3 = (r1 - a2.astype(F32)).astype(BF16)
    return dot(a1, b01) + (dot(a2, b01) + dot(a3, b01))


def _sigmoid(x):
    return 1.0 / (1.0 + jnp.exp(-x))


def _silu(x):
    return x * _sigmoid(x)


def _softplus(x):
    return jnp.maximum(x, 0.0) + jnp.log(1.0 + jnp.exp(-jnp.abs(x)))


def _rows_scale(x, s):
    spt = s.shape[0]
    if spt == 1:
        return x * s[0]
    tm, d = x.shape
    return (x.reshape(spt, tm // spt, d) * s).reshape(tm, d)


def _rows_add(x, s):
    spt = s.shape[0]
    if spt == 1:
        return x + s[0]
    tm, d = x.shape
    return (x.reshape(spt, tm // spt, d) + s).reshape(tm, d)


def _modulate(x, sc, sh):
    return _rows_add(_rows_scale(x, 1.0 + sc), sh)


def _seq_spec(seq_len, tm, d=D_MODEL):
    if tm <= seq_len:
        per = seq_len // tm
        return pl.BlockSpec((1, 1, d), lambda i, *_: (i // per, 0, 0))
    return pl.BlockSpec((tm // seq_len, 1, d), lambda i, *_: (i, 0, 0))


def _row_tile(n_seq, seq_len, pref):
    if seq_len >= pref:
        return _tile(seq_len, pref)
    spt = _tile(n_seq, max(pref // seq_len, 1), mult=1)
    return spt * seq_len


def _full(shape):
    nd = len(shape)
    return pl.BlockSpec(shape, lambda *_: (0,) * nd)


def _cond_kernel(c_ref, w_ref, b_ref, o_ref):
    c = c_ref[...]
    o_ref[0] = _dot3(_silu(c), w_ref[0]) + b_ref[0]


def _cond(c, w_cond, b_cond):
    n_seq = c.shape[0]
    c = jnp.pad(c, ((0, -n_seq % HALO), (0, 0)))
    s = c.shape[0]
    n_out = w_cond.shape[-1]
    tn = _tile(n_out, 1536, LANES)
    return pl.pallas_call(
        _cond_kernel,
        grid=(DEPTH, n_out // tn),
        in_specs=[pl.BlockSpec((s, D_MODEL), lambda l, j: (0, 0)),
                  pl.BlockSpec((1, D_MODEL, tn), lambda l, j: (l, 0, j)),
                  pl.BlockSpec((1, 1, tn), lambda l, j: (l, 0, j))],
        out_specs=pl.BlockSpec((1, s, tn), lambda l, j: (l, 0, j)),
        out_shape=jax.ShapeDtypeStruct((DEPTH, s, n_out), F32),
        compiler_params=_params("parallel", "parallel"),
    )(c, w_cond, b_cond.reshape(DEPTH, 1, n_out))[:, :n_seq]


def _rope_tile(x, cos, sinp, sinm):
    half = MLA_ROPE // 2
    return x * cos + pltpu.roll(x, half, 1) * sinp + pltpu.roll(x, LANES - half, 1) * sinm


def _mla_proj_kernel(x_ref, sc_ref, sh_ref, win_ref, qn_ref, kvn_ref, wuq_ref, cos_ref, sinp_ref, sinm_ref,
                     q_ref, lat_ref, kr_ref):
    h = _modulate(x_ref[...], sc_ref[...], sh_ref[...]).astype(BF16)
    down = _dot(h, win_ref[...])
    cq = down[:, :MLA_Q_LORA]
    cq = cq * lax.rsqrt(jnp.mean(cq * cq, -1, keepdims=True) + RMS_EPS) * qn_ref[...]
    lat = down[:, MLA_Q_LORA:MLA_Q_LORA + MLA_KV_LORA]
    lat_ref[...] = lat * lax.rsqrt(jnp.mean(lat * lat, -1, keepdims=True) + RMS_EPS) * kvn_ref[...]
    cos, sinp, sinm = cos_ref[...], sinp_ref[...], sinm_ref[...]
    kr_ref[...] = _rope_tile(down[:, MLA_Q_LORA + MLA_KV_LORA:], cos, sinp, sinm)
    q = _dot(cq.astype(BF16), wuq_ref[...])
    for hd in range(MLA_HEADS):
        sl = slice(hd * LANES, (hd + 1) * LANES)
        q_ref[:, sl] = (_rope_tile(q[:, sl], cos, sinp, sinm) * (MLA_SCALE * LOG2E)).astype(BF16)


def _mla_proj(x, sc, sh, w, rope_tab, n_seq, seq_len):
    n = x.shape[0]
    tm = _row_tile(n_seq, seq_len, 512)
    tab_rows = rope_tab[0].shape[0]
    per = tab_rows // tm
    tab_spec = pl.BlockSpec((tm, LANES), lambda i: (i % per, 0))
    row = lambda w_: pl.BlockSpec((tm, w_), lambda i: (i, 0))
    return pl.pallas_call(
        _mla_proj_kernel,
        grid=(n // tm,),
        in_specs=[row(D_MODEL), _seq_spec(seq_len, tm), _seq_spec(seq_len, tm),
                  _full(w['w_in'].shape), _full((1, MLA_Q_LORA)), _full((1, MLA_KV_LORA)), _full(w['w_uq'].shape),
                  tab_spec, tab_spec, tab_spec],
        out_specs=[row(MLA_HEADS * LANES), row(MLA_KV_LORA), row(LANES)],
        out_shape=[jax.ShapeDtypeStruct((n, MLA_HEADS * LANES), BF16),
                   jax.ShapeDtypeStruct((n, MLA_KV_LORA), F32),
                   jax.ShapeDtypeStruct((n, LANES), F32)],
        compiler_params=_params("parallel"),
    )(x, sc, sh, w['w_in'], w['q_norm'], w['kv_norm'], w['w_uq'], *rope_tab)


def _kv_expand_kernel(lat_ref, kr_ref, wuk_ref, wuv_ref, one_ref, k_ref, v_ref):
    lat = lat_ref[...].astype(BF16)
    kn = _dot(lat, wuk_ref[...])
    kr = kr_ref[...]
    for hd in range(MLA_HEADS):
        sl = slice(hd * LANES, (hd + 1) * LANES)
        k_ref[:, sl] = (kn[:, sl] + kr).astype(BF16)
    v_ref[...] = (_dot(lat, wuv_ref[...]) + one_ref[...]).astype(BF16)


def _kv_expand(lat, kr, w):
    m = lat.shape[0]
    tm = _tile(m, 1024)
    wide = MLA_HEADS * LANES
    row = lambda w_: pl.BlockSpec((tm, w_), lambda i: (i, 0))
    return pl.pallas_call(
        _kv_expand_kernel,
        grid=(m // tm,),
        in_specs=[row(MLA_KV_LORA), row(LANES), _full((MLA_KV_LORA, wide)), _full((MLA_KV_LORA, wide)),
                  _full((1, wide))],
        out_specs=[row(wide), row(wide)],
        out_shape=[jax.ShapeDtypeStruct((m, wide), BF16)] * 2,
        compiler_params=_params("parallel"),
    )(lat, kr, w['w_uk'], w['w_uv'], w['v_one'])


def _attn_kernel(q_ref, k_ref, v_ref, o_ref, *, tq, tk, n_kv, q_off, hp):
    i = pl.program_id(2)
    qpos0 = q_off + i * tq
    n_full = jnp.minimum(((qpos0 // CHUNK) + 1) * CHUNK // tk, n_kv)
    n_end = jnp.minimum((((qpos0 + tq - 1) // CHUNK + 1) * CHUNK + tk - 1) // tk, n_kv)
    lane = lax.broadcasted_iota(I32, (tq, LANES), 1)
    heads = [slice(hh * LANES, (hh + 1) * LANES) for hh in range(hp)]
    qs = [q_ref[0, :, sl] for sl in heads]

    def step(j, carry, masked):
        ks = pl.multiple_of(j * tk, 16)
        ss = [_dot_nt(q, k_ref[0, pl.ds(ks, tk), sl]) for q, sl in zip(qs, heads)]
        if masked:
            qc = (qpos0 + lax.broadcasted_iota(I32, (tq, tk), 0)) // CHUNK
            kc = (ks + lax.broadcasted_iota(I32, (tq, tk), 1)) // CHUNK
            visible = kc <= qc
            ss = [jnp.where(visible, s, -1e30) for s in ss]
        ms = [jnp.maximum(m, jnp.max(s, axis=-1, keepdims=True)) for (m, _), s in zip(carry, ss)]
        ps = [jnp.exp2(s - m).astype(BF16) for s, m in zip(ss, ms)]
        pvs = [_dot(p, v_ref[0, pl.ds(ks, tk), sl]) for p, sl in zip(ps, heads)]
        return tuple((m_new, acc * jnp.exp2(m - m_new) + pv) for (m, acc), m_new, pv in zip(carry, ms, pvs))

    carry = ((jnp.full((tq, 1), -1e30, F32), jnp.zeros((tq, LANES), F32)),) * hp
    carry = lax.fori_loop(0, n_full, functools.partial(step, masked=False), carry)
    carry = lax.fori_loop(n_full, n_end, functools.partial(step, masked=True), carry)
    outs = [acc / jnp.sum(jnp.where(lane == MLA_V, acc, 0.0), axis=-1, keepdims=True) for _, acc in carry]
    for pr in range(hp // 2):
        o = jnp.where(lane < MLA_V, outs[2 * pr], pltpu.roll(outs[2 * pr + 1], MLA_V, 1))
        o_ref[0, :, pr * LANES:(pr + 1) * LANES] = o.astype(BF16)


def _attention(q, k, v, tq, tk, hp=4):
    b, t_q, _ = q.shape
    t_k = k.shape[1]
    kern = functools.partial(_attn_kernel, tq=tq, tk=tk, n_kv=t_k // tk, q_off=t_k - t_q, hp=hp)
    wide = hp * LANES
    kv_spec = pl.BlockSpec((1, t_k, wide), lambda b_, g, i: (b_, 0, g), pipeline_mode=pl.Buffered(1))
    return pl.pallas_call(
        kern,
        grid=(b, MLA_HEADS // hp, t_q // tq),
        in_specs=[pl.BlockSpec((1, tq, wide), lambda b_, g, i: (b_, i, g)), kv_spec, kv_spec],
        out_specs=pl.BlockSpec((1, tq, hp * MLA_V), lambda b_, g, i: (b_, i, g)),
        out_shape=jax.ShapeDtypeStruct((b, t_q, MLA_HEADS * MLA_V), BF16),
        compiler_params=_params("parallel", "parallel", "arbitrary"),
    )(q, k, v)


def _layernorm(r, g, b):
    mu = jnp.mean(r, -1, keepdims=True)
    d = r - mu
    var = jnp.mean(d * d, -1, keepdims=True)
    return d * lax.rsqrt(var + LN_EPS) * g + b


def _mixer_out_kernel(a_ref, wo_ref, x_ref, gm_ref, lng_ref, lnb_ref, sc_ref, sh_ref, wr_ref, br_ref, up_ref, lo_ref,
                      xo_ref, h_ref, gate_ref, pos_ref, cnt_ref):
    y = _dot(a_ref[...], wo_ref[...])
    r = DEEPNORM_ALPHA * x_ref[...] + _rows_scale(y, 1.0 + gm_ref[...])
    xn = _layernorm(r, lng_ref[...], lnb_ref[...])
    xo_ref[...] = xn
    h = _modulate(xn, sc_ref[...], sh_ref[...])
    h_ref[...] = h.astype(BF16)

    logits = _dot3(wr_ref[...], h, _dot_nt) + br_ref[...]
    tm = logits.shape[1]
    eio = lax.broadcasted_iota(I32, (N_EXPERTS, tm), 0).astype(F32)
    sels, vals = [], []
    work = logits
    for k in range(TOP_K):
        m = jnp.max(work, axis=0, keepdims=True)
        ik = jnp.min(jnp.where(work == m, eio, float(N_EXPERTS)), axis=0, keepdims=True)
        sel = eio == ik
        work = jnp.where(sel, -jnp.inf, work)
        sels.append(sel)
        vals.append(m)
    es = [jnp.exp(v - vals[0]) for v in vals]
    tot = es[0] + es[1] + es[2] + es[3]
    for k in range(TOP_K):
        gate_ref[pl.ds(k, 1), :] = es[k] / tot
    multi = sels[0] | sels[1] | sels[2] | sels[3]
    mh = jnp.where(multi, 1.0, 0.0)
    before = _dot(mh.astype(BF16), up_ref[...])
    cnt = jnp.sum(mh, axis=1, keepdims=True)
    seg = jnp.floor((cnt + (SEG - 1)) * (1.0 / SEG)) * SEG
    seg_b = jnp.broadcast_to(seg, (N_EXPERTS, LANES)).astype(BF16)
    place = before + _dot(lo_ref[...], seg_b)[:, :1]
    for k in range(TOP_K):
        pos_ref[pl.ds(k, 1), :] = jnp.sum(jnp.where(sels[k], place, 0.0), axis=0, keepdims=True).astype(I32)
    cnt_ref[0] = cnt


def _mixer_out(a, w_o, x, gm, lng, lnb, sc, sh, w_r_t, b_r, n_seq, seq_len):
    n, kdim = a.shape
    tm = _row_tile(n_seq, seq_len, ROUTE_TILE)
    upper = (jnp.arange(tm)[:, None] < jnp.arange(tm)[None, :]).astype(BF16)
    lower_e = (jnp.arange(N_EXPERTS)[None, :] < jnp.arange(N_EXPERTS)[:, None]).astype(BF16)
    row = lambda w_: pl.BlockSpec((tm, w_), lambda i: (i, 0))
    col = pl.BlockSpec((TOP_K, tm), lambda i: (0, i))
    ss = _seq_spec(seq_len, tm)
    return pl.pallas_call(
        _mixer_out_kernel,
        grid=(n // tm,),
        in_specs=[row(kdim), _full(w_o.shape), row(D_MODEL), ss, _full((1, D_MODEL)), _full((1, D_MODEL)), ss, ss,
                  _full((N_EXPERTS, D_MODEL)), _full((N_EXPERTS, 1)), _full((tm, tm)),
                  _full((N_EXPERTS, N_EXPERTS))],
        out_specs=[row(D_MODEL), row(D_MODEL), col, col, pl.BlockSpec((1, N_EXPERTS, 1), lambda i: (i, 0, 0))],
        out_shape=[jax.ShapeDtypeStruct((n, D_MODEL), F32), jax.ShapeDtypeStruct((n, D_MODEL), BF16),
                   jax.ShapeDtypeStruct((TOP_K, n), F32), jax.ShapeDtypeStruct((TOP_K, n), I32),
                   jax.ShapeDtypeStruct((n // tm, N_EXPERTS, 1), F32)],
        compiler_params=_params("parallel"),
    )(a, w_o, x, gm, lng, lnb, sc, sh, w_r_t, b_r, upper, lower_e)


def _piece_sizes(tm):
    sizes, size = [], SEG
    while size <= tm:
        sizes.append(size)
        size *= 2
    assert sizes[-1] == tm
    return tuple(reversed(sizes))


def _segment_copies(seg_ref, off_ref, start_ref, tile, sizes, make):
    def body(e, carry):
        n = seg_ref[tile * N_EXPERTS + e]
        off = off_ref[tile * N_EXPERTS + e]
        start = start_ref[tile * N_EXPERTS + e]
        for size in sizes:
            done = n & (-2 * size)

            @pl.when((n & size) != 0)
            def _():
                make(pl.multiple_of(off + done, SEG), pl.multiple_of(start + done, SEG), size)
        return carry

    lax.fori_loop(0, N_EXPERTS, body, 0)


def _dispatch_kernel(seg_ref, off_ref, start_ref, h_ref, pos_ref, zero_ref, xr_ref, buf_ref, sem, *, nt, sizes):
    del zero_ref
    i = pl.program_id(0)
    slot = i % 2

    def copies(tile, sl, wait):
        def make(buf_row, hbm_row, size):
            cp = pltpu.make_async_copy(buf_ref.at[sl, pl.ds(buf_row, size)], xr_ref.at[pl.ds(hbm_row, size)],
                                       sem.at[sl])
            cp.wait() if wait else cp.start()
        _segment_copies(seg_ref, off_ref, start_ref, tile, sizes, make)

    @pl.when(i >= 2)
    def _():
        copies(i - 2, slot, True)

    pos = pos_ref[...]
    rows = buf_ref.shape[1]
    rio = lax.broadcasted_iota(I32, (rows, pos.shape[1]), 0)
    sel = (rio == pos[0:1]) | (rio == pos[1:2]) | (rio == pos[2:3]) | (rio == pos[3:4])
    buf_ref[slot] = _dot(jnp.where(sel, 1.0, 0.0).astype(BF16), h_ref[...]).astype(BF16)
    copies(i, slot, False)

    @pl.when(i == nt - 1)
    def _():
        copies(i, slot, True)

        @pl.when(i >= 1)
        def _():
            copies(i - 1, 1 - slot, True)


def _dispatch(h, pos_t, seg, off, start, p_rows, tm):
    n = h.shape[0]
    nt = n // tm
    sizes = _piece_sizes(tm)
    rows = TOP_K * tm + N_EXPERTS * SEG
    grid_spec = pltpu.PrefetchScalarGridSpec(
        num_scalar_prefetch=3,
        grid=(nt,),
        in_specs=[pl.BlockSpec((tm, D_MODEL), lambda i, *_: (i, 0)),
                  pl.BlockSpec((TOP_K, tm), lambda i, *_: (0, i)),
                  pl.BlockSpec(memory_space=pl.ANY)],
        out_specs=pl.BlockSpec(memory_space=pl.ANY),
        scratch_shapes=[pltpu.VMEM((2, rows, D_MODEL), BF16), pltpu.SemaphoreType.DMA((2,))],
    )
    return pl.pallas_call(
        functools.partial(_dispatch_kernel, nt=nt, sizes=sizes),
        grid_spec=grid_spec,
        out_shape=jax.ShapeDtypeStruct((p_rows, D_MODEL), BF16),
        input_output_aliases={5: 0},
        compiler_params=_params("arbitrary"),
    )(seg, off, start, h, pos_t, jnp.zeros((p_rows, D_MODEL), BF16))


def _expert_kernel(be_ref, na_ref, x_ref, wgu_ref, bgu_ref, wd_ref, bd_ref, y_ref, wgu_s, wd_s):
    i = pl.program_id(0)

    @pl.when((i == 0) | (be_ref[i] != be_ref[jnp.maximum(i - 1, 0)]))
    def _():
        wgu_s[...] = wgu_ref[0].astype(BF16)
        wd_s[...] = wd_ref[0].astype(BF16)

    @pl.when(i < na_ref[0])
    def _():
        gu = _dot(x_ref[...], wgu_s[...]) + bgu_ref[0]
        gate = jnp.minimum(gu[:, :D_FF], SWIGLU_LIMIT)
        up = jnp.clip(gu[:, D_FF:], -SWIGLU_LIMIT, SWIGLU_LIMIT)
        act = (up + 1.0) * gate * _sigmoid(SWIGLU_ALPHA * gate)
        y_ref[...] = (_dot(act.astype(BF16), wd_s[...]) + bd_ref[0]).astype(BF16)

    @pl.when(i >= na_ref[0])
    def _():
        y_ref[...] = jnp.zeros_like(y_ref)


def _experts(x_rows, blk_e, n_act, w_gu, b_gu, w_down, b_down, layer):
    p = x_rows.shape[0]
    grid_spec = pltpu.PrefetchScalarGridSpec(
        num_scalar_prefetch=2,
        grid=(p // MOE_ROWS,),
        in_specs=[pl.BlockSpec((MOE_ROWS, D_MODEL), lambda i, be, na: (i, 0)),
                  pl.BlockSpec((None, 1, D_MODEL, 2 * D_FF), lambda i, be, na: (layer, be[i], 0, 0)),
                  pl.BlockSpec((1, 1, 2 * D_FF), lambda i, be, na: (be[i], 0, 0)),
                  pl.BlockSpec((None, 1, D_FF, D_MODEL), lambda i, be, na: (layer, be[i], 0, 0)),
                  pl.BlockSpec((1, 1, D_MODEL), lambda i, be, na: (be[i], 0, 0))],
        out_specs=pl.BlockSpec((MOE_ROWS, D_MODEL), lambda i, be, na: (i, 0)),
        scratch_shapes=[pltpu.VMEM((D_MODEL, 2 * D_FF), BF16), pltpu.VMEM((D_FF, D_MODEL), BF16)],
    )
    return pl.pallas_call(
        _expert_kernel,
        grid_spec=grid_spec,
        out_shape=jax.ShapeDtypeStruct((p, D_MODEL), BF16),
        compiler_params=_params("arbitrary"),
    )(blk_e, n_act, x_rows, w_gu, b_gu, w_down, b_down)


def _combine_kernel(seg_ref, off_ref, start_ref, y_ref, pos_ref, g_ref, x_ref, gf_ref, lng_ref, lnb_ref, o_ref,
                    buf_ref, sem, *, nt, sizes):
    i = pl.program_id(0)
    slot = i % 2

    def copies(tile, sl, wait):
        def make(buf_row, hbm_row, size):
            cp = pltpu.make_async_copy(y_ref.at[pl.ds(hbm_row, size)], buf_ref.at[sl, pl.ds(buf_row, size)],
                                       sem.at[sl])
            cp.wait() if wait else cp.start()
        _segment_copies(seg_ref, off_ref, start_ref, tile, sizes, make)

    @pl.when(i == 0)
    def _():
        buf_ref[...] = jnp.zeros_like(buf_ref)
        copies(0, 0, False)

    @pl.when(i + 1 < nt)
    def _():
        copies(i + 1, 1 - slot, False)

    copies(i, slot, True)
    pos = pos_ref[...]
    g = g_ref[...]
    cio = lax.broadcasted_iota(I32, (pos.shape[0], buf_ref.shape[1]), 1)
    gm = jnp.where(cio == pos[:, 0:1], g[:, 0:1], 0.0)
    for k in range(1, TOP_K):
        gm = gm + jnp.where(cio == pos[:, k:k + 1], g[:, k:k + 1], 0.0)
    y = _dot(gm.astype(BF16), buf_ref[slot])
    r = DEEPNORM_ALPHA * x_ref[...] + _rows_scale(y, 1.0 + gf_ref[...])
    o_ref[...] = _layernorm(r, lng_ref[...], lnb_ref[...])


def _combine(y_rows, pos, gates, seg, off, start, x, gf, lng, lnb, seq_len, tm):
    n = x.shape[0]
    nt = n // tm
    sizes = _piece_sizes(tm)
    rows = TOP_K * tm + N_EXPERTS * SEG
    row = pl.BlockSpec((tm, D_MODEL), lambda i, *_: (i, 0))
    four = pl.BlockSpec((tm, TOP_K), lambda i, *_: (i, 0))
    grid_spec = pltpu.PrefetchScalarGridSpec(
        num_scalar_prefetch=3,
        grid=(nt,),
        in_specs=[pl.BlockSpec(memory_space=pl.ANY), four, four, row, _seq_spec(seq_len, tm),
                  _full((1, D_MODEL)), _full((1, D_MODEL))],
        out_specs=row,
        scratch_shapes=[pltpu.VMEM((2, rows, D_MODEL), BF16), pltpu.SemaphoreType.DMA((2,))],
    )
    return pl.pallas_call(
        functools.partial(_combine_kernel, nt=nt, sizes=sizes),
        grid_spec=grid_spec,
        out_shape=jax.ShapeDtypeStruct((n, D_MODEL), F32),
        compiler_params=_params("arbitrary"),
    )(seg, off, start, y_rows, pos, gates, x, gf, lng, lnb)


def _moe_and_norm(x, h, gate_t, pos_t, cnt, gf, lng, lnb, w, n_seq, seq_len):
    n = x.shape[0]
    tm = _row_tile(n_seq, seq_len, ROUTE_TILE)
    nt = n // tm
    counts = cnt[:, :, 0].astype(I32)
    seg = (counts + SEG - 1) // SEG * SEG
    group = jnp.sum(seg, axis=0)
    padded = (group + MOE_ROWS - 1) // MOE_ROWS * MOE_ROWS
    pad_end = jnp.cumsum(padded)
    start = (pad_end - padded)[None, :] + jnp.cumsum(seg, axis=0) - seg
    off = jnp.cumsum(seg, axis=1) - seg
    worst = n * TOP_K + nt * N_EXPERTS * (SEG - 1) + N_EXPERTS * (MOE_ROWS - 1)
    nb = (worst + MOE_ROWS - 1) // MOE_ROWS
    seg, off, start = seg.reshape(-1), off.reshape(-1), start.reshape(-1)
    x_rows = _dispatch(h, pos_t, seg, off, start, nb * MOE_ROWS, tm)
    blk_row = jnp.arange(nb, dtype=I32) * MOE_ROWS
    blk_e = jnp.minimum(jnp.sum(pad_end[None, :] <= blk_row[:, None], axis=1), N_EXPERTS - 1).astype(I32)
    n_act = (pad_end[-1:] // MOE_ROWS).astype(I32)
    y_rows = _experts(x_rows, blk_e, n_act, w['w_gu'], w['b_gu'], w['w_down'], w['b_down'], w['layer'])
    return _combine(y_rows, pos_t.T, gate_t.T, seg, off, start, x, gf, lng, lnb, seq_len, tm)


def _mod_matmul_kernel(x_ref, sc_ref, sh_ref, w_ref, o_ref):
    h = _modulate(x_ref[...], sc_ref[...], sh_ref[...]).astype(BF16)
    o_ref[...] = _dot(h, w_ref[...]).astype(o_ref.dtype)


def _mod_matmul(x, sc, sh, w, out_dtype, n_seq, seq_len):
    n = x.shape[0]
    n_out = w.shape[1]
    tm = _row_tile(n_seq, seq_len, 1024)
    tn = _tile(n_out, 1024, LANES)
    return pl.pallas_call(
        _mod_matmul_kernel,
        grid=(n // tm, n_out // tn),
        in_specs=[pl.BlockSpec((tm, D_MODEL), lambda i, j: (i, 0)), _seq_spec(seq_len, tm), _seq_spec(seq_len, tm),
                  pl.BlockSpec((D_MODEL, tn), lambda i, j: (0, j))],
        out_specs=pl.BlockSpec((tm, tn), lambda i, j: (i, j)),
        out_shape=jax.ShapeDtypeStruct((n, n_out), out_dtype),
        compiler_params=_params("parallel", "arbitrary"),
    )(x, sc, sh, w)


def _gdn_gates_kernel(x_ref, sc_ref, sh_ref, wab_ref, wabt_ref, alog_ref, dtb_ref, alogt_ref, dtbt_ref,
                      tri_ref, trit_ref, gc_ref, beta_ref, gct_ref):
    h = _modulate(x_ref[...], sc_ref[...], sh_ref[...]).astype(BF16)
    hv = GDN_V_HEADS
    ab = _dot(h, wab_ref[...])
    beta_ref[...] = _sigmoid(ab[:, :hv])
    g = -jnp.exp(alog_ref[...]) * _softplus(ab[:, hv:] + dtb_ref[...])
    gc_ref[...] = _dot_exact_rhs_lhs(tri_ref[...], g)
    abt = _dot_nt(wabt_ref[...], h)
    gt = -jnp.exp(alogt_ref[...]) * _softplus(abt + dtbt_ref[...])
    gct_ref[...] = _dot_exact_rhs(gt, trit_ref[...])


def _dot_exact_rhs_lhs(a01, b):
    b1 = b.astype(BF16)
    r1 = b - b1.astype(F32)
    b2 = r1.astype(BF16)
    b3 = (r1 - b2.astype(F32)).astype(BF16)
    return _dot(a01, b1) + (_dot(a01, b2) + _dot(a01, b3))


def _gdn_gates(x, sc, sh, w, n_seq, seq_len, chunk):
    n = x.shape[0]
    tm = _row_tile(n_seq, seq_len, 512)
    hv = GDN_V_HEADS
    r = jnp.arange(tm)
    tri = ((r[:, None] // chunk == r[None, :] // chunk) & (r[None, :] <= r[:, None])).astype(BF16)
    row = lambda w_: pl.BlockSpec((tm, w_), lambda i: (i, 0))
    return pl.pallas_call(
        _gdn_gates_kernel,
        grid=(n // tm,),
        in_specs=[row(D_MODEL), _seq_spec(seq_len, tm), _seq_spec(seq_len, tm),
                  _full((D_MODEL, 2 * hv)), _full((hv, D_MODEL)), _full((1, hv)), _full((1, hv)),
                  _full((hv, 1)), _full((hv, 1)), _full((tm, tm)), _full((tm, tm))],
        out_specs=[row(hv), row(hv), pl.BlockSpec((hv, tm), lambda i: (0, i))],
        out_shape=[jax.ShapeDtypeStruct((n, hv), F32), jax.ShapeDtypeStruct((n, hv), F32),
                   jax.ShapeDtypeStruct((hv, n), F32)],
        compiler_params=_params("parallel"),
    )(x, sc, sh, w['w_ab'], w['w_a_t'], w['a_log'], w['dt_bias'], w['a_log'].T, w['dt_bias'].T, tri, tri.T)


def _conv_kernel(x_ref, prev_ref, st_ref, w_ref, o_ref, *, per, norm, n_q_blocks):
    i = pl.program_id(0)
    j = pl.program_id(1)
    x = x_ref[...]
    tm = x.shape[0]
    halo = jnp.where(i % per == 0, st_ref[0], prev_ref[...])
    w = w_ref[...]
    rows = lax.broadcasted_iota(I32, (HALO, x.shape[1]), 0)
    acc = x * w[GDN_CONV_W - 1:GDN_CONV_W]
    for s in range(1, GDN_CONV_W):
        xs = pltpu.roll(x, s, 0)
        head = jnp.where(rows < s, pltpu.roll(halo, s, 0), xs[:HALO])
        xs = jnp.concatenate([head, xs[HALO:]], axis=0) if tm > HALO else head
        acc = acc + xs * w[GDN_CONV_W - 1 - s:GDN_CONV_W - s]
    y = _silu(acc)
    if norm:
        scale = jnp.where(j < n_q_blocks, GDN_DK ** -0.5, 1.0)
        for hd in range(y.shape[1] // GDN_DK):
            sl = slice(hd * GDN_DK, (hd + 1) * GDN_DK)
            yh = y[:, sl]
            o_ref[:, sl] = yh * (lax.rsqrt(jnp.sum(yh * yh, -1, keepdims=True) + L2_EPS) * scale)
    else:
        o_ref[...] = y


def _gdn_conv(qkv, state8, conv_w, n_seq, seq_len, col0, width, norm):
    n = qkv.shape[0]
    tm = _tile(seq_len, 512)
    per = seq_len // tm
    tn = 512
    c0 = col0 // tn
    kern = functools.partial(_conv_kernel, per=per, norm=norm, n_q_blocks=GDN_QK_DIM // tn)
    hb = tm // HALO
    return pl.pallas_call(
        kern,
        grid=(n // tm, width // tn),
        in_specs=[pl.BlockSpec((tm, tn), lambda i, j: (i, c0 + j)),
                  pl.BlockSpec((HALO, tn), lambda i, j: (jnp.maximum(i * hb - 1, 0), c0 + j)),
                  pl.BlockSpec((1, HALO, tn), lambda i, j: (i // per, 0, c0 + j)),
                  pl.BlockSpec((GDN_CONV_W, tn), lambda i, j: (0, c0 + j))],
        out_specs=pl.BlockSpec((tm, tn), lambda i, j: (i, j)),
        out_shape=jax.ShapeDtypeStruct((n, width), F32),
        compiler_params=_params("parallel", "parallel"),
    )(qkv, qkv, state8, conv_w)


def _inv_masks(c):
    row = lax.broadcasted_iota(I32, (c, c), 0)
    col = lax.broadcasted_iota(I32, (c, c), 1)
    eye = jnp.where(row == col, 1.0, 0.0)
    diag = row // INV_BASE == col // INV_BASE
    offs = []
    size = INV_BASE
    while size < c:
        offs.append((row // (2 * size) == col // (2 * size)) & (row // size % 2 == 1) & (col // size % 2 == 0))
        size *= 2
    return eye, diag, offs


def _inv_unit_lower(lmats, masks):
    eye, diag, offs = masks
    pws = [jnp.where(diag, lm, 0.0) for lm in lmats]
    ts = [eye - pw for pw in pws]
    size = 2
    while size < INV_BASE:
        pws = [_dotb(pw, pw) for pw in pws]
        ts = [t + _dotb(t, pw) for t, pw in zip(ts, pws)]
        size *= 2
    for off in offs:
        tbs = [_dotb(t, jnp.where(off, lm, 0.0)) for t, lm in zip(ts, lmats)]
        ts = [t - _dotb(tb, t) for t, tb in zip(ts, tbs)]
    return ts


def _gdn_core_kernel(q_ref, k_ref, v_ref, z_ref, gc_ref, beta_ref, gct_ref, s0_ref, nw_ref, o_ref, s_ref, *, c, hb):
    @pl.when(pl.program_id(2) == 0)
    def _():
        s_ref[...] = s0_ref[...]

    row = lax.broadcasted_iota(I32, (c, c), 0)
    col = lax.broadcasted_iota(I32, (c, c), 1)
    lower = col <= row
    strict = col < row
    masks = _inv_masks(c)
    heads = range(hb)
    ksl = [slice(kh * GDN_DK, (kh + 1) * GDN_DK) for kh in range(hb // 2)]
    vsl = [slice(h * GDN_DV, (h + 1) * GDN_DV) for h in heads]
    qn = [q_ref[0, :, sl] for sl in ksl]
    kn = [k_ref[0, :, sl] for sl in ksl]
    kb = [k.astype(BF16) for k in kn]
    kk = [_dot_nt(k, k) for k in kb]
    qk = [_dot_nt(q.astype(BF16), k) for q, k in zip(qn, kb)]
    gcc = [gc_ref[0, 0, :, h:h + 1] for h in heads]
    bc = [beta_ref[0, 0, :, h:h + 1] for h in heads]
    decay = [jnp.where(lower, jnp.exp(jnp.minimum(gcc[h] - gct_ref[0, 0, h:h + 1, :], 0.0)), 0.0) for h in heads]
    lmat = [jnp.where(strict, kk[h // 2] * decay[h], 0.0) * bc[h] for h in heads]
    ts = _inv_unit_lower(lmat, masks)
    egc = [jnp.exp(g) for g in gcc]
    rhs = [jnp.concatenate([v_ref[0, :, vsl[h]] * bc[h], kn[h // 2] * (bc[h] * egc[h])], axis=1).astype(BF16)
           for h in heads]
    uw = [_dot(ts[h].astype(BF16), rhs[h]) for h in heads]
    s_old = [s_ref[0, h] for h in heads]
    sb = [s.astype(BF16) for s in s_old]
    ws = [_dot(uw[h][:, GDN_DV:].astype(BF16), sb[h]) for h in heads]
    qs = [_dot((qn[h // 2] * egc[h]).astype(BF16), sb[h]) for h in heads]
    vb = [(uw[h][:, :GDN_DV] - ws[h]).astype(BF16) for h in heads]
    o = [qs[h] + _dot((qk[h // 2] * decay[h]).astype(BF16), vb[h]) for h in heads]
    glast = [g[c - 1:c, :] for g in gcc]
    kd = [(kn[h // 2] * jnp.exp(glast[h] - gcc[h])).astype(BF16) for h in heads]
    s_new = [s_old[h] * jnp.exp(glast[h]) + _dot_tn(kd[h], vb[h]) for h in heads]
    for h in heads:
        s_ref[0, h] = s_new[h]
        z = z_ref[0, :, vsl[h]].astype(F32)
        on = o[h] * lax.rsqrt(jnp.mean(o[h] * o[h], -1, keepdims=True) + RMS_EPS) * nw_ref[...]
        o_ref[0, :, vsl[h]] = (on * _silu(z)).astype(BF16)


def _gdn_core(qk, v, z, gc, beta, gct, s0, norm_w, c, hb=GDN_V_HEADS):
    b, t, _ = v.shape
    ng = GDN_V_HEADS // hb
    kw = hb // 2 * GDN_DK
    vw = hb * GDN_DV
    gc_g = gc.reshape(b, t, ng, hb).transpose(0, 2, 1, 3)
    beta_g = beta.reshape(b, t, ng, hb).transpose(0, 2, 1, 3)
    gct_g = gct.reshape(b, ng, hb, t)
    kern = functools.partial(_gdn_core_kernel, c=c, hb=hb)
    nkb = GDN_QK_DIM // kw
    return pl.pallas_call(
        kern,
        grid=(b, ng, t // c),
        in_specs=[pl.BlockSpec((1, c, kw), lambda b_, g, i: (b_, i, g)),
                  pl.BlockSpec((1, c, kw), lambda b_, g, i: (b_, i, nkb + g)),
                  pl.BlockSpec((1, c, vw), lambda b_, g, i: (b_, i, g)),
                  pl.BlockSpec((1, c, vw), lambda b_, g, i: (b_, i, g)),
                  pl.BlockSpec((1, 1, c, hb), lambda b_, g, i: (b_, g, i, 0)),
                  pl.BlockSpec((1, 1, c, hb), lambda b_, g, i: (b_, g, i, 0)),
                  pl.BlockSpec((1, 1, hb, c), lambda b_, g, i: (b_, g, 0, i)),
                  pl.BlockSpec((1, hb, GDN_DK, GDN_DV), lambda b_, g, i: (b_, g, 0, 0)),
                  _full((1, GDN_DV))],
        out_specs=[pl.BlockSpec((1, c, vw), lambda b_, g, i: (b_, i, g)),
                   pl.BlockSpec((1, hb, GDN_DK, GDN_DV), lambda b_, g, i: (b_, g, 0, 0))],
        out_shape=[jax.ShapeDtypeStruct((b, t, GDN_V_DIM), BF16),
                   jax.ShapeDtypeStruct((b, GDN_V_HEADS, GDN_DK, GDN_DV), F32)],
        compiler_params=_params("parallel", "parallel", "arbitrary"),
    )(qk, qk, v, z, gc_g, beta_g, gct_g, s0, norm_w)


def _prep_weights(p):
    f = {}
    pad_heads = lambda w_, dh: jnp.pad(w_.reshape(w_.shape[0], w_.shape[1], MLA_HEADS, dh),
                                       ((0, 0), (0, 0), (0, 0), (0, LANES - dh))).reshape(
                                           w_.shape[0], w_.shape[1], MLA_HEADS * LANES)
    w_in = p['mla_w_in']
    nl = w_in.shape[0]
    lo = MLA_Q_LORA + MLA_KV_LORA
    z = lambda k: jnp.zeros((nl, D_MODEL, k), F32)
    f['mla_w_in'] = jnp.concatenate([w_in[..., :lo], z(MLA_NOPE), w_in[..., lo:], z(LANES - MLA_QK)], -1).astype(BF16)
    f['mla_w_uq'] = pad_heads(p['mla_w_uq'], MLA_QK).astype(BF16)
    f['mla_w_uk'] = pad_heads(p['mla_w_uk'], MLA_NOPE).astype(BF16)
    f['mla_w_uv'] = pad_heads(p['mla_w_uv'], MLA_V).astype(BF16)
    f['mla_w_o'] = p['mla_w_o'].astype(BF16)
    f['v_one'] = jnp.tile((jnp.arange(LANES) == MLA_V).astype(F32), MLA_HEADS)[None]
    g_in = p['gdn_w_in']
    f['gdn_w_qkv'] = g_in[..., :GDN_CONV_DIM].astype(BF16)
    f['gdn_w_z'] = g_in[..., GDN_CONV_DIM:GDN_CONV_DIM + GDN_V_DIM].astype(BF16)
    f['gdn_w_ab'] = g_in[..., GDN_CONV_DIM + GDN_V_DIM:].astype(BF16)
    f['gdn_w_a_t'] = jnp.swapaxes(g_in[..., GDN_CONV_DIM + GDN_V_DIM + GDN_V_HEADS:], 1, 2).astype(BF16)
    f['gdn_w_o'] = p['gdn_w_o'].astype(BF16)
    f['moe_w_r_t'] = jnp.swapaxes(p['moe_w_router'], 1, 2)
    return f


def _rope_tables(pos):
    half = MLA_ROPE // 2
    inv_freq = ROPE_BASE ** (-jnp.arange(half, dtype=F32) / half)
    ang = pos.astype(F32)[:, None] * inv_freq[None, :]
    cos, sin = jnp.cos(ang), jnp.sin(ang)
    t = pos.shape[0]
    z = lambda k: jnp.zeros((t, k), F32)
    cos_t = jnp.concatenate([jnp.ones((t, MLA_NOPE), F32), cos, cos, z(LANES - MLA_QK)], -1)
    sinp_t = jnp.concatenate([z(MLA_NOPE + half), sin, z(LANES - MLA_QK)], -1)
    sinm_t = jnp.concatenate([z(MLA_NOPE), -sin, z(half + LANES - MLA_QK)], -1)
    return cos_t, sinp_t, sinm_t


def _trunk(x, c, pos, past, p, f):
    bsz, t, _ = x.shape
    n = bsz * t
    x = x.reshape(n, D_MODEL)
    mod = _cond(c, p['w_cond'], p['b_cond']).reshape(DEPTH, bsz, 6, 1, D_MODEL)
    tm_rope = _row_tile(bsz, t, 512)
    tabs = _rope_tables(pos)
    if tm_rope > t:
        tabs = tuple(jnp.tile(tb, (tm_rope // t, 1)) for tb in tabs)
    lats, krs, convs, ssms = [], [], [], []
    for layer in range(DEPTH):
        sh_m, sc_m, g_m, sh_f, sc_f, g_f = [mod[layer, :, i] for i in range(6)]
        j = layer // 2
        if layer % 2 == 0:
            w = dict(w_in=f['mla_w_in'][j], q_norm=p['mla_q_norm'][j][None], kv_norm=p['mla_kv_norm'][j][None],
                     w_uq=f['mla_w_uq'][j], w_uk=f['mla_w_uk'][j], w_uv=f['mla_w_uv'][j], v_one=f['v_one'])
            q, lat, krt = _mla_proj(x, sc_m, sh_m, w, tabs, bsz, t)
            lats.append(lat.reshape(bsz, t, MLA_KV_LORA))
            krs.append(krt[:, MLA_NOPE:MLA_QK].reshape(bsz, t, MLA_ROPE))
            if past is None:
                lat_all, kr_all, t_k = lat, krt, t
            else:
                p_lat, p_kr = past[0][j], past[1][j]
                t_k = p_lat.shape[1] + t
                p_krt = jnp.pad(p_kr, ((0, 0), (0, 0), (MLA_NOPE, LANES - MLA_QK)))
                lat_all = jnp.concatenate([p_lat, lat.reshape(bsz, t, -1)], 1).reshape(bsz * t_k, MLA_KV_LORA)
                kr_all = jnp.concatenate([p_krt, krt.reshape(bsz, t, -1)], 1).reshape(bsz * t_k, LANES)
            k, v = _kv_expand(lat_all, kr_all, w)
            wide = MLA_HEADS * LANES
            tq = _tile(t, 512)
            tk = tq if past is None else t_k
            ctx = _attention(q.reshape(bsz, t, wide), k.reshape(bsz, t_k, wide), v.reshape(bsz, t_k, wide), tq, tk)
            a, w_o = ctx.reshape(n, MLA_HEADS * MLA_V), f['mla_w_o'][j]
        else:
            chunk = _tile(t, 128)
            w = dict(w_ab=f['gdn_w_ab'][j], w_a_t=f['gdn_w_a_t'][j], a_log=p['gdn_a_log'][j][None],
                     dt_bias=p['gdn_dt_bias'][j][None])
            qkv = _mod_matmul(x, sc_m, sh_m, f['gdn_w_qkv'][j], F32, bsz, t)
            z = _mod_matmul(x, sc_m, sh_m, f['gdn_w_z'][j], BF16, bsz, t)
            gc, beta, gct = _gdn_gates(x, sc_m, sh_m, w, bsz, t, chunk)
            if past is None:
                conv_state = jnp.zeros((bsz, GDN_CONV_W - 1, GDN_CONV_DIM), F32)
                s0 = jnp.zeros((bsz, GDN_V_HEADS, GDN_DK, GDN_DV), F32)
            else:
                conv_state, s0 = past[2][j], past[3][j]
            state8 = jnp.pad(conv_state, ((0, 0), (HALO - (GDN_CONV_W - 1), 0), (0, 0)))
            qkv3 = qkv.reshape(bsz, t, GDN_CONV_DIM)
            convs.append(qkv3[:, -(GDN_CONV_W - 1):])
            cw = p['gdn_conv_w'][j]
            qk_n = _gdn_conv(qkv, state8, cw, bsz, t, 0, 2 * GDN_QK_DIM, True)
            v_c = _gdn_conv(qkv, state8, cw, bsz, t, 2 * GDN_QK_DIM, GDN_V_DIM, False)
            o, s_new = _gdn_core(qk_n.reshape(bsz, t, -1), v_c.reshape(bsz, t, -1), z.reshape(bsz, t, -1),
                                 gc.reshape(bsz, t, -1), beta.reshape(bsz, t, -1),
                                 gct.reshape(GDN_V_HEADS, bsz, t).transpose(1, 0, 2), s0,
                                 p['gdn_norm'][j][None], chunk)
            ssms.append(s_new)
            a, w_o = o.reshape(n, GDN_V_DIM), f['gdn_w_o'][j]
        x, h, gate_t, pos_t, cnt = _mixer_out(
            a, w_o, x, g_m, p['ln1_g'][layer][None], p['ln1_b'][layer][None], sc_f, sh_f,
            f['moe_w_r_t'][layer], p['moe_b_router'][layer][:, None], bsz, t)
        wm = dict(w_gu=p['moe_w_gu'], b_gu=p['moe_b_gu'][layer][:, None], w_down=p['moe_w_down'],
                  b_down=p['moe_b_down'][layer][:, None], layer=layer)
        x = _moe_and_norm(x, h, gate_t, pos_t, cnt, g_f, p['ln2_g'][layer][None], p['ln2_b'][layer][None],
                          wm, bsz, t)
    return x.reshape(bsz, t, D_MODEL), jnp.stack(lats), jnp.stack(krs), jnp.stack(convs), jnp.stack(ssms)


def kernel(x_prompt, x_sample, c_prompt, c_sample, cache_mla_latent, cache_mla_krope, state_gdn_conv, state_gdn_ssm, w_cond, b_cond, ln1_g, ln1_b, ln2_g, ln2_b, mla_w_in, mla_q_norm, mla_kv_norm, mla_w_uq, mla_w_uk, mla_w_uv, mla_w_o, gdn_w_in, gdn_conv_w, gdn_a_log, gdn_dt_bias, gdn_norm, gdn_w_o, moe_w_router, moe_b_router, moe_w_gu, moe_b_gu, moe_w_down, moe_b_down):
    p = dict(w_cond=w_cond, b_cond=b_cond, ln1_g=ln1_g, ln1_b=ln1_b, ln2_g=ln2_g, ln2_b=ln2_b,
             mla_w_in=mla_w_in, mla_q_norm=mla_q_norm, mla_kv_norm=mla_kv_norm, mla_w_uq=mla_w_uq,
             mla_w_uk=mla_w_uk, mla_w_uv=mla_w_uv, mla_w_o=mla_w_o, gdn_w_in=gdn_w_in,
             gdn_conv_w=gdn_conv_w, gdn_a_log=gdn_a_log, gdn_dt_bias=gdn_dt_bias, gdn_norm=gdn_norm,
             gdn_w_o=gdn_w_o, moe_w_router=moe_w_router, moe_b_router=moe_b_router, moe_w_gu=moe_w_gu,
             moe_b_gu=moe_b_gu, moe_w_down=moe_w_down, moe_b_down=moe_b_down)
    f = _prep_weights(p)
    past_len = cache_mla_latent.shape[2]
    assert past_len % CHUNK == 0 and x_sample.shape[1] <= CHUNK
    pos_p = jnp.arange(x_prompt.shape[1], dtype=I32)
    pos_s = past_len + jnp.arange(x_sample.shape[1], dtype=I32)
    y_p, p_lat, p_kr, p_conv, p_ssm = _trunk(x_prompt, c_prompt, pos_p, None, p, f)
    past = (cache_mla_latent, cache_mla_krope, state_gdn_conv, state_gdn_ssm)
    y_s, s_lat, s_kr, s_conv, s_ssm = _trunk(x_sample, c_sample, pos_s, past, p, f)
    return (y_p, y_s, p_lat, p_kr, p_conv, p_ssm, s_lat, s_kr, s_conv, s_ssm)
```

```python
import functools
import math

import jax
import jax.numpy as jnp
from jax import lax
from jax.experimental import pallas as pl
from jax.experimental.pallas import tpu as pltpu

F32 = jnp.float32
BF16 = jnp.bfloat16
I32 = jnp.int32

D_MODEL = 1024
DEPTH = 4
CHUNK = 64
MLA_HEADS = 16
MLA_Q_LORA = 768
MLA_KV_LORA = 256
MLA_NOPE = 64
MLA_ROPE = 32
MLA_V = 64
MLA_QK = MLA_NOPE + MLA_ROPE
MLA_SCALE = MLA_QK ** -0.5
ROPE_BASE = 10000.0
GDN_QK_HEADS = 8
GDN_V_HEADS = 16
GDN_DK = 128
GDN_DV = 128
GDN_QK_DIM = GDN_QK_HEADS * GDN_DK
GDN_V_DIM = GDN_V_HEADS * GDN_DV
GDN_CONV_DIM = 2 * GDN_QK_DIM + GDN_V_DIM
GDN_CONV_W = 4
N_EXPERTS = 32
TOP_K = 4
D_FF = D_MODEL
SWIGLU_ALPHA = 1.702
SWIGLU_LIMIT = 7.0
DEEPNORM_ALPHA = (2 * DEPTH) ** 0.25
LN_EPS = 1e-5
RMS_EPS = 1e-6
L2_EPS = 1e-6

LANES = 128
HALO = 8
LOG2E = 1.4426950408889634
MOE_ROWS = 512
ROUTE_TILE = 512
SEG = 16
INV_BASE = 16
VMEM_LIMIT = 48 * 1024 * 1024


def _params(*sem):
    return pltpu.CompilerParams(dimension_semantics=sem, vmem_limit_bytes=VMEM_LIMIT)


def _tile(n, pref, mult=8):
    t = min(pref, n)
    while t >= mult:
        if n % t == 0 and t % mult == 0:
            return t
        t -= 1
    return n


def _dot(a, b):
    return jnp.dot(a, b, preferred_element_type=F32)


def _dot_nt(a, b):
    return lax.dot_general(a, b, (((1,), (1,)), ((), ())), preferred_element_type=F32)


def _dot_tn(a, b):
    return lax.dot_general(a, b, (((0,), (0,)), ((), ())), preferred_element_type=F32)


def _dotb(a, b):
    return _dot(a.astype(BF16), b.astype(BF16))


def _split(a):
    hi = a.astype(BF16)
    lo = (a - hi.astype(F32)).astype(BF16)
    return hi, lo


def _dot3(a, b, dot=_dot):
    ah, al = _split(a)
    bh, bl = _split(b)
    return dot(ah, bh) + (dot(ah, bl) + dot(al, bh))


def _dot_exact_rhs(a, b01, dot=_dot):
    a1 = a.astype(BF16)
    r1 = a - a1.astype(F32)
    a2 = r1.astype(BF16)
    a3 = (r1 - a2.astype(F32)).astype(BF16)
    return dot(a1, b01) + (dot(a2, b01) + dot(a3, b01))


def _sigmoid(x):
    return 1.0 / (1.0 + jnp.exp(-x))


def _silu(x):
    return x * _sigmoid(x)


def _softplus(x):
    return jnp.maximum(x, 0.0) + jnp.log(1.0 + jnp.exp(-jnp.abs(x)))


def _rows_scale(x, s):
    spt = s.shape[0]
    if spt == 1:
        return x * s[0]
    tm, d = x.shape
    return (x.reshape(spt, tm // spt, d) * s).reshape(tm, d)


def _rows_add(x, s):
    spt = s.shape[0]
    if spt == 1:
        return x + s[0]
    tm, d = x.shape
    return (x.reshape(spt, tm // spt, d) + s).reshape(tm, d)


def _modulate(x, sc, sh):
    return _rows_add(_rows_scale(x, 1.0 + sc), sh)


def _seq_spec(seq_len, tm, d=D_MODEL):
    if tm <= seq_len:
        per = seq_len // tm
        return pl.BlockSpec((1, 1, d), lambda i, *_: (i // per, 0, 0))
    return pl.BlockSpec((tm // seq_len, 1, d), lambda i, *_: (i, 0, 0))


def _row_tile(n_seq, seq_len, pref):
    if seq_len >= pref:
        return _tile(seq_len, pref)
    spt = _tile(n_seq, max(pref // seq_len, 1), mult=1)
    return spt * seq_len


def _full(shape):
    nd = len(shape)
    return pl.BlockSpec(shape, lambda *_: (0,) * nd)


def _cond_kernel(c_ref, w_ref, b_ref, o_ref):
    c = c_ref[...]
    o_ref[0] = _dot3(_silu(c), w_ref[0]) + b_ref[0]


def _cond(c, w_cond, b_cond):
    n_seq = c.shape[0]
    c = jnp.pad(c, ((0, -n_seq % HALO), (0, 0)))
    s = c.shape[0]
    n_out = w_cond.shape[-1]
    tn = _tile(n_out, 1536, LANES)
    return pl.pallas_call(
        _cond_kernel,
        grid=(DEPTH, n_out // tn),
        in_specs=[pl.BlockSpec((s, D_MODEL), lambda l, j: (0, 0)),
                  pl.BlockSpec((1, D_MODEL, tn), lambda l, j: (l, 0, j)),
                  pl.BlockSpec((1, 1, tn), lambda l, j: (l, 0, j))],
        out_specs=pl.BlockSpec((1, s, tn), lambda l, j: (l, 0, j)),
        out_shape=jax.ShapeDtypeStruct((DEPTH, s, n_out), F32),
        compiler_params=_params("parallel", "parallel"),
    )(c, w_cond, b_cond.reshape(DEPTH, 1, n_out))[:, :n_seq]


def _rope_tile(x, cos, sinp, sinm):
    half = MLA_ROPE // 2
    return x * cos + pltpu.roll(x, half, 1) * sinp + pltpu.roll(x, LANES - half, 1) * sinm


def _mla_proj_kernel(x_ref, sc_ref, sh_ref, win_ref, qn_ref, kvn_ref, wuq_ref, cos_ref, sinp_ref, sinm_ref,
                     q_ref, lat_ref, kr_ref):
    h = _modulate(x_ref[...], sc_ref[...], sh_ref[...]).astype(BF16)
    down = _dot(h, win_ref[...])
    cq = down[:, :MLA_Q_LORA]
    cq = cq * lax.rsqrt(jnp.mean(cq * cq, -1, keepdims=True) + RMS_EPS) * qn_ref[...]
    lat = down[:, MLA_Q_LORA:MLA_Q_LORA + MLA_KV_LORA]
    lat_ref[...] = lat * lax.rsqrt(jnp.mean(lat * lat, -1, keepdims=True) + RMS_EPS) * kvn_ref[...]
    cos, sinp, sinm = cos_ref[...], sinp_ref[...], sinm_ref[...]
    kr_ref[...] = _rope_tile(down[:, MLA_Q_LORA + MLA_KV_LORA:], cos, sinp, sinm)
    q = _dot(cq.astype(BF16), wuq_ref[...])
    for hd in range(MLA_HEADS):
        sl = slice(hd * LANES, (hd + 1) * LANES)
        q_ref[:, sl] = (_rope_tile(q[:, sl], cos, sinp, sinm) * (MLA_SCALE * LOG2E)).astype(BF16)


def _mla_proj(x, sc, sh, w, rope_tab, n_seq, seq_len):
    n = x.shape[0]
    tm = _row_tile(n_seq, seq_len, 512)
    tab_rows = rope_tab[0].shape[0]
    per = tab_rows // tm
    tab_spec = pl.BlockSpec((tm, LANES), lambda i: (i % per, 0))
    row = lambda w_: pl.BlockSpec((tm, w_), lambda i: (i, 0))
    return pl.pallas_call(
        _mla_proj_kernel,
        grid=(n // tm,),
        in_specs=[row(D_MODEL), _seq_spec(seq_len, tm), _seq_spec(seq_len, tm),
                  _full(w['w_in'].shape), _full((1, MLA_Q_LORA)), _full((1, MLA_KV_LORA)), _full(w['w_uq'].shape),
                  tab_spec, tab_spec, tab_spec],
        out_specs=[row(MLA_HEADS * LANES), row(MLA_KV_LORA), row(LANES)],
        out_shape=[jax.ShapeDtypeStruct((n, MLA_HEADS * LANES), BF16),
                   jax.ShapeDtypeStruct((n, MLA_KV_LORA), F32),
                   jax.ShapeDtypeStruct((n, LANES), F32)],
        compiler_params=_params("parallel"),
    )(x, sc, sh, w['w_in'], w['q_norm'], w['kv_norm'], w['w_uq'], *rope_tab)


def _kv_expand_kernel(lat_ref, kr_ref, wuk_ref, wuv_ref, one_ref, k_ref, v_ref):
    lat = lat_ref[...].astype(BF16)
    kn = _dot(lat, wuk_ref[...])
    kr = kr_ref[...]
    for hd in range(MLA_HEADS):
        sl = slice(hd * LANES, (hd + 1) * LANES)
        k_ref[:, sl] = (kn[:, sl] + kr).astype(BF16)
    v_ref[...] = (_dot(lat, wuv_ref[...]) + one_ref[...]).astype(BF16)


def _kv_expand(lat, kr, w):
    m = lat.shape[0]
    tm = _tile(m, 1024)
    wide = MLA_HEADS * LANES
    row = lambda w_: pl.BlockSpec((tm, w_), lambda i: (i, 0))
    return pl.pallas_call(
        _kv_expand_kernel,
        grid=(m // tm,),
        in_specs=[row(MLA_KV_LORA), row(LANES), _full((MLA_KV_LORA, wide)), _full((MLA_KV_LORA, wide)),
                  _full((1, wide))],
        out_specs=[row(wide), row(wide)],
        out_shape=[jax.ShapeDtypeStruct((m, wide), BF16)] * 2,
        compiler_params=_params("parallel"),
    )(lat, kr, w['w_uk'], w['w_uv'], w['v_one'])


def _attn_kernel(q_ref, k_ref, v_ref, o_ref, *, tq, tk, n_kv, q_off, hp):
    i = pl.program_id(2)
    qpos0 = q_off + i * tq
    n_full = jnp.minimum(((qpos0 // CHUNK) + 1) * CHUNK // tk, n_kv)
    n_end = jnp.minimum((((qpos0 + tq - 1) // CHUNK + 1) * CHUNK + tk - 1) // tk, n_kv)
    lane = lax.broadcasted_iota(I32, (tq, LANES), 1)
    heads = [slice(hh * LANES, (hh + 1) * LANES) for hh in range(hp)]
    qs = [q_ref[0, :, sl] for sl in heads]

    def step(j, carry, masked):
        ks = pl.multiple_of(j * tk, 16)
        ss = [_dot_nt(q, k_ref[0, pl.ds(ks, tk), sl]) for q, sl in zip(qs, heads)]
        if masked:
            qc = (qpos0 + lax.broadcasted_iota(I32, (tq, tk), 0)) // CHUNK
            kc = (ks + lax.broadcasted_iota(I32, (tq, tk), 1)) // CHUNK
            visible = kc <= qc
            ss = [jnp.where(visible, s, -1e30) for s in ss]
        ms = [jnp.maximum(m, jnp.max(s, axis=-1, keepdims=True)) for (m, _), s in zip(carry, ss)]
        ps = [jnp.exp2(s - m).astype(BF16) for s, m in zip(ss, ms)]
        pvs = [_dot(p, v_ref[0, pl.ds(ks, tk), sl]) for p, sl in zip(ps, heads)]
        return tuple((m_new, acc * jnp.exp2(m - m_new) + pv) for (m, acc), m_new, pv in zip(carry, ms, pvs))

    carry = ((jnp.full((tq, 1), -1e30, F32), jnp.zeros((tq, LANES), F32)),) * hp
    carry = lax.fori_loop(0, n_full, functools.partial(step, masked=False), carry)
    carry = lax.fori_loop(n_full, n_end, functools.partial(step, masked=True), carry)
    outs = [acc / jnp.sum(jnp.where(lane == MLA_V, acc, 0.0), axis=-1, keepdims=True) for _, acc in carry]
    for pr in range(hp // 2):
        o = jnp.where(lane < MLA_V, outs[2 * pr], pltpu.roll(outs[2 * pr + 1], MLA_V, 1))
        o_ref[0, :, pr * LANES:(pr + 1) * LANES] = o.astype(BF16)


def _attention(q, k, v, tq, tk, hp=4):
    b, t_q, _ = q.shape
    t_k = k.shape[1]
    kern = functools.partial(_attn_kernel, tq=tq, tk=tk, n_kv=t_k // tk, q_off=t_k - t_q, hp=hp)
    wide = hp * LANES
    kv_spec = pl.BlockSpec((1, t_k, wide), lambda b_, g, i: (b_, 0, g), pipeline_mode=pl.Buffered(1))
    return pl.pallas_call(
        kern,
        grid=(b, MLA_HEADS // hp, t_q // tq),
        in_specs=[pl.BlockSpec((1, tq, wide), lambda b_, g, i: (b_, i, g)), kv_spec, kv_spec],
        out_specs=pl.BlockSpec((1, tq, hp * MLA_V), lambda b_, g, i: (b_, i, g)),
        out_shape=jax.ShapeDtypeStruct((b, t_q, MLA_HEADS * MLA_V), BF16),
        compiler_params=_params("parallel", "parallel", "arbitrary"),
    )(q, k, v)


def _latent_attn_kernel(q_ref, latn_ref, krn_ref, latp_ref, krp_ref, wuk_ref, wuv_ref, o_ref, *, group):
    latp = latp_ref[0].astype(BF16)
    latn = latn_ref[0].astype(BF16)
    krp = krp_ref[0].astype(BF16)
    krn = krn_ref[0][:, MLA_NOPE:MLA_QK].astype(BF16)
    t = latn.shape[0]
    for g0 in range(0, MLA_HEADS, group):
        hs = range(g0, g0 + group)
        qa = jnp.concatenate([_dot(q_ref[0, :, h * LANES:h * LANES + MLA_NOPE], wuk_ref[h]) for h in hs],
                             axis=0).astype(BF16)
        qr = jnp.concatenate([q_ref[0, :, h * LANES + MLA_NOPE:h * LANES + MLA_QK] for h in hs], axis=0)
        sp = _dot_nt(qa, latp) + _dot_nt(qr, krp)
        sn = _dot_nt(qa, latn) + _dot_nt(qr, krn)
        m = jnp.maximum(jnp.max(sp, axis=-1, keepdims=True), jnp.max(sn, axis=-1, keepdims=True))
        pp = jnp.exp2(sp - m)
        pn = jnp.exp2(sn - m)
        denom = jnp.sum(pp, axis=-1, keepdims=True) + jnp.sum(pn, axis=-1, keepdims=True)
        ctx = ((_dot(pp.astype(BF16), latp) + _dot(pn.astype(BF16), latn)) / denom).astype(BF16)
        outs = [_dot(ctx[i * t:(i + 1) * t], wuv_ref[h]) for i, h in enumerate(hs)]
        for i in range(0, group, 2):
            col = (g0 + i) * MLA_V
            o_ref[0, :, col:col + 2 * MLA_V] = jnp.concatenate([outs[i], outs[i + 1]], axis=1).astype(BF16)


def _latent_attention(q, lat_new, kr_new, cache_lat, cache_kr, j, w_uk_h, w_uv_h):
    b, t, _ = q.shape
    past = cache_lat.shape[2]
    return pl.pallas_call(
        functools.partial(_latent_attn_kernel, group=MLA_HEADS // 2),
        grid=(b,),
        in_specs=[pl.BlockSpec((1, t, MLA_HEADS * LANES), lambda i: (i, 0, 0)),
                  pl.BlockSpec((1, t, MLA_KV_LORA), lambda i: (i, 0, 0)),
                  pl.BlockSpec((1, t, LANES), lambda i: (i, 0, 0)),
                  pl.BlockSpec((None, 1, past, MLA_KV_LORA), lambda i: (j, i, 0, 0)),
                  pl.BlockSpec((None, 1, past, MLA_ROPE), lambda i: (j, i, 0, 0)),
                  _full(w_uk_h.shape), _full(w_uv_h.shape)],
        out_specs=pl.BlockSpec((1, t, MLA_HEADS * MLA_V), lambda i: (i, 0, 0)),
        out_shape=jax.ShapeDtypeStruct((b, t, MLA_HEADS * MLA_V), BF16),
        compiler_params=_params("parallel"),
    )(q, lat_new, kr_new, cache_lat, cache_kr, w_uk_h, w_uv_h)


def _layernorm(r, g, b):
    mu = jnp.mean(r, -1, keepdims=True)
    d = r - mu
    var = jnp.mean(d * d, -1, keepdims=True)
    return d * lax.rsqrt(var + LN_EPS) * g + b


def _mixer_out_kernel(a_ref, wo_ref, x_ref, gm_ref, lng_ref, lnb_ref, sc_ref, sh_ref, wr_ref, br_ref, up_ref, lo_ref,
                      xo_ref, h_ref, gate_ref, pos_ref, cnt_ref):
    y = _dot(a_ref[...], wo_ref[...])
    r = DEEPNORM_ALPHA * x_ref[...] + _rows_scale(y, 1.0 + gm_ref[...])
    xn = _layernorm(r, lng_ref[...], lnb_ref[...])
    xo_ref[...] = xn
    h = _modulate(xn, sc_ref[...], sh_ref[...])
    h_ref[...] = h.astype(BF16)

    logits = _dot3(wr_ref[...], h, _dot_nt) + br_ref[...]
    tm = logits.shape[1]
    eio = lax.broadcasted_iota(I32, (N_EXPERTS, tm), 0).astype(F32)
    sels, vals = [], []
    work = logits
    for k in range(TOP_K):
        m = jnp.max(work, axis=0, keepdims=True)
        ik = jnp.min(jnp.where(work == m, eio, float(N_EXPERTS)), axis=0, keepdims=True)
        sel = eio == ik
        work = jnp.where(sel, -jnp.inf, work)
        sels.append(sel)
        vals.append(m)
    es = [jnp.exp(v - vals[0]) for v in vals]
    tot = es[0] + es[1] + es[2] + es[3]
    for k in range(TOP_K):
        gate_ref[pl.ds(k, 1), :] = es[k] / tot
    multi = sels[0] | sels[1] | sels[2] | sels[3]
    mh = jnp.where(multi, 1.0, 0.0)
    before = _dot(mh.astype(BF16), up_ref[...])
    cnt = jnp.sum(mh, axis=1, keepdims=True)
    seg = jnp.floor((cnt + (SEG - 1)) * (1.0 / SEG)) * SEG
    seg_b = jnp.broadcast_to(seg, (N_EXPERTS, LANES)).astype(BF16)
    place = before + _dot(lo_ref[...], seg_b)[:, :1]
    for k in range(TOP_K):
        pos_ref[pl.ds(k, 1), :] = jnp.sum(jnp.where(sels[k], place, 0.0), axis=0, keepdims=True).astype(I32)
    cnt_ref[0] = cnt


def _mixer_out(a, w_o, x, gm, lng, lnb, sc, sh, w_r_t, b_r, n_seq, seq_len):
    n, kdim = a.shape
    tm = _row_tile(n_seq, seq_len, ROUTE_TILE)
    upper = (jnp.arange(tm)[:, None] < jnp.arange(tm)[None, :]).astype(BF16)
    lower_e = (jnp.arange(N_EXPERTS)[None, :] < jnp.arange(N_EXPERTS)[:, None]).astype(BF16)
    row = lambda w_: pl.BlockSpec((tm, w_), lambda i: (i, 0))
    col = pl.BlockSpec((TOP_K, tm), lambda i: (0, i))
    ss = _seq_spec(seq_len, tm)
    return pl.pallas_call(
        _mixer_out_kernel,
        grid=(n // tm,),
        in_specs=[row(kdim), _full(w_o.shape), row(D_MODEL), ss, _full((1, D_MODEL)), _full((1, D_MODEL)), ss, ss,
                  _full((N_EXPERTS, D_MODEL)), _full((N_EXPERTS, 1)), _full((tm, tm)),
                  _full((N_EXPERTS, N_EXPERTS))],
        out_specs=[row(D_MODEL), row(D_MODEL), col, col, pl.BlockSpec((1, N_EXPERTS, 1), lambda i: (i, 0, 0))],
        out_shape=[jax.ShapeDtypeStruct((n, D_MODEL), F32), jax.ShapeDtypeStruct((n, D_MODEL), BF16),
                   jax.ShapeDtypeStruct((TOP_K, n), F32), jax.ShapeDtypeStruct((TOP_K, n), I32),
                   jax.ShapeDtypeStruct((n // tm, N_EXPERTS, 1), F32)],
        compiler_params=_params("parallel"),
    )(a, w_o, x, gm, lng, lnb, sc, sh, w_r_t, b_r, upper, lower_e)


def _piece_sizes(tm):
    sizes, size = [], SEG
    while size <= tm:
        sizes.append(size)
        size *= 2
    assert sizes[-1] == tm
    return tuple(reversed(sizes))


def _segment_copies(seg_ref, off_ref, start_ref, tile, sizes, make):
    def body(e, carry):
        n = seg_ref[tile * N_EXPERTS + e]
        off = off_ref[tile * N_EXPERTS + e]
        start = start_ref[tile * N_EXPERTS + e]
        for size in sizes:
            done = n & (-2 * size)

            @pl.when((n & size) != 0)
            def _():
                make(pl.multiple_of(off + done, SEG), pl.multiple_of(start + done, SEG), size)
        return carry

    lax.fori_loop(0, N_EXPERTS, body, 0)


def _dispatch_kernel(seg_ref, off_ref, start_ref, h_ref, pos_ref, zero_ref, xr_ref, buf_ref, sem, *, nt, sizes):
    del zero_ref
    i = pl.program_id(0)
    slot = i % 2

    def copies(tile, sl, wait):
        def make(buf_row, hbm_row, size):
            cp = pltpu.make_async_copy(buf_ref.at[sl, pl.ds(buf_row, size)], xr_ref.at[pl.ds(hbm_row, size)],
                                       sem.at[sl])
            cp.wait() if wait else cp.start()
        _segment_copies(seg_ref, off_ref, start_ref, tile, sizes, make)

    @pl.when(i >= 2)
    def _():
        copies(i - 2, slot, True)

    pos = pos_ref[...]
    rows = buf_ref.shape[1]
    rio = lax.broadcasted_iota(I32, (rows, pos.shape[1]), 0)
    sel = (rio == pos[0:1]) | (rio == pos[1:2]) | (rio == pos[2:3]) | (rio == pos[3:4])
    buf_ref[slot] = _dot(jnp.where(sel, 1.0, 0.0).astype(BF16), h_ref[...]).astype(BF16)
    copies(i, slot, False)

    @pl.when(i == nt - 1)
    def _():
        copies(i, slot, True)

        @pl.when(i >= 1)
        def _():
            copies(i - 1, 1 - slot, True)


def _dispatch(h, pos_t, seg, off, start, p_rows, tm):
    n = h.shape[0]
    nt = n // tm
    sizes = _piece_sizes(tm)
    rows = TOP_K * tm + N_EXPERTS * SEG
    grid_spec = pltpu.PrefetchScalarGridSpec(
        num_scalar_prefetch=3,
        grid=(nt,),
        in_specs=[pl.BlockSpec((tm, D_MODEL), lambda i, *_: (i, 0)),
                  pl.BlockSpec((TOP_K, tm), lambda i, *_: (0, i)),
                  pl.BlockSpec(memory_space=pl.ANY)],
        out_specs=pl.BlockSpec(memory_space=pl.ANY),
        scratch_shapes=[pltpu.VMEM((2, rows, D_MODEL), BF16), pltpu.SemaphoreType.DMA((2,))],
    )
    return pl.pallas_call(
        functools.partial(_dispatch_kernel, nt=nt, sizes=sizes),
        grid_spec=grid_spec,
        out_shape=jax.ShapeDtypeStruct((p_rows, D_MODEL), BF16),
        input_output_aliases={5: 0},
        compiler_params=_params("arbitrary"),
    )(seg, off, start, h, pos_t, jnp.zeros((p_rows, D_MODEL), BF16))


def _expert_kernel(be_ref, na_ref, x_ref, wgu_ref, bgu_ref, wd_ref, bd_ref, y_ref, wgu_s, wd_s):
    i = pl.program_id(0)

    @pl.when((i == 0) | (be_ref[i] != be_ref[jnp.maximum(i - 1, 0)]))
    def _():
        wgu_s[...] = wgu_ref[0].astype(BF16)
        wd_s[...] = wd_ref[0].astype(BF16)

    @pl.when(i < na_ref[0])
    def _():
        gu = _dot(x_ref[...], wgu_s[...]) + bgu_ref[0]
        gate = jnp.minimum(gu[:, :D_FF], SWIGLU_LIMIT)
        up = jnp.clip(gu[:, D_FF:], -SWIGLU_LIMIT, SWIGLU_LIMIT)
        act = (up + 1.0) * gate * _sigmoid(SWIGLU_ALPHA * gate)
        y_ref[...] = (_dot(act.astype(BF16), wd_s[...]) + bd_ref[0]).astype(BF16)

    @pl.when(i >= na_ref[0])
    def _():
        y_ref[...] = jnp.zeros_like(y_ref)


def _experts(x_rows, blk_e, n_act, w_gu, b_gu, w_down, b_down, layer):
    p = x_rows.shape[0]
    grid_spec = pltpu.PrefetchScalarGridSpec(
        num_scalar_prefetch=2,
        grid=(p // MOE_ROWS,),
        in_specs=[pl.BlockSpec((MOE_ROWS, D_MODEL), lambda i, be, na: (i, 0)),
                  pl.BlockSpec((None, 1, D_MODEL, 2 * D_FF), lambda i, be, na: (layer, be[i], 0, 0)),
                  pl.BlockSpec((1, 1, 2 * D_FF), lambda i, be, na: (be[i], 0, 0)),
                  pl.BlockSpec((None, 1, D_FF, D_MODEL), lambda i, be, na: (layer, be[i], 0, 0)),
                  pl.BlockSpec((1, 1, D_MODEL), lambda i, be, na: (be[i], 0, 0))],
        out_specs=pl.BlockSpec((MOE_ROWS, D_MODEL), lambda i, be, na: (i, 0)),
        scratch_shapes=[pltpu.VMEM((D_MODEL, 2 * D_FF), BF16), pltpu.VMEM((D_FF, D_MODEL), BF16)],
    )
    return pl.pallas_call(
        _expert_kernel,
        grid_spec=grid_spec,
        out_shape=jax.ShapeDtypeStruct((p, D_MODEL), BF16),
        compiler_params=_params("arbitrary"),
    )(blk_e, n_act, x_rows, w_gu, b_gu, w_down, b_down)


def _combine_kernel(seg_ref, off_ref, start_ref, y_ref, pos_ref, g_ref, x_ref, gf_ref, lng_ref, lnb_ref, o_ref,
                    buf_ref, sem, *, nt, sizes):
    i = pl.program_id(0)
    slot = i % 2

    def copies(tile, sl, wait):
        def make(buf_row, hbm_row, size):
            cp = pltpu.make_async_copy(y_ref.at[pl.ds(hbm_row, size)], buf_ref.at[sl, pl.ds(buf_row, size)],
                                       sem.at[sl])
            cp.wait() if wait else cp.start()
        _segment_copies(seg_ref, off_ref, start_ref, tile, sizes, make)

    @pl.when(i == 0)
    def _():
        buf_ref[...] = jnp.zeros_like(buf_ref)
        copies(0, 0, False)

    @pl.when(i + 1 < nt)
    def _():
        copies(i + 1, 1 - slot, False)

    copies(i, slot, True)
    pos = pos_ref[...]
    g = g_ref[...]
    cio = lax.broadcasted_iota(I32, (pos.shape[0], buf_ref.shape[1]), 1)
    gm = jnp.where(cio == pos[:, 0:1], g[:, 0:1], 0.0)
    for k in range(1, TOP_K):
        gm = gm + jnp.where(cio == pos[:, k:k + 1], g[:, k:k + 1], 0.0)
    y = _dot(gm.astype(BF16), buf_ref[slot])
    r = DEEPNORM_ALPHA * x_ref[...] + _rows_scale(y, 1.0 + gf_ref[...])
    o_ref[...] = _layernorm(r, lng_ref[...], lnb_ref[...])


def _combine(y_rows, pos, gates, seg, off, start, x, gf, lng, lnb, seq_len, tm):
    n = x.shape[0]
    nt = n // tm
    sizes = _piece_sizes(tm)
    rows = TOP_K * tm + N_EXPERTS * SEG
    row = pl.BlockSpec((tm, D_MODEL), lambda i, *_: (i, 0))
    four = pl.BlockSpec((tm, TOP_K), lambda i, *_: (i, 0))
    grid_spec = pltpu.PrefetchScalarGridSpec(
        num_scalar_prefetch=3,
        grid=(nt,),
        in_specs=[pl.BlockSpec(memory_space=pl.ANY), four, four, row, _seq_spec(seq_len, tm),
                  _full((1, D_MODEL)), _full((1, D_MODEL))],
        out_specs=row,
        scratch_shapes=[pltpu.VMEM((2, rows, D_MODEL), BF16), pltpu.SemaphoreType.DMA((2,))],
    )
    return pl.pallas_call(
        functools.partial(_combine_kernel, nt=nt, sizes=sizes),
        grid_spec=grid_spec,
        out_shape=jax.ShapeDtypeStruct((n, D_MODEL), F32),
        compiler_params=_params("arbitrary"),
    )(seg, off, start, y_rows, pos, gates, x, gf, lng, lnb)


def _moe_and_norm(x, h, gate_t, pos_t, cnt, gf, lng, lnb, w, n_seq, seq_len):
    n = x.shape[0]
    tm = _row_tile(n_seq, seq_len, ROUTE_TILE)
    nt = n // tm
    counts = cnt[:, :, 0].astype(I32)
    seg = (counts + SEG - 1) // SEG * SEG
    group = jnp.sum(seg, axis=0)
    padded = (group + MOE_ROWS - 1) // MOE_ROWS * MOE_ROWS
    pad_end = jnp.cumsum(padded)
    start = (pad_end - padded)[None, :] + jnp.cumsum(seg, axis=0) - seg
    off = jnp.cumsum(seg, axis=1) - seg
    worst = n * TOP_K + nt * N_EXPERTS * (SEG - 1) + N_EXPERTS * (MOE_ROWS - 1)
    nb = (worst + MOE_ROWS - 1) // MOE_ROWS
    seg, off, start = seg.reshape(-1), off.reshape(-1), start.reshape(-1)
    x_rows = _dispatch(h, pos_t, seg, off, start, nb * MOE_ROWS, tm)
    blk_row = jnp.arange(nb, dtype=I32) * MOE_ROWS
    blk_e = jnp.minimum(jnp.sum(pad_end[None, :] <= blk_row[:, None], axis=1), N_EXPERTS - 1).astype(I32)
    n_act = (pad_end[-1:] // MOE_ROWS).astype(I32)
    y_rows = _experts(x_rows, blk_e, n_act, w['w_gu'], w['b_gu'], w['w_down'], w['b_down'], w['layer'])
    return _combine(y_rows, pos_t.T, gate_t.T, seg, off, start, x, gf, lng, lnb, seq_len, tm)


def _mod_matmul_kernel(x_ref, sc_ref, sh_ref, w_ref, o_ref):
    h = _modulate(x_ref[...], sc_ref[...], sh_ref[...]).astype(BF16)
    o_ref[...] = _dot(h, w_ref[...]).astype(o_ref.dtype)


def _mod_matmul(x, sc, sh, w, out_dtype, n_seq, seq_len):
    n = x.shape[0]
    n_out = w.shape[1]
    tm = _row_tile(n_seq, seq_len, 1024)
    tn = _tile(n_out, 1024, LANES)
    return pl.pallas_call(
        _mod_matmul_kernel,
        grid=(n // tm, n_out // tn),
        in_specs=[pl.BlockSpec((tm, D_MODEL), lambda i, j: (i, 0)), _seq_spec(seq_len, tm), _seq_spec(seq_len, tm),
                  pl.BlockSpec((D_MODEL, tn), lambda i, j: (0, j))],
        out_specs=pl.BlockSpec((tm, tn), lambda i, j: (i, j)),
        out_shape=jax.ShapeDtypeStruct((n, n_out), out_dtype),
        compiler_params=_params("parallel", "arbitrary"),
    )(x, sc, sh, w)


def _gdn_gates_kernel(x_ref, sc_ref, sh_ref, wab_ref, wabt_ref, alog_ref, dtb_ref, alogt_ref, dtbt_ref,
                      tri_ref, trit_ref, gc_ref, beta_ref, gct_ref):
    h = _modulate(x_ref[...], sc_ref[...], sh_ref[...]).astype(BF16)
    hv = GDN_V_HEADS
    ab = _dot(h, wab_ref[...])
    beta_ref[...] = _sigmoid(ab[:, :hv])
    g = -jnp.exp(alog_ref[...]) * _softplus(ab[:, hv:] + dtb_ref[...])
    gc_ref[...] = _dot_exact_rhs_lhs(tri_ref[...], g)
    abt = _dot_nt(wabt_ref[...], h)
    gt = -jnp.exp(alogt_ref[...]) * _softplus(abt + dtbt_ref[...])
    gct_ref[...] = _dot_exact_rhs(gt, trit_ref[...])


def _dot_exact_rhs_lhs(a01, b):
    b1 = b.astype(BF16)
    r1 = b - b1.astype(F32)
    b2 = r1.astype(BF16)
    b3 = (r1 - b2.astype(F32)).astype(BF16)
    return _dot(a01, b1) + (_dot(a01, b2) + _dot(a01, b3))


def _gdn_gates(x, sc, sh, w, n_seq, seq_len, chunk):
    n = x.shape[0]
    tm = _row_tile(n_seq, seq_len, 512)
    hv = GDN_V_HEADS
    r = jnp.arange(tm)
    tri = ((r[:, None] // chunk == r[None, :] // chunk) & (r[None, :] <= r[:, None])).astype(BF16)
    row = lambda w_: pl.BlockSpec((tm, w_), lambda i: (i, 0))
    return pl.pallas_call(
        _gdn_gates_kernel,
        grid=(n // tm,),
        in_specs=[row(D_MODEL), _seq_spec(seq_len, tm), _seq_spec(seq_len, tm),
                  _full((D_MODEL, 2 * hv)), _full((hv, D_MODEL)), _full((1, hv)), _full((1, hv)),
                  _full((hv, 1)), _full((hv, 1)), _full((tm, tm)), _full((tm, tm))],
        out_specs=[row(hv), row(hv), pl.BlockSpec((hv, tm), lambda i: (0, i))],
        out_shape=[jax.ShapeDtypeStruct((n, hv), F32), jax.ShapeDtypeStruct((n, hv), F32),
                   jax.ShapeDtypeStruct((hv, n), F32)],
        compiler_params=_params("parallel"),
    )(x, sc, sh, w['w_ab'], w['w_a_t'], w['a_log'], w['dt_bias'], w['a_log'].T, w['dt_bias'].T, tri, tri.T)


def _qkv_conv_kernel(x_ref, xh_ref, sc_ref, sh_ref, w_ref, st_ref, cw_ref, o_ref, last_ref, h_s, *, per, n_q, n_qk):
    i = pl.program_id(0)
    j = pl.program_id(1)
    sc, sh = sc_ref[...], sh_ref[...]

    @pl.when(j == 0)
    def _():
        h_s[...] = _modulate(x_ref[...], sc, sh).astype(BF16)

    w = w_ref[...]
    x = _dot(h_s[...], w)
    tm = x.shape[0]
    prev = _dot(_modulate(xh_ref[...], sc, sh).astype(BF16), w)[HALO:]
    halo = jnp.where(i % per == 0, st_ref[0], prev)
    last_ref[0] = x[tm - HALO:]
    cw = cw_ref[...]
    rows = lax.broadcasted_iota(I32, (HALO, x.shape[1]), 0)
    acc = x * cw[GDN_CONV_W - 1:GDN_CONV_W]
    for s in range(1, GDN_CONV_W):
        xs = pltpu.roll(x, s, 0)
        head = jnp.where(rows < s, pltpu.roll(halo, s, 0), xs[:HALO])
        xs = jnp.concatenate([head, xs[HALO:]], axis=0) if tm > HALO else head
        acc = acc + xs * cw[GDN_CONV_W - 1 - s:GDN_CONV_W - s]
    y = _silu(acc)

    @pl.when(j < n_qk)
    def _():
        scale = jnp.where(j < n_q, GDN_DK ** -0.5, 1.0)
        for hd in range(y.shape[1] // GDN_DK):
            sl = slice(hd * GDN_DK, (hd + 1) * GDN_DK)
            yh = y[:, sl]
            o_ref[:, sl] = yh * (lax.rsqrt(jnp.sum(yh * yh, -1, keepdims=True) + L2_EPS) * scale)

    @pl.when(j >= n_qk)
    def _():
        o_ref[...] = y


def _gdn_qkv_conv(x, sc, sh, w_qkv, state8, conv_w, n_seq, seq_len):
    n = x.shape[0]
    assert seq_len >= 2 * HALO
    tm = _tile(seq_len, 1024)
    per = seq_len // tm
    tn = 512
    width = w_qkv.shape[1]
    kern = functools.partial(_qkv_conv_kernel, per=per, n_q=GDN_QK_DIM // tn, n_qk=2 * GDN_QK_DIM // tn)
    hb = tm // (2 * HALO)
    act, last = pl.pallas_call(
        kern,
        grid=(n // tm, width // tn),
        in_specs=[pl.BlockSpec((tm, D_MODEL), lambda i, j: (i, 0)),
                  pl.BlockSpec((2 * HALO, D_MODEL), lambda i, j: (jnp.maximum(i * hb - 1, 0), 0)),
                  _seq_spec(seq_len, tm), _seq_spec(seq_len, tm),
                  pl.BlockSpec((D_MODEL, tn), lambda i, j: (0, j)),
                  pl.BlockSpec((1, HALO, tn), lambda i, j: (i // per, 0, j)),
                  pl.BlockSpec((GDN_CONV_W, tn), lambda i, j: (0, j))],
        out_specs=[pl.BlockSpec((tm, tn), lambda i, j: (i, j)),
                   pl.BlockSpec((1, HALO, tn), lambda i, j: (i, 0, j))],
        out_shape=[jax.ShapeDtypeStruct((n, width), F32), jax.ShapeDtypeStruct((n // tm, HALO, width), F32)],
        scratch_shapes=[pltpu.VMEM((tm, D_MODEL), BF16)],
        compiler_params=_params("parallel", "arbitrary"),
    )(x, x, sc, sh, w_qkv, state8, conv_w)
    return act, last.reshape(n_seq, per, HALO, width)[:, -1]


def _inv_masks(c):
    row = lax.broadcasted_iota(I32, (c, c), 0)
    col = lax.broadcasted_iota(I32, (c, c), 1)
    eye = jnp.where(row == col, 1.0, 0.0)
    diag = row // INV_BASE == col // INV_BASE
    offs = []
    size = INV_BASE
    while size < c:
        offs.append((row // (2 * size) == col // (2 * size)) & (row // size % 2 == 1) & (col // size % 2 == 0))
        size *= 2
    return eye, diag, offs


def _inv_unit_lower(lmats, masks):
    eye, diag, offs = masks
    pws = [jnp.where(diag, lm, 0.0) for lm in lmats]
    ts = [eye - pw for pw in pws]
    size = 2
    while size < INV_BASE:
        pws = [_dotb(pw, pw) for pw in pws]
        ts = [t + _dotb(t, pw) for t, pw in zip(ts, pws)]
        size *= 2
    for off in offs:
        tbs = [_dotb(t, jnp.where(off, lm, 0.0)) for t, lm in zip(ts, lmats)]
        ts = [t - _dotb(tb, t) for t, tb in zip(ts, tbs)]
    return ts


def _gdn_core_kernel(q_ref, k_ref, v_ref, z_ref, gc_ref, beta_ref, gct_ref, s0_ref, nw_ref, o_ref, s_ref, *, c, hb):
    @pl.when(pl.program_id(2) == 0)
    def _():
        s_ref[...] = s0_ref[...]

    row = lax.broadcasted_iota(I32, (c, c), 0)
    col = lax.broadcasted_iota(I32, (c, c), 1)
    lower = col <= row
    strict = col < row
    masks = _inv_masks(c)
    heads = range(hb)
    ksl = [slice(kh * GDN_DK, (kh + 1) * GDN_DK) for kh in range(hb // 2)]
    vsl = [slice(h * GDN_DV, (h + 1) * GDN_DV) for h in heads]
    qn = [q_ref[0, :, sl] for sl in ksl]
    kn = [k_ref[0, :, sl] for sl in ksl]
    kb = [k.astype(BF16) for k in kn]
    kk = [_dot_nt(k, k) for k in kb]
    qk = [_dot_nt(q.astype(BF16), k) for q, k in zip(qn, kb)]
    gcc = [gc_ref[0, 0, :, h:h + 1] for h in heads]
    bc = [beta_ref[0, 0, :, h:h + 1] for h in heads]
    decay = [jnp.where(lower, jnp.exp(jnp.minimum(gcc[h] - gct_ref[0, 0, h:h + 1, :], 0.0)), 0.0) for h in heads]
    lmat = [jnp.where(strict, kk[h // 2] * decay[h], 0.0) * bc[h] for h in heads]
    ts = _inv_unit_lower(lmat, masks)
    egc = [jnp.exp(g) for g in gcc]
    rhs = [jnp.concatenate([v_ref[0, :, vsl[h]] * bc[h], kn[h // 2] * (bc[h] * egc[h])], axis=1).astype(BF16)
           for h in heads]
    uw = [_dot(ts[h].astype(BF16), rhs[h]) for h in heads]
    s_old = [s_ref[0, h] for h in heads]
    sb = [s.astype(BF16) for s in s_old]
    ws = [_dot(uw[h][:, GDN_DV:].astype(BF16), sb[h]) for h in heads]
    qs = [_dot((qn[h // 2] * egc[h]).astype(BF16), sb[h]) for h in heads]
    vb = [(uw[h][:, :GDN_DV] - ws[h]).astype(BF16) for h in heads]
    o = [qs[h] + _dot((qk[h // 2] * decay[h]).astype(BF16), vb[h]) for h in heads]
    glast = [g[c - 1:c, :] for g in gcc]
    kd = [(kn[h // 2] * jnp.exp(glast[h] - gcc[h])).astype(BF16) for h in heads]
    s_new = [s_old[h] * jnp.exp(glast[h]) + _dot_tn(kd[h], vb[h]) for h in heads]
    for h in heads:
        s_ref[0, h] = s_new[h]
        z = z_ref[0, :, vsl[h]].astype(F32)
        on = o[h] * lax.rsqrt(jnp.mean(o[h] * o[h], -1, keepdims=True) + RMS_EPS) * nw_ref[...]
        o_ref[0, :, vsl[h]] = (on * _silu(z)).astype(BF16)


def _gdn_core(qkv, z, gc, beta, gct, s0, norm_w, c, hb=GDN_V_HEADS):
    b, t, _ = z.shape
    ng = GDN_V_HEADS // hb
    kw = hb // 2 * GDN_DK
    vw = hb * GDN_DV
    gc_g = gc.reshape(b, t, ng, hb).transpose(0, 2, 1, 3)
    beta_g = beta.reshape(b, t, ng, hb).transpose(0, 2, 1, 3)
    gct_g = gct.reshape(b, ng, hb, t)
    kern = functools.partial(_gdn_core_kernel, c=c, hb=hb)
    nkb = GDN_QK_DIM // kw
    nvb = 2 * GDN_QK_DIM // vw
    return pl.pallas_call(
        kern,
        grid=(b, ng, t // c),
        in_specs=[pl.BlockSpec((1, c, kw), lambda b_, g, i: (b_, i, g)),
                  pl.BlockSpec((1, c, kw), lambda b_, g, i: (b_, i, nkb + g)),
                  pl.BlockSpec((1, c, vw), lambda b_, g, i: (b_, i, nvb + g)),
                  pl.BlockSpec((1, c, vw), lambda b_, g, i: (b_, i, g)),
                  pl.BlockSpec((1, 1, c, hb), lambda b_, g, i: (b_, g, i, 0)),
                  pl.BlockSpec((1, 1, c, hb), lambda b_, g, i: (b_, g, i, 0)),
                  pl.BlockSpec((1, 1, hb, c), lambda b_, g, i: (b_, g, 0, i)),
                  pl.BlockSpec((1, hb, GDN_DK, GDN_DV), lambda b_, g, i: (b_, g, 0, 0)),
                  _full((1, GDN_DV))],
        out_specs=[pl.BlockSpec((1, c, vw), lambda b_, g, i: (b_, i, g)),
                   pl.BlockSpec((1, hb, GDN_DK, GDN_DV), lambda b_, g, i: (b_, g, 0, 0))],
        out_shape=[jax.ShapeDtypeStruct((b, t, GDN_V_DIM), BF16),
                   jax.ShapeDtypeStruct((b, GDN_V_HEADS, GDN_DK, GDN_DV), F32)],
        compiler_params=_params("parallel", "parallel", "arbitrary"),
    )(qkv, qkv, qkv, z, gc_g, beta_g, gct_g, s0, norm_w)


def _prep_weights(p):
    f = {}
    pad_heads = lambda w_, dh: jnp.pad(w_.reshape(w_.shape[0], w_.shape[1], MLA_HEADS, dh),
                                       ((0, 0), (0, 0), (0, 0), (0, LANES - dh))).reshape(
                                           w_.shape[0], w_.shape[1], MLA_HEADS * LANES)
    w_in = p['mla_w_in']
    nl = w_in.shape[0]
    lo = MLA_Q_LORA + MLA_KV_LORA
    z = lambda k: jnp.zeros((nl, D_MODEL, k), F32)
    f['mla_w_in'] = jnp.concatenate([w_in[..., :lo], z(MLA_NOPE), w_in[..., lo:], z(LANES - MLA_QK)], -1).astype(BF16)
    f['mla_w_uq'] = pad_heads(p['mla_w_uq'], MLA_QK).astype(BF16)
    f['mla_w_uk'] = pad_heads(p['mla_w_uk'], MLA_NOPE).astype(BF16)
    f['mla_w_uv'] = pad_heads(p['mla_w_uv'], MLA_V).astype(BF16)
    f['mla_w_o'] = p['mla_w_o'].astype(BF16)
    per_head = lambda w_, dh: w_.reshape(w_.shape[0], w_.shape[1], MLA_HEADS, dh)
    f['mla_w_uk_t'] = per_head(p['mla_w_uk'], MLA_NOPE).transpose(0, 2, 3, 1).astype(BF16)
    f['mla_w_uv_h'] = per_head(p['mla_w_uv'], MLA_V).transpose(0, 2, 1, 3).astype(BF16)
    f['v_one'] = jnp.tile((jnp.arange(LANES) == MLA_V).astype(F32), MLA_HEADS)[None]
    g_in = p['gdn_w_in']
    f['gdn_w_qkv'] = g_in[..., :GDN_CONV_DIM].astype(BF16)
    f['gdn_w_z'] = g_in[..., GDN_CONV_DIM:GDN_CONV_DIM + GDN_V_DIM].astype(BF16)
    f['gdn_w_ab'] = g_in[..., GDN_CONV_DIM + GDN_V_DIM:].astype(BF16)
    f['gdn_w_a_t'] = jnp.swapaxes(g_in[..., GDN_CONV_DIM + GDN_V_DIM + GDN_V_HEADS:], 1, 2).astype(BF16)
    f['gdn_w_o'] = p['gdn_w_o'].astype(BF16)
    f['moe_w_r_t'] = jnp.swapaxes(p['moe_w_router'], 1, 2)
    return f


def _rope_tables(pos):
    half = MLA_ROPE // 2
    inv_freq = ROPE_BASE ** (-jnp.arange(half, dtype=F32) / half)
    ang = pos.astype(F32)[:, None] * inv_freq[None, :]
    cos, sin = jnp.cos(ang), jnp.sin(ang)
    t = pos.shape[0]
    z = lambda k: jnp.zeros((t, k), F32)
    cos_t = jnp.concatenate([jnp.ones((t, MLA_NOPE), F32), cos, cos, z(LANES - MLA_QK)], -1)
    sinp_t = jnp.concatenate([z(MLA_NOPE + half), sin, z(LANES - MLA_QK)], -1)
    sinm_t = jnp.concatenate([z(MLA_NOPE), -sin, z(half + LANES - MLA_QK)], -1)
    return cos_t, sinp_t, sinm_t


def _trunk(x, c, pos, past, p, f):
    bsz, t, _ = x.shape
    n = bsz * t
    x = x.reshape(n, D_MODEL)
    mod = _cond(c, p['w_cond'], p['b_cond']).reshape(DEPTH, bsz, 6, 1, D_MODEL)
    tm_rope = _row_tile(bsz, t, 512)
    tabs = _rope_tables(pos)
    if tm_rope > t:
        tabs = tuple(jnp.tile(tb, (tm_rope // t, 1)) for tb in tabs)
    lats, krs, convs, ssms = [], [], [], []
    for layer in range(DEPTH):
        sh_m, sc_m, g_m, sh_f, sc_f, g_f = [mod[layer, :, i] for i in range(6)]
        j = layer // 2
        if layer % 2 == 0:
            w = dict(w_in=f['mla_w_in'][j], q_norm=p['mla_q_norm'][j][None], kv_norm=p['mla_kv_norm'][j][None],
                     w_uq=f['mla_w_uq'][j], w_uk=f['mla_w_uk'][j], w_uv=f['mla_w_uv'][j], v_one=f['v_one'])
            q, lat, krt = _mla_proj(x, sc_m, sh_m, w, tabs, bsz, t)
            lats.append(lat.reshape(bsz, t, MLA_KV_LORA))
            krs.append(krt[:, MLA_NOPE:MLA_QK].reshape(bsz, t, MLA_ROPE))
            wide = MLA_HEADS * LANES
            if past is None:
                k, v = _kv_expand(lat, krt, w)
                tq = _tile(t, 512)
                ctx = _attention(q.reshape(bsz, t, wide), k.reshape(bsz, t, wide), v.reshape(bsz, t, wide), tq, tq)
            else:
                ctx = _latent_attention(q.reshape(bsz, t, wide), lat.reshape(bsz, t, -1), krt.reshape(bsz, t, -1),
                                        past[0], past[1], j, f['mla_w_uk_t'][j], f['mla_w_uv_h'][j])
            a, w_o = ctx.reshape(n, MLA_HEADS * MLA_V), f['mla_w_o'][j]
        else:
            chunk = _tile(t, 128)
            w = dict(w_ab=f['gdn_w_ab'][j], w_a_t=f['gdn_w_a_t'][j], a_log=p['gdn_a_log'][j][None],
                     dt_bias=p['gdn_dt_bias'][j][None])
            z = _mod_matmul(x, sc_m, sh_m, f['gdn_w_z'][j], BF16, bsz, t)
            gc, beta, gct = _gdn_gates(x, sc_m, sh_m, w, bsz, t, chunk)
            if past is None:
                conv_state = jnp.zeros((bsz, GDN_CONV_W - 1, GDN_CONV_DIM), F32)
                s0 = jnp.zeros((bsz, GDN_V_HEADS, GDN_DK, GDN_DV), F32)
            else:
                conv_state, s0 = past[2][j], past[3][j]
            state8 = jnp.pad(conv_state, ((0, 0), (HALO - (GDN_CONV_W - 1), 0), (0, 0)))
            qkv, last = _gdn_qkv_conv(x, sc_m, sh_m, f['gdn_w_qkv'][j], state8, p['gdn_conv_w'][j], bsz, t)
            convs.append(last[:, HALO - (GDN_CONV_W - 1):])
            o, s_new = _gdn_core(qkv.reshape(bsz, t, -1), z.reshape(bsz, t, -1),
                                 gc.reshape(bsz, t, -1), beta.reshape(bsz, t, -1),
                                 gct.reshape(GDN_V_HEADS, bsz, t).transpose(1, 0, 2), s0,
                                 p['gdn_norm'][j][None], chunk)
            ssms.append(s_new)
            a, w_o = o.reshape(n, GDN_V_DIM), f['gdn_w_o'][j]
        x, h, gate_t, pos_t, cnt = _mixer_out(
            a, w_o, x, g_m, p['ln1_g'][layer][None], p['ln1_b'][layer][None], sc_f, sh_f,
            f['moe_w_r_t'][layer], p['moe_b_router'][layer][:, None], bsz, t)
        wm = dict(w_gu=p['moe_w_gu'], b_gu=p['moe_b_gu'][layer][:, None], w_down=p['moe_w_down'],
                  b_down=p['moe_b_down'][layer][:, None], layer=layer)
        x = _moe_and_norm(x, h, gate_t, pos_t, cnt, g_f, p['ln2_g'][layer][None], p['ln2_b'][layer][None],
                          wm, bsz, t)
    return x.reshape(bsz, t, D_MODEL), jnp.stack(lats), jnp.stack(krs), jnp.stack(convs), jnp.stack(ssms)


def kernel(x_prompt, x_sample, c_prompt, c_sample, cache_mla_latent, cache_mla_krope, state_gdn_conv, state_gdn_ssm, w_cond, b_cond, ln1_g, ln1_b, ln2_g, ln2_b, mla_w_in, mla_q_norm, mla_kv_norm, mla_w_uq, mla_w_uk, mla_w_uv, mla_w_o, gdn_w_in, gdn_conv_w, gdn_a_log, gdn_dt_bias, gdn_norm, gdn_w_o, moe_w_router, moe_b_router, moe_w_gu, moe_b_gu, moe_w_down, moe_b_down):
    p = dict(w_cond=w_cond, b_cond=b_cond, ln1_g=ln1_g, ln1_b=ln1_b, ln2_g=ln2_g, ln2_b=ln2_b,
             mla_w_in=mla_w_in, mla_q_norm=mla_q_norm, mla_kv_norm=mla_kv_norm, mla_w_uq=mla_w_uq,
             mla_w_uk=mla_w_uk, mla_w_uv=mla_w_uv, mla_w_o=mla_w_o, gdn_w_in=gdn_w_in,
             gdn_conv_w=gdn_conv_w, gdn_a_log=gdn_a_log, gdn_dt_bias=gdn_dt_bias, gdn_norm=gdn_norm,
             gdn_w_o=gdn_w_o, moe_w_router=moe_w_router, moe_b_router=moe_b_router, moe_w_gu=moe_w_gu,
             moe_b_gu=moe_b_gu, moe_w_down=moe_w_down, moe_b_down=moe_b_down)
    f = _prep_weights(p)
    past_len = cache_mla_latent.shape[2]
    assert past_len % CHUNK == 0 and x_sample.shape[1] <= CHUNK
    pos_p = jnp.arange(x_prompt.shape[1], dtype=I32)
    pos_s = past_len + jnp.arange(x_sample.shape[1], dtype=I32)
    y_p, p_lat, p_kr, p_conv, p_ssm = _trunk(x_prompt, c_prompt, pos_p, None, p, f)
    past = (cache_mla_latent, cache_mla_krope, state_gdn_conv, state_gdn_ssm)
    y_s, s_lat, s_kr, s_conv, s_ssm = _trunk(x_sample, c_sample, pos_s, past, p, f)
    return (y_p, y_s, p_lat, p_kr, p_conv, p_ssm, s_lat, s_kr, s_conv, s_ssm)
```

```python
import functools
import math

import jax
import jax.numpy as jnp
from jax import lax
from jax.experimental import pallas as pl
from jax.experimental.pallas import tpu as pltpu

F32 = jnp.float32
BF16 = jnp.bfloat16
I32 = jnp.int32

D_MODEL = 1024
DEPTH = 4
CHUNK = 64
MLA_HEADS = 16
MLA_Q_LORA = 768
MLA_KV_LORA = 256
MLA_NOPE = 64
MLA_ROPE = 32
MLA_V = 64
MLA_QK = MLA_NOPE + MLA_ROPE
MLA_SCALE = MLA_QK ** -0.5
ROPE_BASE = 10000.0
GDN_QK_HEADS = 8
GDN_V_HEADS = 16
GDN_DK = 128
GDN_DV = 128
GDN_QK_DIM = GDN_QK_HEADS * GDN_DK
GDN_V_DIM = GDN_V_HEADS * GDN_DV
GDN_CONV_DIM = 2 * GDN_QK_DIM + GDN_V_DIM
GDN_CONV_W = 4
N_EXPERTS = 32
TOP_K = 4
D_FF = D_MODEL
SWIGLU_ALPHA = 1.702
SWIGLU_LIMIT = 7.0
DEEPNORM_ALPHA = (2 * DEPTH) ** 0.25
LN_EPS = 1e-5
RMS_EPS = 1e-6
L2_EPS = 1e-6

LANES = 128
HALO = 8
LOG2E = 1.4426950408889634
MOE_ROWS = 512
ROUTE_TILE = 512
SEG = 16
INV_BASE = 16
VMEM_LIMIT = 48 * 1024 * 1024


def _params(*sem):
    return pltpu.CompilerParams(dimension_semantics=sem, vmem_limit_bytes=VMEM_LIMIT)


def _tile(n, pref, mult=8):
    t = min(pref, n)
    while t >= mult:
        if n % t == 0 and t % mult == 0:
            return t
        t -= 1
    return n


def _dot(a, b):
    return jnp.dot(a, b, preferred_element_type=F32)


def _dot_nt(a, b):
    return lax.dot_general(a, b, (((1,), (1,)), ((), ())), preferred_element_type=F32)


def _dot_tn(a, b):
    return lax.dot_general(a, b, (((0,), (0,)), ((), ())), preferred_element_type=F32)


def _dotb(a, b):
    return _dot(a.astype(BF16), b.astype(BF16))


def _split(a):
    hi = a.astype(BF16)
    lo = (a - hi.astype(F32)).astype(BF16)
    return hi, lo


def _dot3(a, b, dot=_dot):
    ah, al = _split(a)
    bh, bl = _split(b)
    return dot(ah, bh) + (dot(ah, bl) + dot(al, bh))


def _dot_exact_rhs(a, b01, dot=_dot):
    a1 = a.astype(BF16)
    r1 = a - a1.astype(F32)
    a2 = r1.astype(BF16)
    a3 = (r1 - a2.astype(F32)).astype(BF16)
    return dot(a1, b01) + (dot(a2, b01) + dot(a3, b01))


def _sigmoid(x):
    return 1.0 / (1.0 + jnp.exp(-x))


def _silu(x):
    return x * _sigmoid(x)


def _softplus(x):
    return jnp.maximum(x, 0.0) + jnp.log(1.0 + jnp.exp(-jnp.abs(x)))


def _rows_scale(x, s):
    spt = s.shape[0]
    if spt == 1:
        return x * s[0]
    tm, d = x.shape
    return (x.reshape(spt, tm // spt, d) * s).reshape(tm, d)


def _rows_add(x, s):
    spt = s.shape[0]
    if spt == 1:
        return x + s[0]
    tm, d = x.shape
    return (x.reshape(spt, tm // spt, d) + s).reshape(tm, d)


def _modulate(x, sc, sh):
    return _rows_add(_rows_scale(x, 1.0 + sc), sh)


def _seq_spec(seq_len, tm, d=D_MODEL):
    if tm <= seq_len:
        per = seq_len // tm
        return pl.BlockSpec((1, 1, d), lambda i, *_: (i // per, 0, 0))
    return pl.BlockSpec((tm // seq_len, 1, d), lambda i, *_: (i, 0, 0))


def _row_tile(n_seq, seq_len, pref):
    if seq_len >= pref:
        return _tile(seq_len, pref)
    spt = _tile(n_seq, max(pref // seq_len, 1), mult=1)
    return spt * seq_len


def _full(shape):
    nd = len(shape)
    return pl.BlockSpec(shape, lambda *_: (0,) * nd)


def _cond_kernel(c_ref, w_ref, b_ref, o_ref):
    c = c_ref[...]
    o_ref[0] = _dot3(_silu(c), w_ref[0]) + b_ref[0]


def _cond(c, w_cond, b_cond):
    n_seq = c.shape[0]
    c = jnp.pad(c, ((0, -n_seq % HALO), (0, 0)))
    s = c.shape[0]
    n_out = w_cond.shape[-1]
    tn = _tile(n_out, 1536, LANES)
    return pl.pallas_call(
        _cond_kernel,
        grid=(DEPTH, n_out // tn),
        in_specs=[pl.BlockSpec((s, D_MODEL), lambda l, j: (0, 0)),
                  pl.BlockSpec((1, D_MODEL, tn), lambda l, j: (l, 0, j)),
                  pl.BlockSpec((1, 1, tn), lambda l, j: (l, 0, j))],
        out_specs=pl.BlockSpec((1, s, tn), lambda l, j: (l, 0, j)),
        out_shape=jax.ShapeDtypeStruct((DEPTH, s, n_out), F32),
        compiler_params=_params("parallel", "parallel"),
    )(c, w_cond, b_cond.reshape(DEPTH, 1, n_out))[:, :n_seq]


def _rope_tile(x, cos, sinp, sinm):
    half = MLA_ROPE // 2
    return x * cos + pltpu.roll(x, half, 1) * sinp + pltpu.roll(x, LANES - half, 1) * sinm


def _mla_proj_kernel(x_ref, sc_ref, sh_ref, win_ref, qn_ref, kvn_ref, wuq_ref, cos_ref, sinp_ref, sinm_ref,
                     q_ref, lat_ref, kr_ref):
    h = _modulate(x_ref[...], sc_ref[...], sh_ref[...]).astype(BF16)
    down = _dot(h, win_ref[...])
    cq = down[:, :MLA_Q_LORA]
    cq = cq * lax.rsqrt(jnp.mean(cq * cq, -1, keepdims=True) + RMS_EPS) * qn_ref[...]
    lat = down[:, MLA_Q_LORA:MLA_Q_LORA + MLA_KV_LORA]
    lat_ref[...] = lat * lax.rsqrt(jnp.mean(lat * lat, -1, keepdims=True) + RMS_EPS) * kvn_ref[...]
    cos, sinp, sinm = cos_ref[...], sinp_ref[...], sinm_ref[...]
    kr_ref[...] = _rope_tile(down[:, MLA_Q_LORA + MLA_KV_LORA:], cos, sinp, sinm)
    q = _dot(cq.astype(BF16), wuq_ref[...])
    for hd in range(MLA_HEADS):
        sl = slice(hd * LANES, (hd + 1) * LANES)
        q_ref[:, sl] = (_rope_tile(q[:, sl], cos, sinp, sinm) * (MLA_SCALE * LOG2E)).astype(BF16)


def _mla_proj(x, sc, sh, w, rope_tab, n_seq, seq_len):
    n = x.shape[0]
    tm = _row_tile(n_seq, seq_len, 512)
    tab_rows = rope_tab[0].shape[0]
    per = tab_rows // tm
    tab_spec = pl.BlockSpec((tm, LANES), lambda i: (i % per, 0))
    row = lambda w_: pl.BlockSpec((tm, w_), lambda i: (i, 0))
    return pl.pallas_call(
        _mla_proj_kernel,
        grid=(n // tm,),
        in_specs=[row(D_MODEL), _seq_spec(seq_len, tm), _seq_spec(seq_len, tm),
                  _full(w['w_in'].shape), _full((1, MLA_Q_LORA)), _full((1, MLA_KV_LORA)), _full(w['w_uq'].shape),
                  tab_spec, tab_spec, tab_spec],
        out_specs=[row(MLA_HEADS * LANES), row(MLA_KV_LORA), row(LANES)],
        out_shape=[jax.ShapeDtypeStruct((n, MLA_HEADS * LANES), BF16),
                   jax.ShapeDtypeStruct((n, MLA_KV_LORA), F32),
                   jax.ShapeDtypeStruct((n, LANES), F32)],
        compiler_params=_params("parallel"),
    )(x, sc, sh, w['w_in'], w['q_norm'], w['kv_norm'], w['w_uq'], *rope_tab)


def _kv_expand_kernel(lat_ref, kr_ref, wuk_ref, wuv_ref, one_ref, k_ref, v_ref):
    lat = lat_ref[...].astype(BF16)
    kn = _dot(lat, wuk_ref[...])
    kr = kr_ref[...]
    for hd in range(MLA_HEADS):
        sl = slice(hd * LANES, (hd + 1) * LANES)
        k_ref[:, sl] = (kn[:, sl] + kr).astype(BF16)
    v_ref[...] = (_dot(lat, wuv_ref[...]) + one_ref[...]).astype(BF16)


def _kv_expand(lat, kr, w):
    m = lat.shape[0]
    tm = _tile(m, 1024)
    wide = MLA_HEADS * LANES
    row = lambda w_: pl.BlockSpec((tm, w_), lambda i: (i, 0))
    return pl.pallas_call(
        _kv_expand_kernel,
        grid=(m // tm,),
        in_specs=[row(MLA_KV_LORA), row(LANES), _full((MLA_KV_LORA, wide)), _full((MLA_KV_LORA, wide)),
                  _full((1, wide))],
        out_specs=[row(wide), row(wide)],
        out_shape=[jax.ShapeDtypeStruct((m, wide), BF16)] * 2,
        compiler_params=_params("parallel"),
    )(lat, kr, w['w_uk'], w['w_uv'], w['v_one'])


def _attn_kernel(q_ref, k_ref, v_ref, o_ref, *, tq, tk, n_kv, q_off, hp):
    i = pl.program_id(2)
    qpos0 = q_off + i * tq
    n_full = jnp.minimum(((qpos0 // CHUNK) + 1) * CHUNK // tk, n_kv)
    n_end = jnp.minimum((((qpos0 + tq - 1) // CHUNK + 1) * CHUNK + tk - 1) // tk, n_kv)
    lane = lax.broadcasted_iota(I32, (tq, LANES), 1)
    heads = [slice(hh * LANES, (hh + 1) * LANES) for hh in range(hp)]
    qs = [q_ref[0, :, sl] for sl in heads]

    def step(j, carry, masked, width):
        ks = pl.multiple_of(j * width, 16)
        ss = [_dot_nt(q, k_ref[0, pl.ds(ks, width), sl]) for q, sl in zip(qs, heads)]
        if masked:
            qc = (qpos0 + lax.broadcasted_iota(I32, (tq, width), 0)) // CHUNK
            kc = (ks + lax.broadcasted_iota(I32, (tq, width), 1)) // CHUNK
            visible = kc <= qc
            ss = [jnp.where(visible, s, -1e30) for s in ss]
        ms = [jnp.maximum(m, jnp.max(s, axis=-1, keepdims=True)) for (m, _), s in zip(carry, ss)]
        ps = [jnp.exp2(s - m).astype(BF16) for s, m in zip(ss, ms)]
        pvs = [_dot(p, v_ref[0, pl.ds(ks, width), sl]) for p, sl in zip(ps, heads)]
        return tuple((m_new, acc * jnp.exp2(m - m_new) + pv) for (m, acc), m_new, pv in zip(carry, ms, pvs))

    carry = ((jnp.full((tq, 1), -1e30, F32), jnp.zeros((tq, LANES), F32)),) * hp
    carry = lax.fori_loop(0, n_full, functools.partial(step, masked=False, width=tk), carry)
    carry = lax.fori_loop(n_full, n_end, functools.partial(step, masked=True, width=tk), carry)
    outs = [acc / jnp.sum(jnp.where(lane == MLA_V, acc, 0.0), axis=-1, keepdims=True) for _, acc in carry]
    for pr in range(hp // 2):
        o = jnp.where(lane < MLA_V, outs[2 * pr], pltpu.roll(outs[2 * pr + 1], MLA_V, 1))
        o_ref[0, :, pr * LANES:(pr + 1) * LANES] = o.astype(BF16)


def _attention(q, k, v, tq, tk, hp=4):
    b, t_q, _ = q.shape
    t_k = k.shape[1]
    kern = functools.partial(_attn_kernel, tq=tq, tk=tk, n_kv=t_k // tk, q_off=t_k - t_q, hp=hp)
    wide = hp * LANES
    kv_spec = pl.BlockSpec((1, t_k, wide), lambda b_, g, i: (b_, 0, g), pipeline_mode=pl.Buffered(1))
    return pl.pallas_call(
        kern,
        grid=(b, MLA_HEADS // hp, t_q // tq),
        in_specs=[pl.BlockSpec((1, tq, wide), lambda b_, g, i: (b_, i, g)), kv_spec, kv_spec],
        out_specs=pl.BlockSpec((1, tq, hp * MLA_V), lambda b_, g, i: (b_, i, g)),
        out_shape=jax.ShapeDtypeStruct((b, t_q, MLA_HEADS * MLA_V), BF16),
        compiler_params=_params("parallel", "parallel", "arbitrary"),
    )(q, k, v)


def _latent_attn_kernel(q_ref, latn_ref, krn_ref, latp_ref, krp_ref, wuk_ref, wuv_ref, o_ref, *, group):
    latp = latp_ref[0].astype(BF16)
    latn = latn_ref[0].astype(BF16)
    krp = krp_ref[0].astype(BF16)
    krn = krn_ref[0][:, MLA_NOPE:MLA_QK].astype(BF16)
    t = latn.shape[0]
    for g0 in range(0, MLA_HEADS, group):
        hs = range(g0, g0 + group)
        qa = jnp.concatenate([_dot(q_ref[0, :, h * LANES:h * LANES + MLA_NOPE], wuk_ref[h]) for h in hs],
                             axis=0).astype(BF16)
        qr = jnp.concatenate([q_ref[0, :, h * LANES + MLA_NOPE:h * LANES + MLA_QK] for h in hs], axis=0)
        sp = _dot_nt(qa, latp) + _dot_nt(qr, krp)
        sn = _dot_nt(qa, latn) + _dot_nt(qr, krn)
        m = jnp.maximum(jnp.max(sp, axis=-1, keepdims=True), jnp.max(sn, axis=-1, keepdims=True))
        pp = jnp.exp2(sp - m)
        pn = jnp.exp2(sn - m)
        denom = jnp.sum(pp, axis=-1, keepdims=True) + jnp.sum(pn, axis=-1, keepdims=True)
        ctx = ((_dot(pp.astype(BF16), latp) + _dot(pn.astype(BF16), latn)) / denom).astype(BF16)
        outs = [_dot(ctx[i * t:(i + 1) * t], wuv_ref[h]) for i, h in enumerate(hs)]
        for i in range(0, group, 2):
            col = (g0 + i) * MLA_V
            o_ref[0, :, col:col + 2 * MLA_V] = jnp.concatenate([outs[i], outs[i + 1]], axis=1).astype(BF16)


def _latent_attention(q, lat_new, kr_new, cache_lat, cache_kr, j, w_uk_h, w_uv_h):
    b, t, _ = q.shape
    past = cache_lat.shape[2]
    return pl.pallas_call(
        functools.partial(_latent_attn_kernel, group=MLA_HEADS // 2),
        grid=(b,),
        in_specs=[pl.BlockSpec((1, t, MLA_HEADS * LANES), lambda i: (i, 0, 0)),
                  pl.BlockSpec((1, t, MLA_KV_LORA), lambda i: (i, 0, 0)),
                  pl.BlockSpec((1, t, LANES), lambda i: (i, 0, 0)),
                  pl.BlockSpec((None, 1, past, MLA_KV_LORA), lambda i: (j, i, 0, 0)),
                  pl.BlockSpec((None, 1, past, MLA_ROPE), lambda i: (j, i, 0, 0)),
                  _full(w_uk_h.shape), _full(w_uv_h.shape)],
        out_specs=pl.BlockSpec((1, t, MLA_HEADS * MLA_V), lambda i: (i, 0, 0)),
        out_shape=jax.ShapeDtypeStruct((b, t, MLA_HEADS * MLA_V), BF16),
        compiler_params=_params("parallel"),
    )(q, lat_new, kr_new, cache_lat, cache_kr, w_uk_h, w_uv_h)


def _layernorm(r, g, b):
    mu = jnp.mean(r, -1, keepdims=True)
    d = r - mu
    var = jnp.mean(d * d, -1, keepdims=True)
    return d * lax.rsqrt(var + LN_EPS) * g + b


def _mixer_out_kernel(a_ref, wo_ref, x_ref, gm_ref, lng_ref, lnb_ref, sc_ref, sh_ref, wr_ref, br_ref, up_ref, lo_ref,
                      xo_ref, h_ref, gate_ref, pos_ref, cnt_ref):
    y = _dot(a_ref[...], wo_ref[...])
    r = DEEPNORM_ALPHA * x_ref[...] + _rows_scale(y, 1.0 + gm_ref[...])
    xn = _layernorm(r, lng_ref[...], lnb_ref[...])
    xo_ref[...] = xn
    h = _modulate(xn, sc_ref[...], sh_ref[...])
    h_ref[...] = h.astype(BF16)

    logits = _dot3(wr_ref[...], h, _dot_nt) + br_ref[...]
    tm = logits.shape[1]
    eio = lax.broadcasted_iota(I32, (N_EXPERTS, tm), 0).astype(F32)
    sels, vals = [], []
    work = logits
    for k in range(TOP_K):
        m = jnp.max(work, axis=0, keepdims=True)
        ik = jnp.min(jnp.where(work == m, eio, float(N_EXPERTS)), axis=0, keepdims=True)
        sel = eio == ik
        work = jnp.where(sel, -jnp.inf, work)
        sels.append(sel)
        vals.append(m)
    es = [jnp.exp(v - vals[0]) for v in vals]
    tot = es[0] + es[1] + es[2] + es[3]
    for k in range(TOP_K):
        gate_ref[pl.ds(k, 1), :] = es[k] / tot
    multi = sels[0] | sels[1] | sels[2] | sels[3]
    mh = jnp.where(multi, 1.0, 0.0)
    before = _dot(mh.astype(BF16), up_ref[...])
    cnt = jnp.sum(mh, axis=1, keepdims=True)
    seg = jnp.floor((cnt + (SEG - 1)) * (1.0 / SEG)) * SEG
    seg_b = jnp.broadcast_to(seg, (N_EXPERTS, LANES)).astype(BF16)
    place = before + _dot(lo_ref[...], seg_b)[:, :1]
    for k in range(TOP_K):
        pos_ref[pl.ds(k, 1), :] = jnp.sum(jnp.where(sels[k], place, 0.0), axis=0, keepdims=True).astype(I32)
    cnt_ref[0] = cnt


def _mixer_out(a, w_o, x, gm, lng, lnb, sc, sh, w_r_t, b_r, n_seq, seq_len):
    n, kdim = a.shape
    tm = _row_tile(n_seq, seq_len, ROUTE_TILE)
    upper = (jnp.arange(tm)[:, None] < jnp.arange(tm)[None, :]).astype(BF16)
    lower_e = (jnp.arange(N_EXPERTS)[None, :] < jnp.arange(N_EXPERTS)[:, None]).astype(BF16)
    row = lambda w_: pl.BlockSpec((tm, w_), lambda i: (i, 0))
    col = pl.BlockSpec((TOP_K, tm), lambda i: (0, i))
    ss = _seq_spec(seq_len, tm)
    return pl.pallas_call(
        _mixer_out_kernel,
        grid=(n // tm,),
        in_specs=[row(kdim), _full(w_o.shape), row(D_MODEL), ss, _full((1, D_MODEL)), _full((1, D_MODEL)), ss, ss,
                  _full((N_EXPERTS, D_MODEL)), _full((N_EXPERTS, 1)), _full((tm, tm)),
                  _full((N_EXPERTS, N_EXPERTS))],
        out_specs=[row(D_MODEL), row(D_MODEL), col, col, pl.BlockSpec((1, N_EXPERTS, 1), lambda i: (i, 0, 0))],
        out_shape=[jax.ShapeDtypeStruct((n, D_MODEL), F32), jax.ShapeDtypeStruct((n, D_MODEL), BF16),
                   jax.ShapeDtypeStruct((TOP_K, n), F32), jax.ShapeDtypeStruct((TOP_K, n), I32),
                   jax.ShapeDtypeStruct((n // tm, N_EXPERTS, 1), F32)],
        compiler_params=_params("parallel"),
    )(a, w_o, x, gm, lng, lnb, sc, sh, w_r_t, b_r, upper, lower_e)


def _piece_sizes(tm):
    sizes, size = [], SEG
    while size <= tm:
        sizes.append(size)
        size *= 2
    assert sizes[-1] == tm
    return tuple(reversed(sizes))


def _segment_copies(seg_ref, off_ref, start_ref, tile, sizes, make):
    def body(e, carry):
        n = seg_ref[tile * N_EXPERTS + e]
        off = off_ref[tile * N_EXPERTS + e]
        start = start_ref[tile * N_EXPERTS + e]
        for size in sizes:
            done = n & (-2 * size)

            @pl.when((n & size) != 0)
            def _():
                make(pl.multiple_of(off + done, SEG), pl.multiple_of(start + done, SEG), size)
        return carry

    lax.fori_loop(0, N_EXPERTS, body, 0)


def _dispatch_kernel(seg_ref, off_ref, start_ref, h_ref, pos_ref, zero_ref, xr_ref, buf_ref, sem, *, nt, sizes):
    del zero_ref
    i = pl.program_id(0)
    slot = i % 2

    def copies(tile, sl, wait):
        def make(buf_row, hbm_row, size):
            cp = pltpu.make_async_copy(buf_ref.at[sl, pl.ds(buf_row, size)], xr_ref.at[pl.ds(hbm_row, size)],
                                       sem.at[sl])
            cp.wait() if wait else cp.start()
        _segment_copies(seg_ref, off_ref, start_ref, tile, sizes, make)

    @pl.when(i >= 2)
    def _():
        copies(i - 2, slot, True)

    pos = pos_ref[...]
    rows = buf_ref.shape[1]
    rio = lax.broadcasted_iota(I32, (rows, pos.shape[1]), 0)
    sel = (rio == pos[0:1]) | (rio == pos[1:2]) | (rio == pos[2:3]) | (rio == pos[3:4])
    buf_ref[slot] = _dot(jnp.where(sel, 1.0, 0.0).astype(BF16), h_ref[...]).astype(BF16)
    copies(i, slot, False)

    @pl.when(i == nt - 1)
    def _():
        copies(i, slot, True)

        @pl.when(i >= 1)
        def _():
            copies(i - 1, 1 - slot, True)


def _dispatch(h, pos_t, seg, off, start, x_rows, tm):
    n = h.shape[0]
    nt = n // tm
    sizes = _piece_sizes(tm)
    rows = TOP_K * tm + N_EXPERTS * SEG
    grid_spec = pltpu.PrefetchScalarGridSpec(
        num_scalar_prefetch=3,
        grid=(nt,),
        in_specs=[pl.BlockSpec((tm, D_MODEL), lambda i, *_: (i, 0)),
                  pl.BlockSpec((TOP_K, tm), lambda i, *_: (0, i)),
                  pl.BlockSpec(memory_space=pl.ANY)],
        out_specs=pl.BlockSpec(memory_space=pl.ANY),
        scratch_shapes=[pltpu.VMEM((2, rows, D_MODEL), BF16), pltpu.SemaphoreType.DMA((2,))],
    )
    return pl.pallas_call(
        functools.partial(_dispatch_kernel, nt=nt, sizes=sizes),
        grid_spec=grid_spec,
        out_shape=jax.ShapeDtypeStruct(x_rows.shape, BF16),
        input_output_aliases={5: 0},
        compiler_params=_params("arbitrary"),
    )(seg, off, start, h, pos_t, x_rows)


def _expert_kernel(be_ref, na_ref, x_ref, wgu_ref, bgu_ref, wd_ref, bd_ref, y_ref, wgu_s, wd_s):
    i = pl.program_id(0)

    @pl.when((i == 0) | (be_ref[i] != be_ref[jnp.maximum(i - 1, 0)]))
    def _():
        wgu_s[...] = wgu_ref[0].astype(BF16)
        wd_s[...] = wd_ref[0].astype(BF16)

    @pl.when(i < na_ref[0])
    def _():
        gu = _dot(x_ref[...], wgu_s[...]) + bgu_ref[0]
        gate = jnp.minimum(gu[:, :D_FF], SWIGLU_LIMIT)
        up = jnp.clip(gu[:, D_FF:], -SWIGLU_LIMIT, SWIGLU_LIMIT)
        act = (up + 1.0) * gate * _sigmoid(SWIGLU_ALPHA * gate)
        y_ref[...] = (_dot(act.astype(BF16), wd_s[...]) + bd_ref[0]).astype(BF16)

    @pl.when(i >= na_ref[0])
    def _():
        y_ref[...] = jnp.zeros_like(y_ref)


def _experts(x_rows, blk_e, n_act, w_gu, b_gu, w_down, b_down, layer):
    p = x_rows.shape[0]
    grid_spec = pltpu.PrefetchScalarGridSpec(
        num_scalar_prefetch=2,
        grid=(p // MOE_ROWS,),
        in_specs=[pl.BlockSpec((MOE_ROWS, D_MODEL), lambda i, be, na: (i, 0)),
                  pl.BlockSpec((None, 1, D_MODEL, 2 * D_FF), lambda i, be, na: (layer, be[i], 0, 0)),
                  pl.BlockSpec((1, 1, 2 * D_FF), lambda i, be, na: (be[i], 0, 0)),
                  pl.BlockSpec((None, 1, D_FF, D_MODEL), lambda i, be, na: (layer, be[i], 0, 0)),
                  pl.BlockSpec((1, 1, D_MODEL), lambda i, be, na: (be[i], 0, 0))],
        out_specs=pl.BlockSpec((MOE_ROWS, D_MODEL), lambda i, be, na: (i, 0)),
        scratch_shapes=[pltpu.VMEM((D_MODEL, 2 * D_FF), BF16), pltpu.VMEM((D_FF, D_MODEL), BF16)],
    )
    return pl.pallas_call(
        _expert_kernel,
        grid_spec=grid_spec,
        out_shape=jax.ShapeDtypeStruct((p, D_MODEL), BF16),
        compiler_params=_params("arbitrary"),
    )(blk_e, n_act, x_rows, w_gu, b_gu, w_down, b_down)


def _combine_kernel(seg_ref, off_ref, start_ref, y_ref, pos_ref, g_ref, x_ref, gf_ref, lng_ref, lnb_ref, o_ref,
                    buf_ref, sem, *, nt, sizes):
    i = pl.program_id(0)
    slot = i % 2

    def copies(tile, sl, wait):
        def make(buf_row, hbm_row, size):
            cp = pltpu.make_async_copy(y_ref.at[pl.ds(hbm_row, size)], buf_ref.at[sl, pl.ds(buf_row, size)],
                                       sem.at[sl])
            cp.wait() if wait else cp.start()
        _segment_copies(seg_ref, off_ref, start_ref, tile, sizes, make)

    @pl.when(i == 0)
    def _():
        buf_ref[...] = jnp.zeros_like(buf_ref)
        copies(0, 0, False)

    @pl.when(i + 1 < nt)
    def _():
        copies(i + 1, 1 - slot, False)

    copies(i, slot, True)
    pos = pos_ref[...]
    g = g_ref[...]
    cio = lax.broadcasted_iota(I32, (pos.shape[0], buf_ref.shape[1]), 1)
    gm = jnp.where(cio == pos[:, 0:1], g[:, 0:1], 0.0)
    for k in range(1, TOP_K):
        gm = gm + jnp.where(cio == pos[:, k:k + 1], g[:, k:k + 1], 0.0)
    y = _dot(gm.astype(BF16), buf_ref[slot])
    r = DEEPNORM_ALPHA * x_ref[...] + _rows_scale(y, 1.0 + gf_ref[...])
    o_ref[...] = _layernorm(r, lng_ref[...], lnb_ref[...])


def _combine(y_rows, pos, gates, seg, off, start, x, gf, lng, lnb, seq_len, tm):
    n = x.shape[0]
    nt = n // tm
    sizes = _piece_sizes(tm)
    rows = TOP_K * tm + N_EXPERTS * SEG
    row = pl.BlockSpec((tm, D_MODEL), lambda i, *_: (i, 0))
    four = pl.BlockSpec((tm, TOP_K), lambda i, *_: (i, 0))
    grid_spec = pltpu.PrefetchScalarGridSpec(
        num_scalar_prefetch=3,
        grid=(nt,),
        in_specs=[pl.BlockSpec(memory_space=pl.ANY), four, four, row, _seq_spec(seq_len, tm),
                  _full((1, D_MODEL)), _full((1, D_MODEL))],
        out_specs=row,
        scratch_shapes=[pltpu.VMEM((2, rows, D_MODEL), BF16), pltpu.SemaphoreType.DMA((2,))],
    )
    return pl.pallas_call(
        functools.partial(_combine_kernel, nt=nt, sizes=sizes),
        grid_spec=grid_spec,
        out_shape=jax.ShapeDtypeStruct((n, D_MODEL), F32),
        compiler_params=_params("arbitrary"),
    )(seg, off, start, y_rows, pos, gates, x, gf, lng, lnb)


def _moe_rows(groups):
    worst = N_EXPERTS * (MOE_ROWS - 1)
    for g in groups:
        n = g['n_seq'] * g['seq_len']
        worst += n * TOP_K + n // _row_tile(g['n_seq'], g['seq_len'], ROUTE_TILE) * N_EXPERTS * (SEG - 1)
    return (worst + MOE_ROWS - 1) // MOE_ROWS * MOE_ROWS


def _moe_and_norm(routed, lng, lnb, w, x_rows):
    tms = [_row_tile(r['n_seq'], r['seq_len'], ROUTE_TILE) for r in routed]
    counts = jnp.concatenate([r['cnt'][:, :, 0] for r in routed], axis=0).astype(I32)
    seg = (counts + SEG - 1) // SEG * SEG
    padded = (jnp.sum(seg, axis=0) + MOE_ROWS - 1) // MOE_ROWS * MOE_ROWS
    pad_end = jnp.cumsum(padded)
    start = (pad_end - padded)[None, :] + jnp.cumsum(seg, axis=0) - seg
    off = jnp.cumsum(seg, axis=1) - seg
    tables, t0 = [], 0
    for r in routed:
        nt = r['cnt'].shape[0]
        tables.append(tuple(a[t0:t0 + nt].reshape(-1) for a in (seg, off, start)))
        t0 += nt
    for r, tm, tab in zip(routed, tms, tables):
        x_rows = _dispatch(r['h'], r['pos_t'], *tab, x_rows, tm)
    nb = x_rows.shape[0] // MOE_ROWS
    blk_row = jnp.arange(nb, dtype=I32) * MOE_ROWS
    blk_e = jnp.minimum(jnp.sum(pad_end[None, :] <= blk_row[:, None], axis=1), N_EXPERTS - 1).astype(I32)
    n_act = (pad_end[-1:] // MOE_ROWS).astype(I32)
    y_rows = _experts(x_rows, blk_e, n_act, w['w_gu'], w['b_gu'], w['w_down'], w['b_down'], w['layer'])
    outs = [_combine(y_rows, r['pos_t'].T, r['gate_t'].T, *tab, r['x'], r['gf'], lng, lnb, r['seq_len'], tm)
            for r, tm, tab in zip(routed, tms, tables)]
    return outs, x_rows


def _mod_matmul_kernel(x_ref, sc_ref, sh_ref, w_ref, o_ref):
    h = _modulate(x_ref[...], sc_ref[...], sh_ref[...]).astype(BF16)
    o_ref[...] = _dot(h, w_ref[...]).astype(o_ref.dtype)


def _mod_matmul(x, sc, sh, w, out_dtype, n_seq, seq_len):
    n = x.shape[0]
    n_out = w.shape[1]
    tm = _row_tile(n_seq, seq_len, 1024)
    tn = _tile(n_out, 1024, LANES)
    return pl.pallas_call(
        _mod_matmul_kernel,
        grid=(n // tm, n_out // tn),
        in_specs=[pl.BlockSpec((tm, D_MODEL), lambda i, j: (i, 0)), _seq_spec(seq_len, tm), _seq_spec(seq_len, tm),
                  pl.BlockSpec((D_MODEL, tn), lambda i, j: (0, j))],
        out_specs=pl.BlockSpec((tm, tn), lambda i, j: (i, j)),
        out_shape=jax.ShapeDtypeStruct((n, n_out), out_dtype),
        compiler_params=_params("parallel", "arbitrary"),
    )(x, sc, sh, w)


def _gdn_gates_kernel(x_ref, sc_ref, sh_ref, wab_ref, wabt_ref, alog_ref, dtb_ref, alogt_ref, dtbt_ref,
                      tri_ref, trit_ref, gc_ref, beta_ref, gct_ref):
    h = _modulate(x_ref[...], sc_ref[...], sh_ref[...]).astype(BF16)
    hv = GDN_V_HEADS
    ab = _dot(h, wab_ref[...])
    beta_ref[...] = _sigmoid(ab[:, :hv])
    g = -jnp.exp(alog_ref[...]) * _softplus(ab[:, hv:] + dtb_ref[...])
    gc_ref[...] = _dot_exact_rhs_lhs(tri_ref[...], g)
    abt = _dot_nt(wabt_ref[...], h)
    gt = -jnp.exp(alogt_ref[...]) * _softplus(abt + dtbt_ref[...])
    gct_ref[...] = _dot_exact_rhs(gt, trit_ref[...])


def _dot_exact_rhs_lhs(a01, b):
    b1 = b.astype(BF16)
    r1 = b - b1.astype(F32)
    b2 = r1.astype(BF16)
    b3 = (r1 - b2.astype(F32)).astype(BF16)
    return _dot(a01, b1) + (_dot(a01, b2) + _dot(a01, b3))


def _gdn_gates(x, sc, sh, w, n_seq, seq_len, chunk):
    n = x.shape[0]
    tm = _row_tile(n_seq, seq_len, 512)
    hv = GDN_V_HEADS
    r = jnp.arange(tm)
    tri = ((r[:, None] // chunk == r[None, :] // chunk) & (r[None, :] <= r[:, None])).astype(BF16)
    row = lambda w_: pl.BlockSpec((tm, w_), lambda i: (i, 0))
    return pl.pallas_call(
        _gdn_gates_kernel,
        grid=(n // tm,),
        in_specs=[row(D_MODEL), _seq_spec(seq_len, tm), _seq_spec(seq_len, tm),
                  _full((D_MODEL, 2 * hv)), _full((hv, D_MODEL)), _full((1, hv)), _full((1, hv)),
                  _full((hv, 1)), _full((hv, 1)), _full((tm, tm)), _full((tm, tm))],
        out_specs=[row(hv), row(hv), pl.BlockSpec((hv, tm), lambda i: (0, i))],
        out_shape=[jax.ShapeDtypeStruct((n, hv), F32), jax.ShapeDtypeStruct((n, hv), F32),
                   jax.ShapeDtypeStruct((hv, n), F32)],
        compiler_params=_params("parallel"),
    )(x, sc, sh, w['w_ab'], w['w_a_t'], w['a_log'], w['dt_bias'], w['a_log'].T, w['dt_bias'].T, tri, tri.T)


def _qkv_conv_kernel(x_ref, xh_ref, sc_ref, sh_ref, w_ref, st_ref, cw_ref, o_ref, last_ref, h_s, *, per, n_q, n_qk):
    i = pl.program_id(0)
    j = pl.program_id(1)
    sc, sh = sc_ref[...], sh_ref[...]

    @pl.when(j == 0)
    def _():
        h_s[...] = _modulate(x_ref[...], sc, sh).astype(BF16)

    w = w_ref[...]
    x = _dot(h_s[...], w)
    tm = x.shape[0]
    prev = _dot(_modulate(xh_ref[...], sc, sh).astype(BF16), w)[HALO:]
    halo = jnp.where(i % per == 0, st_ref[0], prev)
    last_ref[0] = x[tm - HALO:]
    cw = cw_ref[...]
    rows = lax.broadcasted_iota(I32, (HALO, x.shape[1]), 0)
    acc = x * cw[GDN_CONV_W - 1:GDN_CONV_W]
    for s in range(1, GDN_CONV_W):
        xs = pltpu.roll(x, s, 0)
        head = jnp.where(rows < s, pltpu.roll(halo, s, 0), xs[:HALO])
        xs = jnp.concatenate([head, xs[HALO:]], axis=0) if tm > HALO else head
        acc = acc + xs * cw[GDN_CONV_W - 1 - s:GDN_CONV_W - s]
    y = _silu(acc)

    @pl.when(j < n_qk)
    def _():
        scale = jnp.where(j < n_q, GDN_DK ** -0.5, 1.0)
        for hd in range(y.shape[1] // GDN_DK):
            sl = slice(hd * GDN_DK, (hd + 1) * GDN_DK)
            yh = y[:, sl]
            o_ref[:, sl] = yh * (lax.rsqrt(jnp.sum(yh * yh, -1, keepdims=True) + L2_EPS) * scale)

    @pl.when(j >= n_qk)
    def _():
        o_ref[...] = y


def _gdn_qkv_conv(x, sc, sh, w_qkv, state8, conv_w, n_seq, seq_len):
    n = x.shape[0]
    assert seq_len >= 2 * HALO
    tm = _tile(seq_len, 1024)
    per = seq_len // tm
    tn = 512 if tm >= 512 else GDN_QK_DIM
    width = w_qkv.shape[1]
    kern = functools.partial(_qkv_conv_kernel, per=per, n_q=GDN_QK_DIM // tn, n_qk=2 * GDN_QK_DIM // tn)
    hb = tm // (2 * HALO)
    act, last = pl.pallas_call(
        kern,
        grid=(n // tm, width // tn),
        in_specs=[pl.BlockSpec((tm, D_MODEL), lambda i, j: (i, 0)),
                  pl.BlockSpec((2 * HALO, D_MODEL), lambda i, j: (jnp.maximum(i * hb - 1, 0), 0)),
                  _seq_spec(seq_len, tm), _seq_spec(seq_len, tm),
                  pl.BlockSpec((D_MODEL, tn), lambda i, j: (0, j)),
                  pl.BlockSpec((1, HALO, tn), lambda i, j: (i // per, 0, j)),
                  pl.BlockSpec((GDN_CONV_W, tn), lambda i, j: (0, j))],
        out_specs=[pl.BlockSpec((tm, tn), lambda i, j: (i, j)),
                   pl.BlockSpec((1, HALO, tn), lambda i, j: (i, 0, j))],
        out_shape=[jax.ShapeDtypeStruct((n, width), F32), jax.ShapeDtypeStruct((n // tm, HALO, width), F32)],
        scratch_shapes=[pltpu.VMEM((tm, D_MODEL), BF16)],
        compiler_params=_params("parallel", "arbitrary"),
    )(x, x, sc, sh, w_qkv, state8, conv_w)
    return act, last.reshape(n_seq, per, HALO, width)[:, -1]


def _inv_masks(c):
    row = lax.broadcasted_iota(I32, (c, c), 0)
    col = lax.broadcasted_iota(I32, (c, c), 1)
    eye = jnp.where(row == col, 1.0, 0.0)
    diag = row // INV_BASE == col // INV_BASE
    offs = []
    size = INV_BASE
    while size < c:
        offs.append((row // (2 * size) == col // (2 * size)) & (row // size % 2 == 1) & (col // size % 2 == 0))
        size *= 2
    return eye, diag, offs


def _inv_unit_lower(lmats, masks):
    eye, diag, offs = masks
    pws = [jnp.where(diag, lm, 0.0) for lm in lmats]
    ts = [eye - pw for pw in pws]
    size = 2
    while size < INV_BASE:
        pws = [_dotb(pw, pw) for pw in pws]
        ts = [t + _dotb(t, pw) for t, pw in zip(ts, pws)]
        size *= 2
    for off in offs:
        tbs = [_dotb(t, jnp.where(off, lm, 0.0)) for t, lm in zip(ts, lmats)]
        ts = [t - _dotb(tb, t) for t, tb in zip(ts, tbs)]
    return ts


def _gdn_core_kernel(q_ref, k_ref, v_ref, z_ref, gc_ref, beta_ref, gct_ref, s0_ref, nw_ref, o_ref, s_ref, *, c, hb):
    @pl.when(pl.program_id(2) == 0)
    def _():
        s_ref[...] = s0_ref[...]

    row = lax.broadcasted_iota(I32, (c, c), 0)
    col = lax.broadcasted_iota(I32, (c, c), 1)
    lower = col <= row
    strict = col < row
    masks = _inv_masks(c)
    heads = range(hb)
    ksl = [slice(kh * GDN_DK, (kh + 1) * GDN_DK) for kh in range(hb // 2)]
    vsl = [slice(h * GDN_DV, (h + 1) * GDN_DV) for h in heads]
    qn = [q_ref[0, :, sl] for sl in ksl]
    kn = [k_ref[0, :, sl] for sl in ksl]
    kb = [k.astype(BF16) for k in kn]
    kk = [_dot_nt(k, k) for k in kb]
    qk = [_dot_nt(q.astype(BF16), k) for q, k in zip(qn, kb)]
    gcc = [gc_ref[0, 0, :, h:h + 1] for h in heads]
    bc = [beta_ref[0, 0, :, h:h + 1] for h in heads]
    decay = [jnp.where(lower, jnp.exp(jnp.minimum(gcc[h] - gct_ref[0, 0, h:h + 1, :], 0.0)), 0.0) for h in heads]
    lmat = [jnp.where(strict, kk[h // 2] * decay[h], 0.0) * bc[h] for h in heads]
    ts = _inv_unit_lower(lmat, masks)
    egc = [jnp.exp(g) for g in gcc]
    rhs = [jnp.concatenate([v_ref[0, :, vsl[h]] * bc[h], kn[h // 2] * (bc[h] * egc[h])], axis=1).astype(BF16)
           for h in heads]
    uw = [_dot(ts[h].astype(BF16), rhs[h]) for h in heads]
    s_old = [s_ref[0, h] for h in heads]
    sb = [s.astype(BF16) for s in s_old]
    ws = [_dot(uw[h][:, GDN_DV:].astype(BF16), sb[h]) for h in heads]
    qs = [_dot((qn[h // 2] * egc[h]).astype(BF16), sb[h]) for h in heads]
    vb = [(uw[h][:, :GDN_DV] - ws[h]).astype(BF16) for h in heads]
    o = [qs[h] + _dot((qk[h // 2] * decay[h]).astype(BF16), vb[h]) for h in heads]
    glast = [g[c - 1:c, :] for g in gcc]
    kd = [(kn[h // 2] * jnp.exp(glast[h] - gcc[h])).astype(BF16) for h in heads]
    s_new = [s_old[h] * jnp.exp(glast[h]) + _dot_tn(kd[h], vb[h]) for h in heads]
    for h in heads:
        s_ref[0, h] = s_new[h]
        z = z_ref[0, :, vsl[h]].astype(F32)
        on = o[h] * lax.rsqrt(jnp.mean(o[h] * o[h], -1, keepdims=True) + RMS_EPS) * nw_ref[...]
        o_ref[0, :, vsl[h]] = (on * _silu(z)).astype(BF16)


def _gdn_core(qkv, z, gc, beta, gct, s0, norm_w, c, hb=GDN_V_HEADS):
    b, t, _ = z.shape
    ng = GDN_V_HEADS // hb
    kw = hb // 2 * GDN_DK
    vw = hb * GDN_DV
    gc_g = gc.reshape(b, t, ng, hb).transpose(0, 2, 1, 3)
    beta_g = beta.reshape(b, t, ng, hb).transpose(0, 2, 1, 3)
    gct_g = gct.reshape(b, ng, hb, t)
    kern = functools.partial(_gdn_core_kernel, c=c, hb=hb)
    nkb = GDN_QK_DIM // kw
    nvb = 2 * GDN_QK_DIM // vw
    return pl.pallas_call(
        kern,
        grid=(b, ng, t // c),
        in_specs=[pl.BlockSpec((1, c, kw), lambda b_, g, i: (b_, i, g)),
                  pl.BlockSpec((1, c, kw), lambda b_, g, i: (b_, i, nkb + g)),
                  pl.BlockSpec((1, c, vw), lambda b_, g, i: (b_, i, nvb + g)),
                  pl.BlockSpec((1, c, vw), lambda b_, g, i: (b_, i, g)),
                  pl.BlockSpec((1, 1, c, hb), lambda b_, g, i: (b_, g, i, 0)),
                  pl.BlockSpec((1, 1, c, hb), lambda b_, g, i: (b_, g, i, 0)),
                  pl.BlockSpec((1, 1, hb, c), lambda b_, g, i: (b_, g, 0, i)),
                  pl.BlockSpec((1, hb, GDN_DK, GDN_DV), lambda b_, g, i: (b_, g, 0, 0)),
                  _full((1, GDN_DV))],
        out_specs=[pl.BlockSpec((1, c, vw), lambda b_, g, i: (b_, i, g)),
                   pl.BlockSpec((1, hb, GDN_DK, GDN_DV), lambda b_, g, i: (b_, g, 0, 0))],
        out_shape=[jax.ShapeDtypeStruct((b, t, GDN_V_DIM), BF16),
                   jax.ShapeDtypeStruct((b, GDN_V_HEADS, GDN_DK, GDN_DV), F32)],
        compiler_params=_params("parallel", "parallel", "arbitrary"),
    )(qkv, qkv, qkv, z, gc_g, beta_g, gct_g, s0, norm_w)


def _prep_weights(p):
    f = {}
    pad_heads = lambda w_, dh: jnp.pad(w_.reshape(w_.shape[0], w_.shape[1], MLA_HEADS, dh),
                                       ((0, 0), (0, 0), (0, 0), (0, LANES - dh))).reshape(
                                           w_.shape[0], w_.shape[1], MLA_HEADS * LANES)
    w_in = p['mla_w_in']
    nl = w_in.shape[0]
    lo = MLA_Q_LORA + MLA_KV_LORA
    z = lambda k: jnp.zeros((nl, D_MODEL, k), F32)
    f['mla_w_in'] = jnp.concatenate([w_in[..., :lo], z(MLA_NOPE), w_in[..., lo:], z(LANES - MLA_QK)], -1).astype(BF16)
    f['mla_w_uq'] = pad_heads(p['mla_w_uq'], MLA_QK).astype(BF16)
    f['mla_w_uk'] = pad_heads(p['mla_w_uk'], MLA_NOPE).astype(BF16)
    f['mla_w_uv'] = pad_heads(p['mla_w_uv'], MLA_V).astype(BF16)
    f['mla_w_o'] = p['mla_w_o'].astype(BF16)
    per_head = lambda w_, dh: w_.reshape(w_.shape[0], w_.shape[1], MLA_HEADS, dh)
    f['mla_w_uk_t'] = per_head(p['mla_w_uk'], MLA_NOPE).transpose(0, 2, 3, 1).astype(BF16)
    f['mla_w_uv_h'] = per_head(p['mla_w_uv'], MLA_V).transpose(0, 2, 1, 3).astype(BF16)
    f['v_one'] = jnp.tile((jnp.arange(LANES) == MLA_V).astype(F32), MLA_HEADS)[None]
    g_in = p['gdn_w_in']
    f['gdn_w_qkv'] = g_in[..., :GDN_CONV_DIM].astype(BF16)
    f['gdn_w_z'] = g_in[..., GDN_CONV_DIM:GDN_CONV_DIM + GDN_V_DIM].astype(BF16)
    f['gdn_w_ab'] = g_in[..., GDN_CONV_DIM + GDN_V_DIM:].astype(BF16)
    f['gdn_w_a_t'] = jnp.swapaxes(g_in[..., GDN_CONV_DIM + GDN_V_DIM + GDN_V_HEADS:], 1, 2).astype(BF16)
    f['gdn_w_o'] = p['gdn_w_o'].astype(BF16)
    f['moe_w_r_t'] = jnp.swapaxes(p['moe_w_router'], 1, 2)
    return f


def _rope_tables(pos):
    half = MLA_ROPE // 2
    inv_freq = ROPE_BASE ** (-jnp.arange(half, dtype=F32) / half)
    ang = pos.astype(F32)[:, None] * inv_freq[None, :]
    cos, sin = jnp.cos(ang), jnp.sin(ang)
    t = pos.shape[0]
    z = lambda k: jnp.zeros((t, k), F32)
    cos_t = jnp.concatenate([jnp.ones((t, MLA_NOPE), F32), cos, cos, z(LANES - MLA_QK)], -1)
    sinp_t = jnp.concatenate([z(MLA_NOPE + half), sin, z(LANES - MLA_QK)], -1)
    sinm_t = jnp.concatenate([z(MLA_NOPE), -sin, z(half + LANES - MLA_QK)], -1)
    return cos_t, sinp_t, sinm_t


def _mixer(g, layer, mods, p, f):
    x, bsz, t, past = g['x'], g['n_seq'], g['seq_len'], g['past']
    sh_m, sc_m = mods[0], mods[1]
    n = bsz * t
    j = layer // 2
    if layer % 2 == 0:
        w = dict(w_in=f['mla_w_in'][j], q_norm=p['mla_q_norm'][j][None], kv_norm=p['mla_kv_norm'][j][None],
                 w_uq=f['mla_w_uq'][j], w_uk=f['mla_w_uk'][j], w_uv=f['mla_w_uv'][j], v_one=f['v_one'])
        q, lat, krt = _mla_proj(x, sc_m, sh_m, w, g['tabs'], bsz, t)
        g['lats'].append(lat.reshape(bsz, t, MLA_KV_LORA))
        g['krs'].append(krt[:, MLA_NOPE:MLA_QK].reshape(bsz, t, MLA_ROPE))
        wide = MLA_HEADS * LANES
        if past is None:
            k, v = _kv_expand(lat, krt, w)
            tq = _tile(t, 512)
            ctx = _attention(q.reshape(bsz, t, wide), k.reshape(bsz, t, wide), v.reshape(bsz, t, wide), tq, tq)
        else:
            ctx = _latent_attention(q.reshape(bsz, t, wide), lat.reshape(bsz, t, -1), krt.reshape(bsz, t, -1),
                                    past[0], past[1], j, f['mla_w_uk_t'][j], f['mla_w_uv_h'][j])
        return ctx.reshape(n, MLA_HEADS * MLA_V), f['mla_w_o'][j]
    chunk = _tile(t, 128)
    w = dict(w_ab=f['gdn_w_ab'][j], w_a_t=f['gdn_w_a_t'][j], a_log=p['gdn_a_log'][j][None],
             dt_bias=p['gdn_dt_bias'][j][None])
    z = _mod_matmul(x, sc_m, sh_m, f['gdn_w_z'][j], BF16, bsz, t)
    gc, beta, gct = _gdn_gates(x, sc_m, sh_m, w, bsz, t, chunk)
    if past is None:
        conv_state = jnp.zeros((bsz, GDN_CONV_W - 1, GDN_CONV_DIM), F32)
        s0 = jnp.zeros((bsz, GDN_V_HEADS, GDN_DK, GDN_DV), F32)
    else:
        conv_state, s0 = past[2][j], past[3][j]
    state8 = jnp.pad(conv_state, ((0, 0), (HALO - (GDN_CONV_W - 1), 0), (0, 0)))
    qkv, last = _gdn_qkv_conv(x, sc_m, sh_m, f['gdn_w_qkv'][j], state8, p['gdn_conv_w'][j], bsz, t)
    g['convs'].append(last[:, HALO - (GDN_CONV_W - 1):])
    o, s_new = _gdn_core(qkv.reshape(bsz, t, -1), z.reshape(bsz, t, -1), gc.reshape(bsz, t, -1),
                         beta.reshape(bsz, t, -1), gct.reshape(GDN_V_HEADS, bsz, t).transpose(1, 0, 2), s0,
                         p['gdn_norm'][j][None], chunk)
    g['ssms'].append(s_new)
    return o.reshape(n, GDN_V_DIM), f['gdn_w_o'][j]


def _forward(groups, p, f):
    for g in groups:
        bsz, t, _ = g['x'].shape
        g.update(n_seq=bsz, seq_len=t, x=g['x'].reshape(bsz * t, D_MODEL), lats=[], krs=[], convs=[], ssms=[],
                 mod=_cond(g['c'], p['w_cond'], p['b_cond']).reshape(DEPTH, bsz, 6, 1, D_MODEL))
        tabs = _rope_tables(g['pos'])
        tm_rope = _row_tile(bsz, t, 512)
        g['tabs'] = tuple(jnp.tile(tb, (tm_rope // t, 1)) for tb in tabs) if tm_rope > t else tabs
    x_rows = jnp.zeros((_moe_rows(groups), D_MODEL), BF16)
    for layer in range(DEPTH):
        routed = []
        for g in groups:
            sh_m, sc_m, g_m, sh_f, sc_f, g_f = [g['mod'][layer, :, i] for i in range(6)]
            a, w_o = _mixer(g, layer, (sh_m, sc_m), p, f)
            x, h, gate_t, pos_t, cnt = _mixer_out(
                a, w_o, g['x'], g_m, p['ln1_g'][layer][None], p['ln1_b'][layer][None], sc_f, sh_f,
                f['moe_w_r_t'][layer], p['moe_b_router'][layer][:, None], g['n_seq'], g['seq_len'])
            routed.append(dict(x=x, h=h, gate_t=gate_t, pos_t=pos_t, cnt=cnt, gf=g_f, n_seq=g['n_seq'],
                               seq_len=g['seq_len']))
        wm = dict(w_gu=p['moe_w_gu'], b_gu=p['moe_b_gu'][layer][:, None], w_down=p['moe_w_down'],
                  b_down=p['moe_b_down'][layer][:, None], layer=layer)
        xs, x_rows = _moe_and_norm(routed, p['ln2_g'][layer][None], p['ln2_b'][layer][None], wm, x_rows)
        for g, x in zip(groups, xs):
            g['x'] = x
    return [(g['x'].reshape(g['n_seq'], g['seq_len'], D_MODEL), jnp.stack(g['lats']), jnp.stack(g['krs']),
             jnp.stack(g['convs']), jnp.stack(g['ssms'])) for g in groups]


def kernel(x_prompt, x_sample, c_prompt, c_sample, cache_mla_latent, cache_mla_krope, state_gdn_conv, state_gdn_ssm, w_cond, b_cond, ln1_g, ln1_b, ln2_g, ln2_b, mla_w_in, mla_q_norm, mla_kv_norm, mla_w_uq, mla_w_uk, mla_w_uv, mla_w_o, gdn_w_in, gdn_conv_w, gdn_a_log, gdn_dt_bias, gdn_norm, gdn_w_o, moe_w_router, moe_b_router, moe_w_gu, moe_b_gu, moe_w_down, moe_b_down):
    p = dict(w_cond=w_cond, b_cond=b_cond, ln1_g=ln1_g, ln1_b=ln1_b, ln2_g=ln2_g, ln2_b=ln2_b,
             mla_w_in=mla_w_in, mla_q_norm=mla_q_norm, mla_kv_norm=mla_kv_norm, mla_w_uq=mla_w_uq,
             mla_w_uk=mla_w_uk, mla_w_uv=mla_w_uv, mla_w_o=mla_w_o, gdn_w_in=gdn_w_in,
             gdn_conv_w=gdn_conv_w, gdn_a_log=gdn_a_log, gdn_dt_bias=gdn_dt_bias, gdn_norm=gdn_norm,
             gdn_w_o=gdn_w_o, moe_w_router=moe_w_router, moe_b_router=moe_b_router, moe_w_gu=moe_w_gu,
             moe_b_gu=moe_b_gu, moe_w_down=moe_w_down, moe_b_down=moe_b_down)
    f = _prep_weights(p)
    past_len = cache_mla_latent.shape[2]
    assert past_len % CHUNK == 0 and x_sample.shape[1] <= CHUNK
    past = (cache_mla_latent, cache_mla_krope, state_gdn_conv, state_gdn_ssm)
    prompt = dict(x=x_prompt, c=c_prompt, pos=jnp.arange(x_prompt.shape[1], dtype=I32), past=None)
    sample = dict(x=x_sample, c=c_sample, pos=past_len + jnp.arange(x_sample.shape[1], dtype=I32), past=past)
    (y_p, p_lat, p_kr, p_conv, p_ssm), (y_s, s_lat, s_kr, s_conv, s_ssm) = _forward([prompt, sample], p, f)
    return (y_p, y_s, p_lat, p_kr, p_conv, p_ssm, s_lat, s_kr, s_conv, s_ssm)
```

```python
import functools
import math

import jax
import jax.numpy as jnp
from jax import lax
from jax.experimental import pallas as pl
from jax.experimental.pallas import tpu as pltpu

F32 = jnp.float32
BF16 = jnp.bfloat16
I32 = jnp.int32

D_MODEL = 1024
DEPTH = 4
CHUNK = 64
MLA_HEADS = 16
MLA_Q_LORA = 768
MLA_KV_LORA = 256
MLA_NOPE = 64
MLA_ROPE = 32
MLA_V = 64
MLA_QK = MLA_NOPE + MLA_ROPE
MLA_SCALE = MLA_QK ** -0.5
ROPE_BASE = 10000.0
GDN_QK_HEADS = 8
GDN_V_HEADS = 16
GDN_DK = 128
GDN_DV = 128
GDN_QK_DIM = GDN_QK_HEADS * GDN_DK
GDN_V_DIM = GDN_V_HEADS * GDN_DV
GDN_CONV_DIM = 2 * GDN_QK_DIM + GDN_V_DIM
GDN_CONV_W = 4
N_EXPERTS = 32
TOP_K = 4
D_FF = D_MODEL
SWIGLU_ALPHA = 1.702
SWIGLU_LIMIT = 7.0
DEEPNORM_ALPHA = (2 * DEPTH) ** 0.25
LN_EPS = 1e-5
RMS_EPS = 1e-6
L2_EPS = 1e-6

LANES = 128
HALO = 8
LOG2E = 1.4426950408889634
MOE_ROWS = 512
ROUTE_TILE = 512
SEG = 16
INV_BASE = 16
VMEM_LIMIT = 48 * 1024 * 1024


def _params(*sem):
    return pltpu.CompilerParams(dimension_semantics=sem, vmem_limit_bytes=VMEM_LIMIT)


def _tile(n, pref, mult=8):
    t = min(pref, n)
    while t >= mult:
        if n % t == 0 and t % mult == 0:
            return t
        t -= 1
    return n


def _dot(a, b):
    return jnp.dot(a, b, preferred_element_type=F32)


def _dot_nt(a, b):
    return lax.dot_general(a, b, (((1,), (1,)), ((), ())), preferred_element_type=F32)


def _dot_tn(a, b):
    return lax.dot_general(a, b, (((0,), (0,)), ((), ())), preferred_element_type=F32)


def _dotb(a, b):
    return _dot(a.astype(BF16), b.astype(BF16))


def _split(a):
    hi = a.astype(BF16)
    lo = (a - hi.astype(F32)).astype(BF16)
    return hi, lo


def _dot3(a, b, dot=_dot):
    ah, al = _split(a)
    bh, bl = _split(b)
    return dot(ah, bh) + (dot(ah, bl) + dot(al, bh))


def _dot_exact_rhs(a, b01, dot=_dot):
    a1 = a.astype(BF16)
    r1 = a - a1.astype(F32)
    a2 = r1.astype(BF16)
    a3 = (r1 - a2.astype(F32)).astype(BF16)
    return dot(a1, b01) + (dot(a2, b01) + dot(a3, b01))


def _sigmoid(x):
    return 1.0 / (1.0 + jnp.exp(-x))


def _silu(x):
    return x * _sigmoid(x)


def _softplus(x):
    return jnp.maximum(x, 0.0) + jnp.log(1.0 + jnp.exp(-jnp.abs(x)))


def _rows_scale(x, s):
    spt = s.shape[0]
    if spt == 1:
        return x * s[0]
    tm, d = x.shape
    return (x.reshape(spt, tm // spt, d) * s).reshape(tm, d)


def _rows_add(x, s):
    spt = s.shape[0]
    if spt == 1:
        return x + s[0]
    tm, d = x.shape
    return (x.reshape(spt, tm // spt, d) + s).reshape(tm, d)


def _modulate(x, sc, sh):
    return _rows_add(_rows_scale(x, 1.0 + sc), sh)


def _seq_spec(seq_len, tm, d=D_MODEL):
    if tm <= seq_len:
        per = seq_len // tm
        return pl.BlockSpec((1, 1, d), lambda i, *_: (i // per, 0, 0))
    return pl.BlockSpec((tm // seq_len, 1, d), lambda i, *_: (i, 0, 0))


def _row_tile(n_seq, seq_len, pref):
    if seq_len >= pref:
        return _tile(seq_len, pref)
    spt = _tile(n_seq, max(pref // seq_len, 1), mult=1)
    return spt * seq_len


def _full(shape):
    nd = len(shape)
    return pl.BlockSpec(shape, lambda *_: (0,) * nd)


def _cond_kernel(c_ref, w_ref, b_ref, o_ref):
    c = c_ref[...]
    o_ref[0] = _dot3(_silu(c), w_ref[0]) + b_ref[0]


def _cond(c, w_cond, b_cond):
    n_seq = c.shape[0]
    c = jnp.pad(c, ((0, -n_seq % HALO), (0, 0)))
    s = c.shape[0]
    n_out = w_cond.shape[-1]
    tn = _tile(n_out, 1536, LANES)
    return pl.pallas_call(
        _cond_kernel,
        grid=(DEPTH, n_out // tn),
        in_specs=[pl.BlockSpec((s, D_MODEL), lambda l, j: (0, 0)),
                  pl.BlockSpec((1, D_MODEL, tn), lambda l, j: (l, 0, j)),
                  pl.BlockSpec((1, 1, tn), lambda l, j: (l, 0, j))],
        out_specs=pl.BlockSpec((1, s, tn), lambda l, j: (l, 0, j)),
        out_shape=jax.ShapeDtypeStruct((DEPTH, s, n_out), F32),
        compiler_params=_params("parallel", "parallel"),
    )(c, w_cond, b_cond.reshape(DEPTH, 1, n_out))[:, :n_seq]


def _rope_tile(x, cos, sinp, sinm):
    half = MLA_ROPE // 2
    return x * cos + pltpu.roll(x, half, 1) * sinp + pltpu.roll(x, LANES - half, 1) * sinm


def _mla_proj_kernel(x_ref, sc_ref, sh_ref, win_ref, qn_ref, kvn_ref, wuq_ref, cos_ref, sinp_ref, sinm_ref,
                     q_ref, lat_ref, kr_ref):
    h = _modulate(x_ref[...], sc_ref[...], sh_ref[...]).astype(BF16)
    down = _dot(h, win_ref[...])
    cq = down[:, :MLA_Q_LORA]
    cq = cq * lax.rsqrt(jnp.mean(cq * cq, -1, keepdims=True) + RMS_EPS) * qn_ref[...]
    lat = down[:, MLA_Q_LORA:MLA_Q_LORA + MLA_KV_LORA]
    lat_ref[...] = lat * lax.rsqrt(jnp.mean(lat * lat, -1, keepdims=True) + RMS_EPS) * kvn_ref[...]
    cos, sinp, sinm = cos_ref[...], sinp_ref[...], sinm_ref[...]
    kr_ref[...] = _rope_tile(down[:, MLA_Q_LORA + MLA_KV_LORA:], cos, sinp, sinm)
    q = _dot(cq.astype(BF16), wuq_ref[...])
    for hd in range(MLA_HEADS):
        sl = slice(hd * LANES, (hd + 1) * LANES)
        q_ref[:, sl] = (_rope_tile(q[:, sl], cos, sinp, sinm) * (MLA_SCALE * LOG2E)).astype(BF16)


def _mla_proj(x, sc, sh, w, rope_tab, n_seq, seq_len):
    n = x.shape[0]
    tm = _row_tile(n_seq, seq_len, 512)
    tab_rows = rope_tab[0].shape[0]
    per = tab_rows // tm
    tab_spec = pl.BlockSpec((tm, LANES), lambda i: (i % per, 0))
    row = lambda w_: pl.BlockSpec((tm, w_), lambda i: (i, 0))
    return pl.pallas_call(
        _mla_proj_kernel,
        grid=(n // tm,),
        in_specs=[row(D_MODEL), _seq_spec(seq_len, tm), _seq_spec(seq_len, tm),
                  _full(w['w_in'].shape), _full((1, MLA_Q_LORA)), _full((1, MLA_KV_LORA)), _full(w['w_uq'].shape),
                  tab_spec, tab_spec, tab_spec],
        out_specs=[row(MLA_HEADS * LANES), row(MLA_KV_LORA), row(LANES)],
        out_shape=[jax.ShapeDtypeStruct((n, MLA_HEADS * LANES), BF16),
                   jax.ShapeDtypeStruct((n, MLA_KV_LORA), F32),
                   jax.ShapeDtypeStruct((n, LANES), F32)],
        compiler_params=_params("parallel"),
    )(x, sc, sh, w['w_in'], w['q_norm'], w['kv_norm'], w['w_uq'], *rope_tab)


def _kv_expand_kernel(lat_ref, kr_ref, wuk_ref, wuv_ref, one_ref, k_ref, v_ref):
    lat = lat_ref[...].astype(BF16)
    kn = _dot(lat, wuk_ref[...])
    kr = kr_ref[...]
    for hd in range(MLA_HEADS):
        sl = slice(hd * LANES, (hd + 1) * LANES)
        k_ref[:, sl] = (kn[:, sl] + kr).astype(BF16)
    v_ref[...] = (_dot(lat, wuv_ref[...]) + one_ref[...]).astype(BF16)


def _kv_expand(lat, kr, w):
    m = lat.shape[0]
    tm = _tile(m, 1024)
    wide = MLA_HEADS * LANES
    row = lambda w_: pl.BlockSpec((tm, w_), lambda i: (i, 0))
    return pl.pallas_call(
        _kv_expand_kernel,
        grid=(m // tm,),
        in_specs=[row(MLA_KV_LORA), row(LANES), _full((MLA_KV_LORA, wide)), _full((MLA_KV_LORA, wide)),
                  _full((1, wide))],
        out_specs=[row(wide), row(wide)],
        out_shape=[jax.ShapeDtypeStruct((m, wide), BF16)] * 2,
        compiler_params=_params("parallel"),
    )(lat, kr, w['w_uk'], w['w_uv'], w['v_one'])


def _attn_kernel(q_ref, k_ref, v_ref, o_ref, *, tq, tk, n_kv, q_off, hp):
    i = pl.program_id(2)
    qpos0 = q_off + i * tq
    n_full = jnp.minimum(((qpos0 // CHUNK) + 1) * CHUNK // tk, n_kv)
    n_end = jnp.minimum((((qpos0 + tq - 1) // CHUNK + 1) * CHUNK + tk - 1) // tk, n_kv)
    lane = lax.broadcasted_iota(I32, (tq, LANES), 1)
    heads = [slice(hh * LANES, (hh + 1) * LANES) for hh in range(hp)]
    qs = [q_ref[0, :, sl] for sl in heads]

    def step(j, carry, masked, width):
        ks = pl.multiple_of(j * width, 16)
        ss = [_dot_nt(q, k_ref[0, pl.ds(ks, width), sl]) for q, sl in zip(qs, heads)]
        if masked:
            qc = (qpos0 + lax.broadcasted_iota(I32, (tq, width), 0)) // CHUNK
            kc = (ks + lax.broadcasted_iota(I32, (tq, width), 1)) // CHUNK
            visible = kc <= qc
            ss = [jnp.where(visible, s, -1e30) for s in ss]
        ms = [jnp.maximum(m, jnp.max(s, axis=-1, keepdims=True)) for (m, _), s in zip(carry, ss)]
        ps = [jnp.exp2(s - m).astype(BF16) for s, m in zip(ss, ms)]
        pvs = [_dot(p, v_ref[0, pl.ds(ks, width), sl]) for p, sl in zip(ps, heads)]
        return tuple((m_new, acc * jnp.exp2(m - m_new) + pv) for (m, acc), m_new, pv in zip(carry, ms, pvs))

    carry = ((jnp.full((tq, 1), -1e30, F32), jnp.zeros((tq, LANES), F32)),) * hp
    n_pair = n_full // 2
    carry = lax.fori_loop(0, n_pair, functools.partial(step, masked=False, width=2 * tk), carry)
    carry = lax.fori_loop(2 * n_pair, n_full, functools.partial(step, masked=False, width=tk), carry)
    carry = lax.fori_loop(n_full, n_end, functools.partial(step, masked=True, width=tk), carry)
    outs = [acc / jnp.sum(jnp.where(lane == MLA_V, acc, 0.0), axis=-1, keepdims=True) for _, acc in carry]
    for pr in range(hp // 2):
        o = jnp.where(lane < MLA_V, outs[2 * pr], pltpu.roll(outs[2 * pr + 1], MLA_V, 1))
        o_ref[0, :, pr * LANES:(pr + 1) * LANES] = o.astype(BF16)


def _attention(q, k, v, tq, tk, hp=4):
    b, t_q, _ = q.shape
    t_k = k.shape[1]
    kern = functools.partial(_attn_kernel, tq=tq, tk=tk, n_kv=t_k // tk, q_off=t_k - t_q, hp=hp)
    wide = hp * LANES
    kv_spec = pl.BlockSpec((1, t_k, wide), lambda b_, g, i: (b_, 0, g), pipeline_mode=pl.Buffered(1))
    return pl.pallas_call(
        kern,
        grid=(b, MLA_HEADS // hp, t_q // tq),
        in_specs=[pl.BlockSpec((1, tq, wide), lambda b_, g, i: (b_, i, g)), kv_spec, kv_spec],
        out_specs=pl.BlockSpec((1, tq, hp * MLA_V), lambda b_, g, i: (b_, i, g)),
        out_shape=jax.ShapeDtypeStruct((b, t_q, MLA_HEADS * MLA_V), BF16),
        compiler_params=_params("parallel", "parallel", "arbitrary"),
    )(q, k, v)


def _latent_attn_kernel(q_ref, latn_ref, krn_ref, latp_ref, krp_ref, wuk_ref, wuv_ref, o_ref, *, group):
    latp = latp_ref[0].astype(BF16)
    latn = latn_ref[0].astype(BF16)
    krp = krp_ref[0].astype(BF16)
    krn = krn_ref[0][:, MLA_NOPE:MLA_QK].astype(BF16)
    t = latn.shape[0]
    for g0 in range(0, MLA_HEADS, group):
        hs = range(g0, g0 + group)
        qa = jnp.concatenate([_dot(q_ref[0, :, h * LANES:h * LANES + MLA_NOPE], wuk_ref[h]) for h in hs],
                             axis=0).astype(BF16)
        qr = jnp.concatenate([q_ref[0, :, h * LANES + MLA_NOPE:h * LANES + MLA_QK] for h in hs], axis=0)
        sp = _dot_nt(qa, latp) + _dot_nt(qr, krp)
        sn = _dot_nt(qa, latn) + _dot_nt(qr, krn)
        m = jnp.maximum(jnp.max(sp, axis=-1, keepdims=True), jnp.max(sn, axis=-1, keepdims=True))
        pp = jnp.exp2(sp - m)
        pn = jnp.exp2(sn - m)
        denom = jnp.sum(pp, axis=-1, keepdims=True) + jnp.sum(pn, axis=-1, keepdims=True)
        ctx = ((_dot(pp.astype(BF16), latp) + _dot(pn.astype(BF16), latn)) / denom).astype(BF16)
        outs = [_dot(ctx[i * t:(i + 1) * t], wuv_ref[h]) for i, h in enumerate(hs)]
        for i in range(0, group, 2):
            col = (g0 + i) * MLA_V
            o_ref[0, :, col:col + 2 * MLA_V] = jnp.concatenate([outs[i], outs[i + 1]], axis=1).astype(BF16)


def _latent_attention(q, lat_new, kr_new, cache_lat, cache_kr, j, w_uk_h, w_uv_h):
    b, t, _ = q.shape
    past = cache_lat.shape[2]
    return pl.pallas_call(
        functools.partial(_latent_attn_kernel, group=MLA_HEADS // 2),
        grid=(b,),
        in_specs=[pl.BlockSpec((1, t, MLA_HEADS * LANES), lambda i: (i, 0, 0)),
                  pl.BlockSpec((1, t, MLA_KV_LORA), lambda i: (i, 0, 0)),
                  pl.BlockSpec((1, t, LANES), lambda i: (i, 0, 0)),
                  pl.BlockSpec((None, 1, past, MLA_KV_LORA), lambda i: (j, i, 0, 0)),
                  pl.BlockSpec((None, 1, past, MLA_ROPE), lambda i: (j, i, 0, 0)),
                  _full(w_uk_h.shape), _full(w_uv_h.shape)],
        out_specs=pl.BlockSpec((1, t, MLA_HEADS * MLA_V), lambda i: (i, 0, 0)),
        out_shape=jax.ShapeDtypeStruct((b, t, MLA_HEADS * MLA_V), BF16),
        compiler_params=_params("parallel"),
    )(q, lat_new, kr_new, cache_lat, cache_kr, w_uk_h, w_uv_h)


def _layernorm(r, g, b):
    mu = jnp.mean(r, -1, keepdims=True)
    d = r - mu
    var = jnp.mean(d * d, -1, keepdims=True)
    return d * lax.rsqrt(var + LN_EPS) * g + b


def _mixer_out_kernel(a_ref, wo_ref, x_ref, gm_ref, lng_ref, lnb_ref, sc_ref, sh_ref, wr_ref, br_ref, up_ref, lo_ref,
                      xo_ref, h_ref, gate_ref, pos_ref, cnt_ref):
    y = _dot(a_ref[...], wo_ref[...])
    r = DEEPNORM_ALPHA * x_ref[...] + _rows_scale(y, 1.0 + gm_ref[...])
    xn = _layernorm(r, lng_ref[...], lnb_ref[...])
    xo_ref[...] = xn
    h = _modulate(xn, sc_ref[...], sh_ref[...])
    h_ref[...] = h.astype(BF16)

    logits = _dot3(wr_ref[...], h, _dot_nt) + br_ref[...]
    tm = logits.shape[1]
    eio = lax.broadcasted_iota(I32, (N_EXPERTS, tm), 0).astype(F32)
    sels, vals = [], []
    work = logits
    for k in range(TOP_K):
        m = jnp.max(work, axis=0, keepdims=True)
        ik = jnp.min(jnp.where(work == m, eio, float(N_EXPERTS)), axis=0, keepdims=True)
        sel = eio == ik
        work = jnp.where(sel, -jnp.inf, work)
        sels.append(sel)
        vals.append(m)
    es = [jnp.exp(v - vals[0]) for v in vals]
    tot = es[0] + es[1] + es[2] + es[3]
    for k in range(TOP_K):
        gate_ref[pl.ds(k, 1), :] = es[k] / tot
    multi = sels[0] | sels[1] | sels[2] | sels[3]
    mh = jnp.where(multi, 1.0, 0.0)
    before = _dot(mh.astype(BF16), up_ref[...])
    cnt = jnp.sum(mh, axis=1, keepdims=True)
    seg = jnp.floor((cnt + (SEG - 1)) * (1.0 / SEG)) * SEG
    seg_b = jnp.broadcast_to(seg, (N_EXPERTS, LANES)).astype(BF16)
    place = before + _dot(lo_ref[...], seg_b)[:, :1]
    for k in range(TOP_K):
        pos_ref[pl.ds(k, 1), :] = jnp.sum(jnp.where(sels[k], place, 0.0), axis=0, keepdims=True).astype(I32)
    cnt_ref[0] = cnt


def _mixer_out(a, w_o, x, gm, lng, lnb, sc, sh, w_r_t, b_r, n_seq, seq_len):
    n, kdim = a.shape
    tm = _row_tile(n_seq, seq_len, ROUTE_TILE)
    upper = (jnp.arange(tm)[:, None] < jnp.arange(tm)[None, :]).astype(BF16)
    lower_e = (jnp.arange(N_EXPERTS)[None, :] < jnp.arange(N_EXPERTS)[:, None]).astype(BF16)
    row = lambda w_: pl.BlockSpec((tm, w_), lambda i: (i, 0))
    col = pl.BlockSpec((TOP_K, tm), lambda i: (0, i))
    ss = _seq_spec(seq_len, tm)
    return pl.pallas_call(
        _mixer_out_kernel,
        grid=(n // tm,),
        in_specs=[row(kdim), _full(w_o.shape), row(D_MODEL), ss, _full((1, D_MODEL)), _full((1, D_MODEL)), ss, ss,
                  _full((N_EXPERTS, D_MODEL)), _full((N_EXPERTS, 1)), _full((tm, tm)),
                  _full((N_EXPERTS, N_EXPERTS))],
        out_specs=[row(D_MODEL), row(D_MODEL), col, col, pl.BlockSpec((1, N_EXPERTS, 1), lambda i: (i, 0, 0))],
        out_shape=[jax.ShapeDtypeStruct((n, D_MODEL), F32), jax.ShapeDtypeStruct((n, D_MODEL), BF16),
                   jax.ShapeDtypeStruct((TOP_K, n), F32), jax.ShapeDtypeStruct((TOP_K, n), I32),
                   jax.ShapeDtypeStruct((n // tm, N_EXPERTS, 1), F32)],
        compiler_params=_params("parallel"),
    )(a, w_o, x, gm, lng, lnb, sc, sh, w_r_t, b_r, upper, lower_e)


def _piece_sizes(tm):
    sizes, size = [], SEG
    while size <= tm:
        sizes.append(size)
        size *= 2
    assert sizes[-1] == tm
    return tuple(reversed(sizes))


def _segment_copies(seg_ref, off_ref, start_ref, tile, sizes, make):
    def body(e, carry):
        n = seg_ref[tile * N_EXPERTS + e]
        off = off_ref[tile * N_EXPERTS + e]
        start = start_ref[tile * N_EXPERTS + e]
        for size in sizes:
            done = n & (-2 * size)

            @pl.when((n & size) != 0)
            def _():
                make(pl.multiple_of(off + done, SEG), pl.multiple_of(start + done, SEG), size)
        return carry

    lax.fori_loop(0, N_EXPERTS, body, 0)


def _dispatch_kernel(seg_ref, off_ref, start_ref, h_ref, pos_ref, zero_ref, xr_ref, buf_ref, sem, *, nt, sizes):
    del zero_ref
    i = pl.program_id(0)
    slot = i % 2

    def copies(tile, sl, wait):
        def make(buf_row, hbm_row, size):
            cp = pltpu.make_async_copy(buf_ref.at[sl, pl.ds(buf_row, size)], xr_ref.at[pl.ds(hbm_row, size)],
                                       sem.at[sl])
            cp.wait() if wait else cp.start()
        _segment_copies(seg_ref, off_ref, start_ref, tile, sizes, make)

    @pl.when(i >= 2)
    def _():
        copies(i - 2, slot, True)

    pos = pos_ref[...]
    rows = buf_ref.shape[1]
    rio = lax.broadcasted_iota(I32, (rows, pos.shape[1]), 0)
    sel = (rio == pos[0:1]) | (rio == pos[1:2]) | (rio == pos[2:3]) | (rio == pos[3:4])
    buf_ref[slot] = _dot(jnp.where(sel, 1.0, 0.0).astype(BF16), h_ref[...]).astype(BF16)
    copies(i, slot, False)

    @pl.when(i == nt - 1)
    def _():
        copies(i, slot, True)

        @pl.when(i >= 1)
        def _():
            copies(i - 1, 1 - slot, True)


def _dispatch(h, pos_t, seg, off, start, x_rows, tm):
    n = h.shape[0]
    nt = n // tm
    sizes = _piece_sizes(tm)
    rows = TOP_K * tm + N_EXPERTS * SEG
    grid_spec = pltpu.PrefetchScalarGridSpec(
        num_scalar_prefetch=3,
        grid=(nt,),
        in_specs=[pl.BlockSpec((tm, D_MODEL), lambda i, *_: (i, 0)),
                  pl.BlockSpec((TOP_K, tm), lambda i, *_: (0, i)),
                  pl.BlockSpec(memory_space=pl.ANY)],
        out_specs=pl.BlockSpec(memory_space=pl.ANY),
        scratch_shapes=[pltpu.VMEM((2, rows, D_MODEL), BF16), pltpu.SemaphoreType.DMA((2,))],
    )
    return pl.pallas_call(
        functools.partial(_dispatch_kernel, nt=nt, sizes=sizes),
        grid_spec=grid_spec,
        out_shape=jax.ShapeDtypeStruct(x_rows.shape, BF16),
        input_output_aliases={5: 0},
        compiler_params=_params("arbitrary"),
    )(seg, off, start, h, pos_t, x_rows)


def _expert_kernel(be_ref, na_ref, x_ref, wgu_ref, bgu_ref, wd_ref, bd_ref, y_ref, wgu_s, wd_s):
    i = pl.program_id(0)

    @pl.when((i == 0) | (be_ref[i] != be_ref[jnp.maximum(i - 1, 0)]))
    def _():
        wgu_s[...] = wgu_ref[0].astype(BF16)
        wd_s[...] = wd_ref[0].astype(BF16)

    @pl.when(i < na_ref[0])
    def _():
        gu = _dot(x_ref[...], wgu_s[...]) + bgu_ref[0]
        gate = jnp.minimum(gu[:, :D_FF], SWIGLU_LIMIT)
        up = jnp.clip(gu[:, D_FF:], -SWIGLU_LIMIT, SWIGLU_LIMIT)
        act = (up + 1.0) * gate * _sigmoid(SWIGLU_ALPHA * gate)
        y_ref[...] = (_dot(act.astype(BF16), wd_s[...]) + bd_ref[0]).astype(BF16)

    @pl.when(i >= na_ref[0])
    def _():
        y_ref[...] = jnp.zeros_like(y_ref)


def _experts(x_rows, blk_e, n_act, w_gu, b_gu, w_down, b_down, layer):
    p = x_rows.shape[0]
    grid_spec = pltpu.PrefetchScalarGridSpec(
        num_scalar_prefetch=2,
        grid=(p // MOE_ROWS,),
        in_specs=[pl.BlockSpec((MOE_ROWS, D_MODEL), lambda i, be, na: (i, 0)),
                  pl.BlockSpec((None, 1, D_MODEL, 2 * D_FF), lambda i, be, na: (layer, be[i], 0, 0)),
                  pl.BlockSpec((1, 1, 2 * D_FF), lambda i, be, na: (be[i], 0, 0)),
                  pl.BlockSpec((None, 1, D_FF, D_MODEL), lambda i, be, na: (layer, be[i], 0, 0)),
                  pl.BlockSpec((1, 1, D_MODEL), lambda i, be, na: (be[i], 0, 0))],
        out_specs=pl.BlockSpec((MOE_ROWS, D_MODEL), lambda i, be, na: (i, 0)),
        scratch_shapes=[pltpu.VMEM((D_MODEL, 2 * D_FF), BF16), pltpu.VMEM((D_FF, D_MODEL), BF16)],
    )
    return pl.pallas_call(
        _expert_kernel,
        grid_spec=grid_spec,
        out_shape=jax.ShapeDtypeStruct((p, D_MODEL), BF16),
        compiler_params=_params("arbitrary"),
    )(blk_e, n_act, x_rows, w_gu, b_gu, w_down, b_down)


def _combine_kernel(seg_ref, off_ref, start_ref, y_ref, pos_ref, g_ref, x_ref, gf_ref, lng_ref, lnb_ref, o_ref,
                    buf_ref, sem, *, nt, sizes):
    i = pl.program_id(0)
    slot = i % 2

    def copies(tile, sl, wait):
        def make(buf_row, hbm_row, size):
            cp = pltpu.make_async_copy(y_ref.at[pl.ds(hbm_row, size)], buf_ref.at[sl, pl.ds(buf_row, size)],
                                       sem.at[sl])
            cp.wait() if wait else cp.start()
        _segment_copies(seg_ref, off_ref, start_ref, tile, sizes, make)

    @pl.when(i == 0)
    def _():
        buf_ref[...] = jnp.zeros_like(buf_ref)
        copies(0, 0, False)

    @pl.when(i + 1 < nt)
    def _():
        copies(i + 1, 1 - slot, False)

    copies(i, slot, True)
    pos = pos_ref[...]
    g = g_ref[...]
    cio = lax.broadcasted_iota(I32, (pos.shape[0], buf_ref.shape[1]), 1)
    gm = jnp.where(cio == pos[:, 0:1], g[:, 0:1], 0.0)
    for k in range(1, TOP_K):
        gm = gm + jnp.where(cio == pos[:, k:k + 1], g[:, k:k + 1], 0.0)
    y = _dot(gm.astype(BF16), buf_ref[slot])
    r = DEEPNORM_ALPHA * x_ref[...] + _rows_scale(y, 1.0 + gf_ref[...])
    o_ref[...] = _layernorm(r, lng_ref[...], lnb_ref[...])


def _combine(y_rows, pos, gates, seg, off, start, x, gf, lng, lnb, seq_len, tm):
    n = x.shape[0]
    nt = n // tm
    sizes = _piece_sizes(tm)
    rows = TOP_K * tm + N_EXPERTS * SEG
    row = pl.BlockSpec((tm, D_MODEL), lambda i, *_: (i, 0))
    four = pl.BlockSpec((tm, TOP_K), lambda i, *_: (i, 0))
    grid_spec = pltpu.PrefetchScalarGridSpec(
        num_scalar_prefetch=3,
        grid=(nt,),
        in_specs=[pl.BlockSpec(memory_space=pl.ANY), four, four, row, _seq_spec(seq_len, tm),
                  _full((1, D_MODEL)), _full((1, D_MODEL))],
        out_specs=row,
        scratch_shapes=[pltpu.VMEM((2, rows, D_MODEL), BF16), pltpu.SemaphoreType.DMA((2,))],
    )
    return pl.pallas_call(
        functools.partial(_combine_kernel, nt=nt, sizes=sizes),
        grid_spec=grid_spec,
        out_shape=jax.ShapeDtypeStruct((n, D_MODEL), F32),
        compiler_params=_params("arbitrary"),
    )(seg, off, start, y_rows, pos, gates, x, gf, lng, lnb)


def _moe_rows(groups):
    worst = N_EXPERTS * (MOE_ROWS - 1)
    for g in groups:
        n = g['n_seq'] * g['seq_len']
        worst += n * TOP_K + n // _row_tile(g['n_seq'], g['seq_len'], ROUTE_TILE) * N_EXPERTS * (SEG - 1)
    return (worst + MOE_ROWS - 1) // MOE_ROWS * MOE_ROWS


def _moe_and_norm(routed, lng, lnb, w, x_rows):
    tms = [_row_tile(r['n_seq'], r['seq_len'], ROUTE_TILE) for r in routed]
    counts = jnp.concatenate([r['cnt'][:, :, 0] for r in routed], axis=0).astype(I32)
    seg = (counts + SEG - 1) // SEG * SEG
    padded = (jnp.sum(seg, axis=0) + MOE_ROWS - 1) // MOE_ROWS * MOE_ROWS
    pad_end = jnp.cumsum(padded)
    start = (pad_end - padded)[None, :] + jnp.cumsum(seg, axis=0) - seg
    off = jnp.cumsum(seg, axis=1) - seg
    tables, t0 = [], 0
    for r in routed:
        nt = r['cnt'].shape[0]
        tables.append(tuple(a[t0:t0 + nt].reshape(-1) for a in (seg, off, start)))
        t0 += nt
    for r, tm, tab in zip(routed, tms, tables):
        x_rows = _dispatch(r['h'], r['pos_t'], *tab, x_rows, tm)
    nb = x_rows.shape[0] // MOE_ROWS
    blk_row = jnp.arange(nb, dtype=I32) * MOE_ROWS
    blk_e = jnp.minimum(jnp.sum(pad_end[None, :] <= blk_row[:, None], axis=1), N_EXPERTS - 1).astype(I32)
    n_act = (pad_end[-1:] // MOE_ROWS).astype(I32)
    y_rows = _experts(x_rows, blk_e, n_act, w['w_gu'], w['b_gu'], w['w_down'], w['b_down'], w['layer'])
    outs = [_combine(y_rows, r['pos_t'].T, r['gate_t'].T, *tab, r['x'], r['gf'], lng, lnb, r['seq_len'], tm)
            for r, tm, tab in zip(routed, tms, tables)]
    return outs, x_rows


def _mod_matmul_kernel(x_ref, sc_ref, sh_ref, w_ref, o_ref):
    h = _modulate(x_ref[...], sc_ref[...], sh_ref[...]).astype(BF16)
    o_ref[...] = _dot(h, w_ref[...]).astype(o_ref.dtype)


def _mod_matmul(x, sc, sh, w, out_dtype, n_seq, seq_len):
    n = x.shape[0]
    n_out = w.shape[1]
    tm = _row_tile(n_seq, seq_len, 1024)
    tn = _tile(n_out, 1024, LANES)
    return pl.pallas_call(
        _mod_matmul_kernel,
        grid=(n // tm, n_out // tn),
        in_specs=[pl.BlockSpec((tm, D_MODEL), lambda i, j: (i, 0)), _seq_spec(seq_len, tm), _seq_spec(seq_len, tm),
                  pl.BlockSpec((D_MODEL, tn), lambda i, j: (0, j))],
        out_specs=pl.BlockSpec((tm, tn), lambda i, j: (i, j)),
        out_shape=jax.ShapeDtypeStruct((n, n_out), out_dtype),
        compiler_params=_params("parallel", "arbitrary"),
    )(x, sc, sh, w)


def _gdn_gates_kernel(x_ref, sc_ref, sh_ref, wab_ref, wabt_ref, alog_ref, dtb_ref, alogt_ref, dtbt_ref,
                      tri_ref, trit_ref, gc_ref, beta_ref, gct_ref):
    h = _modulate(x_ref[...], sc_ref[...], sh_ref[...]).astype(BF16)
    hv = GDN_V_HEADS
    ab = _dot(h, wab_ref[...])
    beta_ref[...] = _sigmoid(ab[:, :hv])
    g = -jnp.exp(alog_ref[...]) * _softplus(ab[:, hv:] + dtb_ref[...])
    gc_ref[...] = _dot_exact_rhs_lhs(tri_ref[...], g)
    abt = _dot_nt(wabt_ref[...], h)
    gt = -jnp.exp(alogt_ref[...]) * _softplus(abt + dtbt_ref[...])
    gct_ref[...] = _dot_exact_rhs(gt, trit_ref[...])


def _dot_exact_rhs_lhs(a01, b):
    b1 = b.astype(BF16)
    r1 = b - b1.astype(F32)
    b2 = r1.astype(BF16)
    b3 = (r1 - b2.astype(F32)).astype(BF16)
    return _dot(a01, b1) + (_dot(a01, b2) + _dot(a01, b3))


def _gdn_gates(x, sc, sh, w, n_seq, seq_len, chunk):
    n = x.shape[0]
    tm = _row_tile(n_seq, seq_len, 512)
    hv = GDN_V_HEADS
    r = jnp.arange(tm)
    tri = ((r[:, None] // chunk == r[None, :] // chunk) & (r[None, :] <= r[:, None])).astype(BF16)
    row = lambda w_: pl.BlockSpec((tm, w_), lambda i: (i, 0))
    return pl.pallas_call(
        _gdn_gates_kernel,
        grid=(n // tm,),
        in_specs=[row(D_MODEL), _seq_spec(seq_len, tm), _seq_spec(seq_len, tm),
                  _full((D_MODEL, 2 * hv)), _full((hv, D_MODEL)), _full((1, hv)), _full((1, hv)),
                  _full((hv, 1)), _full((hv, 1)), _full((tm, tm)), _full((tm, tm))],
        out_specs=[row(hv), row(hv), pl.BlockSpec((hv, tm), lambda i: (0, i))],
        out_shape=[jax.ShapeDtypeStruct((n, hv), F32), jax.ShapeDtypeStruct((n, hv), F32),
                   jax.ShapeDtypeStruct((hv, n), F32)],
        compiler_params=_params("parallel"),
    )(x, sc, sh, w['w_ab'], w['w_a_t'], w['a_log'], w['dt_bias'], w['a_log'].T, w['dt_bias'].T, tri, tri.T)


def _qkv_conv_kernel(x_ref, xh_ref, sc_ref, sh_ref, w_ref, st_ref, cw_ref, o_ref, last_ref, h_s, *, per, n_q, n_qk):
    i = pl.program_id(0)
    j = pl.program_id(1)
    sc, sh = sc_ref[...], sh_ref[...]

    @pl.when(j == 0)
    def _():
        h_s[...] = _modulate(x_ref[...], sc, sh).astype(BF16)

    w = w_ref[...]
    x = _dot(h_s[...], w)
    tm = x.shape[0]
    prev = _dot(_modulate(xh_ref[...], sc, sh).astype(BF16), w)[HALO:]
    halo = jnp.where(i % per == 0, st_ref[0], prev)
    last_ref[0] = x[tm - HALO:]
    cw = cw_ref[...]
    rows = lax.broadcasted_iota(I32, (HALO, x.shape[1]), 0)
    acc = x * cw[GDN_CONV_W - 1:GDN_CONV_W]
    for s in range(1, GDN_CONV_W):
        xs = pltpu.roll(x, s, 0)
        head = jnp.where(rows < s, pltpu.roll(halo, s, 0), xs[:HALO])
        xs = jnp.concatenate([head, xs[HALO:]], axis=0) if tm > HALO else head
        acc = acc + xs * cw[GDN_CONV_W - 1 - s:GDN_CONV_W - s]
    y = _silu(acc)

    @pl.when(j < n_qk)
    def _():
        scale = jnp.where(j < n_q, GDN_DK ** -0.5, 1.0)
        for hd in range(y.shape[1] // GDN_DK):
            sl = slice(hd * GDN_DK, (hd + 1) * GDN_DK)
            yh = y[:, sl]
            o_ref[:, sl] = yh * (lax.rsqrt(jnp.sum(yh * yh, -1, keepdims=True) + L2_EPS) * scale)

    @pl.when(j >= n_qk)
    def _():
        o_ref[...] = y


def _gdn_qkv_conv(x, sc, sh, w_qkv, state8, conv_w, n_seq, seq_len):
    n = x.shape[0]
    assert seq_len >= 2 * HALO
    tm = _tile(seq_len, 1024)
    per = seq_len // tm
    tn = 512 if tm >= 512 else GDN_QK_DIM
    width = w_qkv.shape[1]
    kern = functools.partial(_qkv_conv_kernel, per=per, n_q=GDN_QK_DIM // tn, n_qk=2 * GDN_QK_DIM // tn)
    hb = tm // (2 * HALO)
    act, last = pl.pallas_call(
        kern,
        grid=(n // tm, width // tn),
        in_specs=[pl.BlockSpec((tm, D_MODEL), lambda i, j: (i, 0)),
                  pl.BlockSpec((2 * HALO, D_MODEL), lambda i, j: (jnp.maximum(i * hb - 1, 0), 0)),
                  _seq_spec(seq_len, tm), _seq_spec(seq_len, tm),
                  pl.BlockSpec((D_MODEL, tn), lambda i, j: (0, j)),
                  pl.BlockSpec((1, HALO, tn), lambda i, j: (i // per, 0, j)),
                  pl.BlockSpec((GDN_CONV_W, tn), lambda i, j: (0, j))],
        out_specs=[pl.BlockSpec((tm, tn), lambda i, j: (i, j)),
                   pl.BlockSpec((1, HALO, tn), lambda i, j: (i, 0, j))],
        out_shape=[jax.ShapeDtypeStruct((n, width), F32), jax.ShapeDtypeStruct((n // tm, HALO, width), F32)],
        scratch_shapes=[pltpu.VMEM((tm, D_MODEL), BF16)],
        compiler_params=_params("parallel", "arbitrary"),
    )(x, x, sc, sh, w_qkv, state8, conv_w)
    return act, last.reshape(n_seq, per, HALO, width)[:, -1]


def _inv_masks(c):
    row = lax.broadcasted_iota(I32, (c, c), 0)
    col = lax.broadcasted_iota(I32, (c, c), 1)
    eye = jnp.where(row == col, 1.0, 0.0)
    diag = row // INV_BASE == col // INV_BASE
    offs = []
    size = INV_BASE
    while size < c:
        offs.append((row // (2 * size) == col // (2 * size)) & (row // size % 2 == 1) & (col // size % 2 == 0))
        size *= 2
    return eye, diag, offs


def _inv_unit_lower(lmats, masks):
    eye, diag, offs = masks
    pws = [jnp.where(diag, lm, 0.0) for lm in lmats]
    ts = [eye - pw for pw in pws]
    size = 2
    while size < INV_BASE:
        pws = [_dotb(pw, pw) for pw in pws]
        ts = [t + _dotb(t, pw) for t, pw in zip(ts, pws)]
        size *= 2
    for off in offs:
        tbs = [_dotb(t, jnp.where(off, lm, 0.0)) for t, lm in zip(ts, lmats)]
        ts = [t - _dotb(tb, t) for t, tb in zip(ts, tbs)]
    return ts


def _gdn_core_kernel(q_ref, k_ref, v_ref, z_ref, gc_ref, beta_ref, gct_ref, s0_ref, nw_ref, o_ref, s_ref, *, c, hb):
    @pl.when(pl.program_id(2) == 0)
    def _():
        s_ref[...] = s0_ref[...]

    row = lax.broadcasted_iota(I32, (c, c), 0)
    col = lax.broadcasted_iota(I32, (c, c), 1)
    lower = col <= row
    strict = col < row
    masks = _inv_masks(c)
    heads = range(hb)
    ksl = [slice(kh * GDN_DK, (kh + 1) * GDN_DK) for kh in range(hb // 2)]
    vsl = [slice(h * GDN_DV, (h + 1) * GDN_DV) for h in heads]
    qn = [q_ref[0, :, sl] for sl in ksl]
    kn = [k_ref[0, :, sl] for sl in ksl]
    kb = [k.astype(BF16) for k in kn]
    kk = [_dot_nt(k, k) for k in kb]
    qk = [_dot_nt(q.astype(BF16), k) for q, k in zip(qn, kb)]
    gcc = [gc_ref[0, 0, :, h:h + 1] for h in heads]
    bc = [beta_ref[0, 0, :, h:h + 1] for h in heads]
    decay = [jnp.where(lower, jnp.exp(jnp.minimum(gcc[h] - gct_ref[0, 0, h:h + 1, :], 0.0)), 0.0) for h in heads]
    lmat = [jnp.where(strict, kk[h // 2] * decay[h], 0.0) * bc[h] for h in heads]
    ts = _inv_unit_lower(lmat, masks)
    egc = [jnp.exp(g) for g in gcc]
    rhs = [jnp.concatenate([v_ref[0, :, vsl[h]] * bc[h], kn[h // 2] * (bc[h] * egc[h])], axis=1).astype(BF16)
           for h in heads]
    uw = [_dot(ts[h].astype(BF16), rhs[h]) for h in heads]
    s_old = [s_ref[0, h] for h in heads]
    sb = [s.astype(BF16) for s in s_old]
    ws = [_dot(uw[h][:, GDN_DV:].astype(BF16), sb[h]) for h in heads]
    qs = [_dot((qn[h // 2] * egc[h]).astype(BF16), sb[h]) for h in heads]
    vb = [(uw[h][:, :GDN_DV] - ws[h]).astype(BF16) for h in heads]
    o = [qs[h] + _dot((qk[h // 2] * decay[h]).astype(BF16), vb[h]) for h in heads]
    glast = [g[c - 1:c, :] for g in gcc]
    kd = [(kn[h // 2] * jnp.exp(glast[h] - gcc[h])).astype(BF16) for h in heads]
    s_new = [s_old[h] * jnp.exp(glast[h]) + _dot_tn(kd[h], vb[h]) for h in heads]
    for h in heads:
        s_ref[0, h] = s_new[h]
        z = z_ref[0, :, vsl[h]].astype(F32)
        on = o[h] * lax.rsqrt(jnp.mean(o[h] * o[h], -1, keepdims=True) + RMS_EPS) * nw_ref[...]
        o_ref[0, :, vsl[h]] = (on * _silu(z)).astype(BF16)


def _gdn_core(qkv, z, gc, beta, gct, s0, norm_w, c, hb=GDN_V_HEADS):
    b, t, _ = z.shape
    ng = GDN_V_HEADS // hb
    kw = hb // 2 * GDN_DK
    vw = hb * GDN_DV
    gc_g = gc.reshape(b, t, ng, hb).transpose(0, 2, 1, 3)
    beta_g = beta.reshape(b, t, ng, hb).transpose(0, 2, 1, 3)
    gct_g = gct.reshape(b, ng, hb, t)
    kern = functools.partial(_gdn_core_kernel, c=c, hb=hb)
    nkb = GDN_QK_DIM // kw
    nvb = 2 * GDN_QK_DIM // vw
    return pl.pallas_call(
        kern,
        grid=(b, ng, t // c),
        in_specs=[pl.BlockSpec((1, c, kw), lambda b_, g, i: (b_, i, g)),
                  pl.BlockSpec((1, c, kw), lambda b_, g, i: (b_, i, nkb + g)),
                  pl.BlockSpec((1, c, vw), lambda b_, g, i: (b_, i, nvb + g)),
                  pl.BlockSpec((1, c, vw), lambda b_, g, i: (b_, i, g)),
                  pl.BlockSpec((1, 1, c, hb), lambda b_, g, i: (b_, g, i, 0)),
                  pl.BlockSpec((1, 1, c, hb), lambda b_, g, i: (b_, g, i, 0)),
                  pl.BlockSpec((1, 1, hb, c), lambda b_, g, i: (b_, g, 0, i)),
                  pl.BlockSpec((1, hb, GDN_DK, GDN_DV), lambda b_, g, i: (b_, g, 0, 0)),
                  _full((1, GDN_DV))],
        out_specs=[pl.BlockSpec((1, c, vw), lambda b_, g, i: (b_, i, g)),
                   pl.BlockSpec((1, hb, GDN_DK, GDN_DV), lambda b_, g, i: (b_, g, 0, 0))],
        out_shape=[jax.ShapeDtypeStruct((b, t, GDN_V_DIM), BF16),
                   jax.ShapeDtypeStruct((b, GDN_V_HEADS, GDN_DK, GDN_DV), F32)],
        compiler_params=_params("parallel", "parallel", "arbitrary"),
    )(qkv, qkv, qkv, z, gc_g, beta_g, gct_g, s0, norm_w)


def _prep_weights(p):
    f = {}
    pad_heads = lambda w_, dh: jnp.pad(w_.reshape(w_.shape[0], w_.shape[1], MLA_HEADS, dh),
                                       ((0, 0), (0, 0), (0, 0), (0, LANES - dh))).reshape(
                                           w_.shape[0], w_.shape[1], MLA_HEADS * LANES)
    w_in = p['mla_w_in']
    nl = w_in.shape[0]
    lo = MLA_Q_LORA + MLA_KV_LORA
    z = lambda k: jnp.zeros((nl, D_MODEL, k), F32)
    f['mla_w_in'] = jnp.concatenate([w_in[..., :lo], z(MLA_NOPE), w_in[..., lo:], z(LANES - MLA_QK)], -1).astype(BF16)
    f['mla_w_uq'] = pad_heads(p['mla_w_uq'], MLA_QK).astype(BF16)
    f['mla_w_uk'] = pad_heads(p['mla_w_uk'], MLA_NOPE).astype(BF16)
    f['mla_w_uv'] = pad_heads(p['mla_w_uv'], MLA_V).astype(BF16)
    f['mla_w_o'] = p['mla_w_o'].astype(BF16)
    per_head = lambda w_, dh: w_.reshape(w_.shape[0], w_.shape[1], MLA_HEADS, dh)
    f['mla_w_uk_t'] = per_head(p['mla_w_uk'], MLA_NOPE).transpose(0, 2, 3, 1).astype(BF16)
    f['mla_w_uv_h'] = per_head(p['mla_w_uv'], MLA_V).transpose(0, 2, 1, 3).astype(BF16)
    f['v_one'] = jnp.tile((jnp.arange(LANES) == MLA_V).astype(F32), MLA_HEADS)[None]
    g_in = p['gdn_w_in']
    f['gdn_w_qkv'] = g_in[..., :GDN_CONV_DIM].astype(BF16)
    f['gdn_w_z'] = g_in[..., GDN_CONV_DIM:GDN_CONV_DIM + GDN_V_DIM].astype(BF16)
    f['gdn_w_ab'] = g_in[..., GDN_CONV_DIM + GDN_V_DIM:].astype(BF16)
    f['gdn_w_a_t'] = jnp.swapaxes(g_in[..., GDN_CONV_DIM + GDN_V_DIM + GDN_V_HEADS:], 1, 2).astype(BF16)
    f['gdn_w_o'] = p['gdn_w_o'].astype(BF16)
    f['moe_w_r_t'] = jnp.swapaxes(p['moe_w_router'], 1, 2)
    return f


def _rope_tables(pos):
    half = MLA_ROPE // 2
    inv_freq = ROPE_BASE ** (-jnp.arange(half, dtype=F32) / half)
    ang = pos.astype(F32)[:, None] * inv_freq[None, :]
    cos, sin = jnp.cos(ang), jnp.sin(ang)
    t = pos.shape[0]
    z = lambda k: jnp.zeros((t, k), F32)
    cos_t = jnp.concatenate([jnp.ones((t, MLA_NOPE), F32), cos, cos, z(LANES - MLA_QK)], -1)
    sinp_t = jnp.concatenate([z(MLA_NOPE + half), sin, z(LANES - MLA_QK)], -1)
    sinm_t = jnp.concatenate([z(MLA_NOPE), -sin, z(half + LANES - MLA_QK)], -1)
    return cos_t, sinp_t, sinm_t


def _mixer(g, layer, mods, p, f):
    x, bsz, t, past = g['x'], g['n_seq'], g['seq_len'], g['past']
    sh_m, sc_m = mods[0], mods[1]
    n = bsz * t
    j = layer // 2
    if layer % 2 == 0:
        w = dict(w_in=f['mla_w_in'][j], q_norm=p['mla_q_norm'][j][None], kv_norm=p['mla_kv_norm'][j][None],
                 w_uq=f['mla_w_uq'][j], w_uk=f['mla_w_uk'][j], w_uv=f['mla_w_uv'][j], v_one=f['v_one'])
        q, lat, krt = _mla_proj(x, sc_m, sh_m, w, g['tabs'], bsz, t)
        g['lats'].append(lat.reshape(bsz, t, MLA_KV_LORA))
        g['krs'].append(krt[:, MLA_NOPE:MLA_QK].reshape(bsz, t, MLA_ROPE))
        wide = MLA_HEADS * LANES
        if past is None:
            k, v = _kv_expand(lat, krt, w)
            tq = _tile(t, 512)
            ctx = _attention(q.reshape(bsz, t, wide), k.reshape(bsz, t, wide), v.reshape(bsz, t, wide), tq, tq)
        else:
            ctx = _latent_attention(q.reshape(bsz, t, wide), lat.reshape(bsz, t, -1), krt.reshape(bsz, t, -1),
                                    past[0], past[1], j, f['mla_w_uk_t'][j], f['mla_w_uv_h'][j])
        return ctx.reshape(n, MLA_HEADS * MLA_V), f['mla_w_o'][j]
    chunk = _tile(t, 128)
    w = dict(w_ab=f['gdn_w_ab'][j], w_a_t=f['gdn_w_a_t'][j], a_log=p['gdn_a_log'][j][None],
             dt_bias=p['gdn_dt_bias'][j][None])
    z = _mod_matmul(x, sc_m, sh_m, f['gdn_w_z'][j], BF16, bsz, t)
    gc, beta, gct = _gdn_gates(x, sc_m, sh_m, w, bsz, t, chunk)
    if past is None:
        conv_state = jnp.zeros((bsz, GDN_CONV_W - 1, GDN_CONV_DIM), F32)
        s0 = jnp.zeros((bsz, GDN_V_HEADS, GDN_DK, GDN_DV), F32)
    else:
        conv_state, s0 = past[2][j], past[3][j]
    state8 = jnp.pad(conv_state, ((0, 0), (HALO - (GDN_CONV_W - 1), 0), (0, 0)))
    qkv, last = _gdn_qkv_conv(x, sc_m, sh_m, f['gdn_w_qkv'][j], state8, p['gdn_conv_w'][j], bsz, t)
    g['convs'].append(last[:, HALO - (GDN_CONV_W - 1):])
    o, s_new = _gdn_core(qkv.reshape(bsz, t, -1), z.reshape(bsz, t, -1), gc.reshape(bsz, t, -1),
                         beta.reshape(bsz, t, -1), gct.reshape(GDN_V_HEADS, bsz, t).transpose(1, 0, 2), s0,
                         p['gdn_norm'][j][None], chunk)
    g['ssms'].append(s_new)
    return o.reshape(n, GDN_V_DIM), f['gdn_w_o'][j]


def _forward(groups, p, f):
    for g in groups:
        bsz, t, _ = g['x'].shape
        g.update(n_seq=bsz, seq_len=t, x=g['x'].reshape(bsz * t, D_MODEL), lats=[], krs=[], convs=[], ssms=[],
                 mod=_cond(g['c'], p['w_cond'], p['b_cond']).reshape(DEPTH, bsz, 6, 1, D_MODEL))
        tabs = _rope_tables(g['pos'])
        tm_rope = _row_tile(bsz, t, 512)
        g['tabs'] = tuple(jnp.tile(tb, (tm_rope // t, 1)) for tb in tabs) if tm_rope > t else tabs
    x_rows = jnp.zeros((_moe_rows(groups), D_MODEL), BF16)
    for layer in range(DEPTH):
        routed = []
        for g in groups:
            sh_m, sc_m, g_m, sh_f, sc_f, g_f = [g['mod'][layer, :, i] for i in range(6)]
            a, w_o = _mixer(g, layer, (sh_m, sc_m), p, f)
            x, h, gate_t, pos_t, cnt = _mixer_out(
                a, w_o, g['x'], g_m, p['ln1_g'][layer][None], p['ln1_b'][layer][None], sc_f, sh_f,
                f['moe_w_r_t'][layer], p['moe_b_router'][layer][:, None], g['n_seq'], g['seq_len'])
            routed.append(dict(x=x, h=h, gate_t=gate_t, pos_t=pos_t, cnt=cnt, gf=g_f, n_seq=g['n_seq'],
                               seq_len=g['seq_len']))
        wm = dict(w_gu=p['moe_w_gu'], b_gu=p['moe_b_gu'][layer][:, None], w_down=p['moe_w_down'],
                  b_down=p['moe_b_down'][layer][:, None], layer=layer)
        xs, x_rows = _moe_and_norm(routed, p['ln2_g'][layer][None], p['ln2_b'][layer][None], wm, x_rows)
        for g, x in zip(groups, xs):
            g['x'] = x
    return [(g['x'].reshape(g['n_seq'], g['seq_len'], D_MODEL), jnp.stack(g['lats']), jnp.stack(g['krs']),
             jnp.stack(g['convs']), jnp.stack(g['ssms'])) for g in groups]


def kernel(x_prompt, x_sample, c_prompt, c_sample, cache_mla_latent, cache_mla_krope, state_gdn_conv, state_gdn_ssm, w_cond, b_cond, ln1_g, ln1_b, ln2_g, ln2_b, mla_w_in, mla_q_norm, mla_kv_norm, mla_w_uq, mla_w_uk, mla_w_uv, mla_w_o, gdn_w_in, gdn_conv_w, gdn_a_log, gdn_dt_bias, gdn_norm, gdn_w_o, moe_w_router, moe_b_router, moe_w_gu, moe_b_gu, moe_w_down, moe_b_down):
    p = dict(w_cond=w_cond, b_cond=b_cond, ln1_g=ln1_g, ln1_b=ln1_b, ln2_g=ln2_g, ln2_b=ln2_b,
             mla_w_in=mla_w_in, mla_q_norm=mla_q_norm, mla_kv_norm=mla_kv_norm, mla_w_uq=mla_w_uq,
             mla_w_uk=mla_w_uk, mla_w_uv=mla_w_uv, mla_w_o=mla_w_o, gdn_w_in=gdn_w_in,
             gdn_conv_w=gdn_conv_w, gdn_a_log=gdn_a_log, gdn_dt_bias=gdn_dt_bias, gdn_norm=gdn_norm,
             gdn_w_o=gdn_w_o, moe_w_router=moe_w_router, moe_b_router=moe_b_router, moe_w_gu=moe_w_gu,
             moe_b_gu=moe_b_gu, moe_w_down=moe_w_down, moe_b_down=moe_b_down)
    f = _prep_weights(p)
    past_len = cache_mla_latent.shape[2]
    assert past_len % CHUNK == 0 and x_sample.shape[1] <= CHUNK
    past = (cache_mla_latent, cache_mla_krope, state_gdn_conv, state_gdn_ssm)
    prompt = dict(x=x_prompt, c=c_prompt, pos=jnp.arange(x_prompt.shape[1], dtype=I32), past=None)
    sample = dict(x=x_sample, c=c_sample, pos=past_len + jnp.arange(x_sample.shape[1], dtype=I32), past=past)
    (y_p, p_lat, p_kr, p_conv, p_ssm), (y_s, s_lat, s_kr, s_conv, s_ssm) = _forward([prompt, sample], p, f)
    return (y_p, y_s, p_lat, p_kr, p_conv, p_ssm, s_lat, s_kr, s_conv, s_ssm)
```

```python
import functools
import math

import jax
import jax.numpy as jnp
from jax import lax
from jax.experimental import pallas as pl
from jax.experimental.pallas import tpu as pltpu

F32 = jnp.float32
BF16 = jnp.bfloat16
I32 = jnp.int32

D_MODEL = 1024
DEPTH = 4
CHUNK = 64
MLA_HEADS = 16
MLA_Q_LORA = 768
MLA_KV_LORA = 256
MLA_NOPE = 64
MLA_ROPE = 32
MLA_V = 64
MLA_QK = MLA_NOPE + MLA_ROPE
MLA_SCALE = MLA_QK ** -0.5
ROPE_BASE = 10000.0
GDN_QK_HEADS = 8
GDN_V_HEADS = 16
GDN_DK = 128
GDN_DV = 128
GDN_QK_DIM = GDN_QK_HEADS * GDN_DK
GDN_V_DIM = GDN_V_HEADS * GDN_DV
GDN_CONV_DIM = 2 * GDN_QK_DIM + GDN_V_DIM
GDN_CONV_W = 4
N_EXPERTS = 32
TOP_K = 4
D_FF = D_MODEL
SWIGLU_ALPHA = 1.702
SWIGLU_LIMIT = 7.0
DEEPNORM_ALPHA = (2 * DEPTH) ** 0.25
LN_EPS = 1e-5
RMS_EPS = 1e-6
L2_EPS = 1e-6

LANES = 128
HALO = 8
LOG2E = 1.4426950408889634
MOE_ROWS = 512
ROUTE_TILE = 512
SEG = 16
PIECE_MAX = 128
INV_BASE = 16
VMEM_LIMIT = 48 * 1024 * 1024


def _params(*sem):
    return pltpu.CompilerParams(dimension_semantics=sem, vmem_limit_bytes=VMEM_LIMIT)


def _tile(n, pref, mult=8):
    t = min(pref, n)
    while t >= mult:
        if n % t == 0 and t % mult == 0:
            return t
        t -= 1
    return n


def _dot(a, b):
    return jnp.dot(a, b, preferred_element_type=F32)


def _dot_nt(a, b):
    return lax.dot_general(a, b, (((1,), (1,)), ((), ())), preferred_element_type=F32)


def _dot_tn(a, b):
    return lax.dot_general(a, b, (((0,), (0,)), ((), ())), preferred_element_type=F32)


def _dotb(a, b):
    return _dot(a.astype(BF16), b.astype(BF16))


def _split(a):
    hi = a.astype(BF16)
    lo = (a - hi.astype(F32)).astype(BF16)
    return hi, lo


def _dot3(a, b, dot=_dot):
    ah, al = _split(a)
    bh, bl = _split(b)
    return dot(ah, bh) + (dot(ah, bl) + dot(al, bh))


def _dot_exact_rhs(a, b01, dot=_dot):
    a1 = a.astype(BF16)
    r1 = a - a1.astype(F32)
    a2 = r1.astype(BF16)
    a3 = (r1 - a2.astype(F32)).astype(BF16)
    return dot(a1, b01) + (dot(a2, b01) + dot(a3, b01))


def _sigmoid(x):
    return 1.0 / (1.0 + jnp.exp(-x))


def _silu(x):
    return x * _sigmoid(x)


def _softplus(x):
    return jnp.maximum(x, 0.0) + jnp.log(1.0 + jnp.exp(-jnp.abs(x)))


def _rows_scale(x, s):
    spt = s.shape[0]
    if spt == 1:
        return x * s[0]
    tm, d = x.shape
    return (x.reshape(spt, tm // spt, d) * s).reshape(tm, d)


def _rows_add(x, s):
    spt = s.shape[0]
    if spt == 1:
        return x + s[0]
    tm, d = x.shape
    return (x.reshape(spt, tm // spt, d) + s).reshape(tm, d)


def _modulate(x, sc, sh):
    return _rows_add(_rows_scale(x, 1.0 + sc), sh)


def _seq_spec(seq_len, tm, d=D_MODEL):
    if tm <= seq_len:
        per = seq_len // tm
        return pl.BlockSpec((1, 1, d), lambda i, *_: (i // per, 0, 0))
    return pl.BlockSpec((tm // seq_len, 1, d), lambda i, *_: (i, 0, 0))


def _row_tile(n_seq, seq_len, pref):
    if seq_len >= pref:
        return _tile(seq_len, pref)
    spt = _tile(n_seq, max(pref // seq_len, 1), mult=1)
    return spt * seq_len


def _full(shape):
    nd = len(shape)
    return pl.BlockSpec(shape, lambda *_: (0,) * nd)


def _cond_kernel(c_ref, w_ref, b_ref, o_ref):
    c = c_ref[...]
    o_ref[0] = _dot3(_silu(c), w_ref[0]) + b_ref[0]


def _cond(c, w_cond, b_cond):
    n_seq = c.shape[0]
    c = jnp.pad(c, ((0, -n_seq % HALO), (0, 0)))
    s = c.shape[0]
    n_out = w_cond.shape[-1]
    tn = _tile(n_out, 1536, LANES)
    return pl.pallas_call(
        _cond_kernel,
        grid=(DEPTH, n_out // tn),
        in_specs=[pl.BlockSpec((s, D_MODEL), lambda l, j: (0, 0)),
                  pl.BlockSpec((1, D_MODEL, tn), lambda l, j: (l, 0, j)),
                  pl.BlockSpec((1, 1, tn), lambda l, j: (l, 0, j))],
        out_specs=pl.BlockSpec((1, s, tn), lambda l, j: (l, 0, j)),
        out_shape=jax.ShapeDtypeStruct((DEPTH, s, n_out), F32),
        compiler_params=_params("parallel", "parallel"),
    )(c, w_cond, b_cond.reshape(DEPTH, 1, n_out))[:, :n_seq]


def _rope_tile(x, cos, sinp, sinm):
    half = MLA_ROPE // 2
    return x * cos + pltpu.roll(x, half, 1) * sinp + pltpu.roll(x, LANES - half, 1) * sinm


def _mla_proj_kernel(x_ref, sc_ref, sh_ref, win_ref, qn_ref, kvn_ref, wuq_ref, cos_ref, sinp_ref, sinm_ref,
                     q_ref, lat_ref, kr_ref):
    h = _modulate(x_ref[...], sc_ref[...], sh_ref[...]).astype(BF16)
    down = _dot(h, win_ref[...])
    cq = down[:, :MLA_Q_LORA]
    cq = cq * lax.rsqrt(jnp.mean(cq * cq, -1, keepdims=True) + RMS_EPS) * qn_ref[...]
    lat = down[:, MLA_Q_LORA:MLA_Q_LORA + MLA_KV_LORA]
    lat_ref[...] = lat * lax.rsqrt(jnp.mean(lat * lat, -1, keepdims=True) + RMS_EPS) * kvn_ref[...]
    cos, sinp, sinm = cos_ref[...], sinp_ref[...], sinm_ref[...]
    kr_ref[...] = _rope_tile(down[:, MLA_Q_LORA + MLA_KV_LORA:], cos, sinp, sinm)
    q = _dot(cq.astype(BF16), wuq_ref[...])
    for hd in range(MLA_HEADS):
        sl = slice(hd * LANES, (hd + 1) * LANES)
        q_ref[:, sl] = (_rope_tile(q[:, sl], cos, sinp, sinm) * (MLA_SCALE * LOG2E)).astype(BF16)


def _mla_proj(x, sc, sh, w, rope_tab, n_seq, seq_len):
    n = x.shape[0]
    tm = _row_tile(n_seq, seq_len, 512)
    tab_rows = rope_tab[0].shape[0]
    per = tab_rows // tm
    tab_spec = pl.BlockSpec((tm, LANES), lambda i: (i % per, 0))
    row = lambda w_: pl.BlockSpec((tm, w_), lambda i: (i, 0))
    return pl.pallas_call(
        _mla_proj_kernel,
        grid=(n // tm,),
        in_specs=[row(D_MODEL), _seq_spec(seq_len, tm), _seq_spec(seq_len, tm),
                  _full(w['w_in'].shape), _full((1, MLA_Q_LORA)), _full((1, MLA_KV_LORA)), _full(w['w_uq'].shape),
                  tab_spec, tab_spec, tab_spec],
        out_specs=[row(MLA_HEADS * LANES), row(MLA_KV_LORA), row(LANES)],
        out_shape=[jax.ShapeDtypeStruct((n, MLA_HEADS * LANES), BF16),
                   jax.ShapeDtypeStruct((n, MLA_KV_LORA), F32),
                   jax.ShapeDtypeStruct((n, LANES), F32)],
        compiler_params=_params("parallel"),
    )(x, sc, sh, w['w_in'], w['q_norm'], w['kv_norm'], w['w_uq'], *rope_tab)


def _kv_expand_kernel(lat_ref, kr_ref, wuk_ref, wuv_ref, one_ref, k_ref, v_ref):
    lat = lat_ref[...].astype(BF16)
    kn = _dot(lat, wuk_ref[...])
    kr = kr_ref[...]
    for hd in range(MLA_HEADS):
        sl = slice(hd * LANES, (hd + 1) * LANES)
        k_ref[:, sl] = (kn[:, sl] + kr).astype(BF16)
    v_ref[...] = (_dot(lat, wuv_ref[...]) + one_ref[...]).astype(BF16)


def _kv_expand(lat, kr, w):
    m = lat.shape[0]
    tm = _tile(m, 1024)
    wide = MLA_HEADS * LANES
    row = lambda w_: pl.BlockSpec((tm, w_), lambda i: (i, 0))
    return pl.pallas_call(
        _kv_expand_kernel,
        grid=(m // tm,),
        in_specs=[row(MLA_KV_LORA), row(LANES), _full((MLA_KV_LORA, wide)), _full((MLA_KV_LORA, wide)),
                  _full((1, wide))],
        out_specs=[row(wide), row(wide)],
        out_shape=[jax.ShapeDtypeStruct((m, wide), BF16)] * 2,
        compiler_params=_params("parallel"),
    )(lat, kr, w['w_uk'], w['w_uv'], w['v_one'])


def _attn_kernel(q_ref, k_ref, v_ref, o_ref, *, tq, tk, n_kv, q_off, hp):
    i = pl.program_id(2)
    qpos0 = q_off + i * tq
    n_full = jnp.minimum(((qpos0 // CHUNK) + 1) * CHUNK // tk, n_kv)
    n_end = jnp.minimum((((qpos0 + tq - 1) // CHUNK + 1) * CHUNK + tk - 1) // tk, n_kv)
    lane = lax.broadcasted_iota(I32, (tq, LANES), 1)
    heads = [slice(hh * LANES, (hh + 1) * LANES) for hh in range(hp)]
    qs = [q_ref[0, :, sl] for sl in heads]

    def step(j, carry, masked, width):
        ks = pl.multiple_of(j * width, 16)
        ss = [_dot_nt(q, k_ref[0, pl.ds(ks, width), sl]) for q, sl in zip(qs, heads)]
        if masked:
            qc = (qpos0 + lax.broadcasted_iota(I32, (tq, width), 0)) // CHUNK
            kc = (ks + lax.broadcasted_iota(I32, (tq, width), 1)) // CHUNK
            visible = kc <= qc
            ss = [jnp.where(visible, s, -1e30) for s in ss]
        ms = [jnp.maximum(m, jnp.max(s, axis=-1, keepdims=True)) for (m, _), s in zip(carry, ss)]
        ps = [jnp.exp2(s - m).astype(BF16) for s, m in zip(ss, ms)]
        pvs = [_dot(p, v_ref[0, pl.ds(ks, width), sl]) for p, sl in zip(ps, heads)]
        return tuple((m_new, acc * jnp.exp2(m - m_new) + pv) for (m, acc), m_new, pv in zip(carry, ms, pvs))

    carry = ((jnp.full((tq, 1), -1e30, F32), jnp.zeros((tq, LANES), F32)),) * hp
    n_pair = n_full // 2
    carry = lax.fori_loop(0, n_pair, functools.partial(step, masked=False, width=2 * tk), carry)
    carry = lax.fori_loop(2 * n_pair, n_full, functools.partial(step, masked=False, width=tk), carry)
    carry = lax.fori_loop(n_full, n_end, functools.partial(step, masked=True, width=tk), carry)
    outs = [acc / jnp.sum(jnp.where(lane == MLA_V, acc, 0.0), axis=-1, keepdims=True) for _, acc in carry]
    for pr in range(hp // 2):
        o = jnp.where(lane < MLA_V, outs[2 * pr], pltpu.roll(outs[2 * pr + 1], MLA_V, 1))
        o_ref[0, :, pr * LANES:(pr + 1) * LANES] = o.astype(BF16)


def _attention(q, k, v, tq, tk, hp=4):
    b, t_q, _ = q.shape
    t_k = k.shape[1]
    kern = functools.partial(_attn_kernel, tq=tq, tk=tk, n_kv=t_k // tk, q_off=t_k - t_q, hp=hp)
    wide = hp * LANES
    kv_spec = pl.BlockSpec((1, t_k, wide), lambda b_, g, i: (b_, 0, g), pipeline_mode=pl.Buffered(1))
    return pl.pallas_call(
        kern,
        grid=(b, MLA_HEADS // hp, t_q // tq),
        in_specs=[pl.BlockSpec((1, tq, wide), lambda b_, g, i: (b_, i, g)), kv_spec, kv_spec],
        out_specs=pl.BlockSpec((1, tq, hp * MLA_V), lambda b_, g, i: (b_, i, g)),
        out_shape=jax.ShapeDtypeStruct((b, t_q, MLA_HEADS * MLA_V), BF16),
        compiler_params=_params("parallel", "parallel", "arbitrary"),
    )(q, k, v)


def _latent_attn_kernel(q_ref, latn_ref, krn_ref, latp_ref, krp_ref, wuk_ref, wuv_ref, o_ref, *, group):
    latp = latp_ref[0].astype(BF16)
    latn = latn_ref[0].astype(BF16)
    krp = krp_ref[0].astype(BF16)
    krn = krn_ref[0][:, MLA_NOPE:MLA_QK].astype(BF16)
    t = latn.shape[0]
    for g0 in range(0, MLA_HEADS, group):
        hs = range(g0, g0 + group)
        qa = jnp.concatenate([_dot(q_ref[0, :, h * LANES:h * LANES + MLA_NOPE], wuk_ref[h]) for h in hs],
                             axis=0).astype(BF16)
        qr = jnp.concatenate([q_ref[0, :, h * LANES + MLA_NOPE:h * LANES + MLA_QK] for h in hs], axis=0)
        sp = _dot_nt(qa, latp) + _dot_nt(qr, krp)
        sn = _dot_nt(qa, latn) + _dot_nt(qr, krn)
        m = jnp.maximum(jnp.max(sp, axis=-1, keepdims=True), jnp.max(sn, axis=-1, keepdims=True))
        pp = jnp.exp2(sp - m)
        pn = jnp.exp2(sn - m)
        denom = jnp.sum(pp, axis=-1, keepdims=True) + jnp.sum(pn, axis=-1, keepdims=True)
        ctx = ((_dot(pp.astype(BF16), latp) + _dot(pn.astype(BF16), latn)) / denom).astype(BF16)
        outs = [_dot(ctx[i * t:(i + 1) * t], wuv_ref[h]) for i, h in enumerate(hs)]
        for i in range(0, group, 2):
            col = (g0 + i) * MLA_V
            o_ref[0, :, col:col + 2 * MLA_V] = jnp.concatenate([outs[i], outs[i + 1]], axis=1).astype(BF16)


def _latent_attention(q, lat_new, kr_new, cache_lat, cache_kr, j, w_uk_h, w_uv_h):
    b, t, _ = q.shape
    past = cache_lat.shape[2]
    return pl.pallas_call(
        functools.partial(_latent_attn_kernel, group=MLA_HEADS // 2),
        grid=(b,),
        in_specs=[pl.BlockSpec((1, t, MLA_HEADS * LANES), lambda i: (i, 0, 0)),
                  pl.BlockSpec((1, t, MLA_KV_LORA), lambda i: (i, 0, 0)),
                  pl.BlockSpec((1, t, LANES), lambda i: (i, 0, 0)),
                  pl.BlockSpec((None, 1, past, MLA_KV_LORA), lambda i: (j, i, 0, 0)),
                  pl.BlockSpec((None, 1, past, MLA_ROPE), lambda i: (j, i, 0, 0)),
                  _full(w_uk_h.shape), _full(w_uv_h.shape)],
        out_specs=pl.BlockSpec((1, t, MLA_HEADS * MLA_V), lambda i: (i, 0, 0)),
        out_shape=jax.ShapeDtypeStruct((b, t, MLA_HEADS * MLA_V), BF16),
        compiler_params=_params("parallel"),
    )(q, lat_new, kr_new, cache_lat, cache_kr, w_uk_h, w_uv_h)


def _layernorm(r, g, b):
    mu = jnp.mean(r, -1, keepdims=True)
    d = r - mu
    var = jnp.mean(d * d, -1, keepdims=True)
    return d * lax.rsqrt(var + LN_EPS) * g + b


def _mixer_out_kernel(a_ref, wo_ref, x_ref, gm_ref, lng_ref, lnb_ref, sc_ref, sh_ref, wr_ref, br_ref, up_ref, lo_ref,
                      xo_ref, h_ref, gate_ref, pos_ref, cnt_ref):
    y = _dot(a_ref[...], wo_ref[...])
    r = DEEPNORM_ALPHA * x_ref[...] + _rows_scale(y, 1.0 + gm_ref[...])
    xn = _layernorm(r, lng_ref[...], lnb_ref[...])
    xo_ref[...] = xn
    h = _modulate(xn, sc_ref[...], sh_ref[...])
    h_ref[...] = h.astype(BF16)

    logits = _dot3(wr_ref[...], h, _dot_nt) + br_ref[...]
    tm = logits.shape[1]
    eio = lax.broadcasted_iota(I32, (N_EXPERTS, tm), 0).astype(F32)
    sels, vals = [], []
    work = logits
    for k in range(TOP_K):
        m = jnp.max(work, axis=0, keepdims=True)
        ik = jnp.min(jnp.where(work == m, eio, float(N_EXPERTS)), axis=0, keepdims=True)
        sel = eio == ik
        work = jnp.where(sel, -jnp.inf, work)
        sels.append(sel)
        vals.append(m)
    es = [jnp.exp(v - vals[0]) for v in vals]
    tot = es[0] + es[1] + es[2] + es[3]
    for k in range(TOP_K):
        gate_ref[pl.ds(k, 1), :] = es[k] / tot
    multi = sels[0] | sels[1] | sels[2] | sels[3]
    mh = jnp.where(multi, 1.0, 0.0)
    before = _dot(mh.astype(BF16), up_ref[...])
    cnt = jnp.sum(mh, axis=1, keepdims=True)
    seg = jnp.floor((cnt + (SEG - 1)) * (1.0 / SEG)) * SEG
    seg_b = jnp.broadcast_to(seg, (N_EXPERTS, LANES)).astype(BF16)
    place = before + _dot(lo_ref[...], seg_b)[:, :1]
    for k in range(TOP_K):
        pos_ref[pl.ds(k, 1), :] = jnp.sum(jnp.where(sels[k], place, 0.0), axis=0, keepdims=True).astype(I32)
    cnt_ref[0] = cnt


def _mixer_out(a, w_o, x, gm, lng, lnb, sc, sh, w_r_t, b_r, n_seq, seq_len):
    n, kdim = a.shape
    tm = _row_tile(n_seq, seq_len, ROUTE_TILE)
    upper = (jnp.arange(tm)[:, None] < jnp.arange(tm)[None, :]).astype(BF16)
    lower_e = (jnp.arange(N_EXPERTS)[None, :] < jnp.arange(N_EXPERTS)[:, None]).astype(BF16)
    row = lambda w_: pl.BlockSpec((tm, w_), lambda i: (i, 0))
    col = pl.BlockSpec((TOP_K, tm), lambda i: (0, i))
    ss = _seq_spec(seq_len, tm)
    return pl.pallas_call(
        _mixer_out_kernel,
        grid=(n // tm,),
        in_specs=[row(kdim), _full(w_o.shape), row(D_MODEL), ss, _full((1, D_MODEL)), _full((1, D_MODEL)), ss, ss,
                  _full((N_EXPERTS, D_MODEL)), _full((N_EXPERTS, 1)), _full((tm, tm)),
                  _full((N_EXPERTS, N_EXPERTS))],
        out_specs=[row(D_MODEL), row(D_MODEL), col, col, pl.BlockSpec((1, N_EXPERTS, 1), lambda i: (i, 0, 0))],
        out_shape=[jax.ShapeDtypeStruct((n, D_MODEL), F32), jax.ShapeDtypeStruct((n, D_MODEL), BF16),
                   jax.ShapeDtypeStruct((TOP_K, n), F32), jax.ShapeDtypeStruct((TOP_K, n), I32),
                   jax.ShapeDtypeStruct((n // tm, N_EXPERTS, 1), F32)],
        compiler_params=_params("parallel"),
    )(a, w_o, x, gm, lng, lnb, sc, sh, w_r_t, b_r, upper, lower_e)


def _piece_sizes(tm):
    sizes, size = [], min(PIECE_MAX, tm)
    while size >= SEG:
        sizes.append(size)
        size //= 2
    return tuple(sizes)


def _segment_copies(seg_ref, off_ref, start_ref, tile, sizes, make):
    big = sizes[0]

    def body(e, carry):
        n = seg_ref[tile * N_EXPERTS + e]
        off = off_ref[tile * N_EXPERTS + e]
        start = start_ref[tile * N_EXPERTS + e]

        def big_piece(k, c):
            make(pl.multiple_of(off + k * big, SEG), pl.multiple_of(start + k * big, SEG), big)
            return c

        lax.fori_loop(0, n // big, big_piece, 0)
        for size in sizes[1:]:
            done = n & (-2 * size)

            @pl.when((n & size) != 0)
            def _():
                make(pl.multiple_of(off + done, SEG), pl.multiple_of(start + done, SEG), size)
        return carry

    lax.fori_loop(0, N_EXPERTS, body, 0)


def _dispatch_kernel(seg_ref, off_ref, start_ref, h_ref, pos_ref, zero_ref, xr_ref, buf_ref, sem, *, nt, sizes):
    del zero_ref
    i = pl.program_id(0)
    slot = i % 2

    def copies(tile, sl, wait):
        def make(buf_row, hbm_row, size):
            cp = pltpu.make_async_copy(buf_ref.at[sl, pl.ds(buf_row, size)], xr_ref.at[pl.ds(hbm_row, size)],
                                       sem.at[sl])
            cp.wait() if wait else cp.start()
        _segment_copies(seg_ref, off_ref, start_ref, tile, sizes, make)

    @pl.when(i >= 2)
    def _():
        copies(i - 2, slot, True)

    pos = pos_ref[...]
    rows = buf_ref.shape[1]
    rio = lax.broadcasted_iota(I32, (rows, pos.shape[1]), 0)
    sel = (rio == pos[0:1]) | (rio == pos[1:2]) | (rio == pos[2:3]) | (rio == pos[3:4])
    buf_ref[slot] = _dot(jnp.where(sel, 1.0, 0.0).astype(BF16), h_ref[...]).astype(BF16)
    copies(i, slot, False)

    @pl.when(i == nt - 1)
    def _():
        copies(i, slot, True)

        @pl.when(i >= 1)
        def _():
            copies(i - 1, 1 - slot, True)


def _dispatch(h, pos_t, seg, off, start, x_rows, tm):
    n = h.shape[0]
    nt = n // tm
    sizes = _piece_sizes(tm)
    rows = TOP_K * tm + N_EXPERTS * SEG
    grid_spec = pltpu.PrefetchScalarGridSpec(
        num_scalar_prefetch=3,
        grid=(nt,),
        in_specs=[pl.BlockSpec((tm, D_MODEL), lambda i, *_: (i, 0)),
                  pl.BlockSpec((TOP_K, tm), lambda i, *_: (0, i)),
                  pl.BlockSpec(memory_space=pl.ANY)],
        out_specs=pl.BlockSpec(memory_space=pl.ANY),
        scratch_shapes=[pltpu.VMEM((2, rows, D_MODEL), BF16), pltpu.SemaphoreType.DMA((2,))],
    )
    return pl.pallas_call(
        functools.partial(_dispatch_kernel, nt=nt, sizes=sizes),
        grid_spec=grid_spec,
        out_shape=jax.ShapeDtypeStruct(x_rows.shape, BF16),
        input_output_aliases={5: 0},
        compiler_params=_params("arbitrary"),
    )(seg, off, start, h, pos_t, x_rows)


def _expert_kernel(be_ref, na_ref, x_ref, wgu_ref, bgu_ref, wd_ref, bd_ref, y_ref, wgu_s, wd_s):
    i = pl.program_id(0)

    @pl.when((i == 0) | (be_ref[i] != be_ref[jnp.maximum(i - 1, 0)]))
    def _():
        wgu_s[...] = wgu_ref[0].astype(BF16)
        wd_s[...] = wd_ref[0].astype(BF16)

    @pl.when(i < na_ref[0])
    def _():
        gu = _dot(x_ref[...], wgu_s[...]) + bgu_ref[0]
        gate = jnp.minimum(gu[:, :D_FF], SWIGLU_LIMIT)
        up = jnp.clip(gu[:, D_FF:], -SWIGLU_LIMIT, SWIGLU_LIMIT)
        act = (up + 1.0) * gate * _sigmoid(SWIGLU_ALPHA * gate)
        y_ref[...] = (_dot(act.astype(BF16), wd_s[...]) + bd_ref[0]).astype(BF16)

    @pl.when(i >= na_ref[0])
    def _():
        y_ref[...] = jnp.zeros_like(y_ref)


def _experts(x_rows, blk_e, n_act, w_gu, b_gu, w_down, b_down, layer):
    p = x_rows.shape[0]
    grid_spec = pltpu.PrefetchScalarGridSpec(
        num_scalar_prefetch=2,
        grid=(p // MOE_ROWS,),
        in_specs=[pl.BlockSpec((MOE_ROWS, D_MODEL), lambda i, be, na: (i, 0)),
                  pl.BlockSpec((None, 1, D_MODEL, 2 * D_FF), lambda i, be, na: (layer, be[i], 0, 0)),
                  pl.BlockSpec((1, 1, 2 * D_FF), lambda i, be, na: (be[i], 0, 0)),
                  pl.BlockSpec((None, 1, D_FF, D_MODEL), lambda i, be, na: (layer, be[i], 0, 0)),
                  pl.BlockSpec((1, 1, D_MODEL), lambda i, be, na: (be[i], 0, 0))],
        out_specs=pl.BlockSpec((MOE_ROWS, D_MODEL), lambda i, be, na: (i, 0)),
        scratch_shapes=[pltpu.VMEM((D_MODEL, 2 * D_FF), BF16), pltpu.VMEM((D_FF, D_MODEL), BF16)],
    )
    return pl.pallas_call(
        _expert_kernel,
        grid_spec=grid_spec,
        out_shape=jax.ShapeDtypeStruct((p, D_MODEL), BF16),
        compiler_params=_params("arbitrary"),
    )(blk_e, n_act, x_rows, w_gu, b_gu, w_down, b_down)


def _combine_kernel(seg_ref, off_ref, start_ref, y_ref, pos_ref, g_ref, x_ref, gf_ref, lng_ref, lnb_ref, o_ref,
                    buf_ref, sem, *, nt, sizes):
    i = pl.program_id(0)
    slot = i % 2

    def copies(tile, sl, wait):
        def make(buf_row, hbm_row, size):
            cp = pltpu.make_async_copy(y_ref.at[pl.ds(hbm_row, size)], buf_ref.at[sl, pl.ds(buf_row, size)],
                                       sem.at[sl])
            cp.wait() if wait else cp.start()
        _segment_copies(seg_ref, off_ref, start_ref, tile, sizes, make)

    @pl.when(i == 0)
    def _():
        buf_ref[...] = jnp.zeros_like(buf_ref)
        copies(0, 0, False)

    @pl.when(i + 1 < nt)
    def _():
        copies(i + 1, 1 - slot, False)

    copies(i, slot, True)
    pos = pos_ref[...]
    g = g_ref[...]
    cio = lax.broadcasted_iota(I32, (pos.shape[0], buf_ref.shape[1]), 1)
    gm = jnp.where(cio == pos[:, 0:1], g[:, 0:1], 0.0)
    for k in range(1, TOP_K):
        gm = gm + jnp.where(cio == pos[:, k:k + 1], g[:, k:k + 1], 0.0)
    y = _dot(gm.astype(BF16), buf_ref[slot])
    r = DEEPNORM_ALPHA * x_ref[...] + _rows_scale(y, 1.0 + gf_ref[...])
    o_ref[...] = _layernorm(r, lng_ref[...], lnb_ref[...])


def _combine(y_rows, pos, gates, seg, off, start, x, gf, lng, lnb, seq_len, tm):
    n = x.shape[0]
    nt = n // tm
    sizes = _piece_sizes(tm)
    rows = TOP_K * tm + N_EXPERTS * SEG
    row = pl.BlockSpec((tm, D_MODEL), lambda i, *_: (i, 0))
    four = pl.BlockSpec((tm, TOP_K), lambda i, *_: (i, 0))
    grid_spec = pltpu.PrefetchScalarGridSpec(
        num_scalar_prefetch=3,
        grid=(nt,),
        in_specs=[pl.BlockSpec(memory_space=pl.ANY), four, four, row, _seq_spec(seq_len, tm),
                  _full((1, D_MODEL)), _full((1, D_MODEL))],
        out_specs=row,
        scratch_shapes=[pltpu.VMEM((2, rows, D_MODEL), BF16), pltpu.SemaphoreType.DMA((2,))],
    )
    return pl.pallas_call(
        functools.partial(_combine_kernel, nt=nt, sizes=sizes),
        grid_spec=grid_spec,
        out_shape=jax.ShapeDtypeStruct((n, D_MODEL), F32),
        compiler_params=_params("arbitrary"),
    )(seg, off, start, y_rows, pos, gates, x, gf, lng, lnb)


def _moe_rows(groups):
    worst = N_EXPERTS * (MOE_ROWS - 1)
    for g in groups:
        n = g['n_seq'] * g['seq_len']
        worst += n * TOP_K + n // _row_tile(g['n_seq'], g['seq_len'], ROUTE_TILE) * N_EXPERTS * (SEG - 1)
    return (worst + MOE_ROWS - 1) // MOE_ROWS * MOE_ROWS


def _moe_and_norm(routed, lng, lnb, w, x_rows):
    tms = [_row_tile(r['n_seq'], r['seq_len'], ROUTE_TILE) for r in routed]
    counts = jnp.concatenate([r['cnt'][:, :, 0] for r in routed], axis=0).astype(I32)
    seg = (counts + SEG - 1) // SEG * SEG
    padded = (jnp.sum(seg, axis=0) + MOE_ROWS - 1) // MOE_ROWS * MOE_ROWS
    pad_end = jnp.cumsum(padded)
    start = (pad_end - padded)[None, :] + jnp.cumsum(seg, axis=0) - seg
    off = jnp.cumsum(seg, axis=1) - seg
    tables, t0 = [], 0
    for r in routed:
        nt = r['cnt'].shape[0]
        tables.append(tuple(a[t0:t0 + nt].reshape(-1) for a in (seg, off, start)))
        t0 += nt
    for r, tm, tab in zip(routed, tms, tables):
        x_rows = _dispatch(r['h'], r['pos_t'], *tab, x_rows, tm)
    nb = x_rows.shape[0] // MOE_ROWS
    blk_row = jnp.arange(nb, dtype=I32) * MOE_ROWS
    blk_e = jnp.minimum(jnp.sum(pad_end[None, :] <= blk_row[:, None], axis=1), N_EXPERTS - 1).astype(I32)
    n_act = (pad_end[-1:] // MOE_ROWS).astype(I32)
    y_rows = _experts(x_rows, blk_e, n_act, w['w_gu'], w['b_gu'], w['w_down'], w['b_down'], w['layer'])
    outs = [_combine(y_rows, r['pos_t'].T, r['gate_t'].T, *tab, r['x'], r['gf'], lng, lnb, r['seq_len'], tm)
            for r, tm, tab in zip(routed, tms, tables)]
    return outs, x_rows


def _gdn_gates_kernel(x_ref, sc_ref, sh_ref, wz_ref, wab_ref, wabt_ref, alog_ref, dtb_ref, alogt_ref, dtbt_ref,
                      tri_ref, trit_ref, z_ref, gc_ref, beta_ref, gct_ref):
    h = _modulate(x_ref[...], sc_ref[...], sh_ref[...]).astype(BF16)
    z_ref[...] = _dot(h, wz_ref[...]).astype(BF16)
    hv = GDN_V_HEADS
    ab = _dot(h, wab_ref[...])
    beta_ref[...] = _sigmoid(ab[:, :hv])
    g = -jnp.exp(alog_ref[...]) * _softplus(ab[:, hv:] + dtb_ref[...])
    gc_ref[...] = _dot_exact_rhs_lhs(tri_ref[...], g)
    abt = _dot_nt(wabt_ref[...], h)
    gt = -jnp.exp(alogt_ref[...]) * _softplus(abt + dtbt_ref[...])
    gct_ref[...] = _dot_exact_rhs(gt, trit_ref[...])


def _dot_exact_rhs_lhs(a01, b):
    b1 = b.astype(BF16)
    r1 = b - b1.astype(F32)
    b2 = r1.astype(BF16)
    b3 = (r1 - b2.astype(F32)).astype(BF16)
    return _dot(a01, b1) + (_dot(a01, b2) + _dot(a01, b3))


def _gdn_gates(x, sc, sh, w, n_seq, seq_len, chunk):
    n = x.shape[0]
    tm = _row_tile(n_seq, seq_len, 512)
    hv = GDN_V_HEADS
    r = jnp.arange(tm)
    tri = ((r[:, None] // chunk == r[None, :] // chunk) & (r[None, :] <= r[:, None])).astype(BF16)
    row = lambda w_: pl.BlockSpec((tm, w_), lambda i: (i, 0))
    return pl.pallas_call(
        _gdn_gates_kernel,
        grid=(n // tm,),
        in_specs=[row(D_MODEL), _seq_spec(seq_len, tm), _seq_spec(seq_len, tm), _full(w['w_z'].shape),
                  _full((D_MODEL, 2 * hv)), _full((hv, D_MODEL)), _full((1, hv)), _full((1, hv)),
                  _full((hv, 1)), _full((hv, 1)), _full((tm, tm)), _full((tm, tm))],
        out_specs=[row(GDN_V_DIM), row(hv), row(hv), pl.BlockSpec((hv, tm), lambda i: (0, i))],
        out_shape=[jax.ShapeDtypeStruct((n, GDN_V_DIM), BF16), jax.ShapeDtypeStruct((n, hv), F32),
                   jax.ShapeDtypeStruct((n, hv), F32), jax.ShapeDtypeStruct((hv, n), F32)],
        compiler_params=_params("parallel"),
    )(x, sc, sh, w['w_z'], w['w_ab'], w['w_a_t'], w['a_log'], w['dt_bias'], w['a_log'].T, w['dt_bias'].T,
      tri, tri.T)


def _qkv_conv_kernel(x_ref, xh_ref, sc_ref, sh_ref, w_ref, st_ref, cw_ref, o_ref, last_ref, h_s, *, per, n_q, n_qk):
    i = pl.program_id(0)
    j = pl.program_id(1)
    sc, sh = sc_ref[...], sh_ref[...]

    @pl.when(j == 0)
    def _():
        h_s[...] = _modulate(x_ref[...], sc, sh).astype(BF16)

    w = w_ref[...]
    x = _dot(h_s[...], w)
    tm = x.shape[0]
    prev = _dot(_modulate(xh_ref[...], sc, sh).astype(BF16), w)[HALO:]
    halo = jnp.where(i % per == 0, st_ref[0], prev)
    last_ref[0] = x[tm - HALO:]
    cw = cw_ref[...]
    rows = lax.broadcasted_iota(I32, (HALO, x.shape[1]), 0)
    acc = x * cw[GDN_CONV_W - 1:GDN_CONV_W]
    for s in range(1, GDN_CONV_W):
        xs = pltpu.roll(x, s, 0)
        head = jnp.where(rows < s, pltpu.roll(halo, s, 0), xs[:HALO])
        xs = jnp.concatenate([head, xs[HALO:]], axis=0) if tm > HALO else head
        acc = acc + xs * cw[GDN_CONV_W - 1 - s:GDN_CONV_W - s]
    y = _silu(acc)

    @pl.when(j < n_qk)
    def _():
        scale = jnp.where(j < n_q, GDN_DK ** -0.5, 1.0)
        for hd in range(y.shape[1] // GDN_DK):
            sl = slice(hd * GDN_DK, (hd + 1) * GDN_DK)
            yh = y[:, sl]
            o_ref[:, sl] = yh * (lax.rsqrt(jnp.sum(yh * yh, -1, keepdims=True) + L2_EPS) * scale)

    @pl.when(j >= n_qk)
    def _():
        o_ref[...] = y


def _gdn_qkv_conv(x, sc, sh, w_qkv, state8, conv_w, n_seq, seq_len):
    n = x.shape[0]
    assert seq_len >= 2 * HALO
    tm = _tile(seq_len, 1024)
    per = seq_len // tm
    tn = 512 if tm >= 512 else GDN_QK_DIM
    width = w_qkv.shape[1]
    kern = functools.partial(_qkv_conv_kernel, per=per, n_q=GDN_QK_DIM // tn, n_qk=2 * GDN_QK_DIM // tn)
    hb = tm // (2 * HALO)
    act, last = pl.pallas_call(
        kern,
        grid=(n // tm, width // tn),
        in_specs=[pl.BlockSpec((tm, D_MODEL), lambda i, j: (i, 0)),
                  pl.BlockSpec((2 * HALO, D_MODEL), lambda i, j: (jnp.maximum(i * hb - 1, 0), 0)),
                  _seq_spec(seq_len, tm), _seq_spec(seq_len, tm),
                  pl.BlockSpec((D_MODEL, tn), lambda i, j: (0, j)),
                  pl.BlockSpec((1, HALO, tn), lambda i, j: (i // per, 0, j)),
                  pl.BlockSpec((GDN_CONV_W, tn), lambda i, j: (0, j))],
        out_specs=[pl.BlockSpec((tm, tn), lambda i, j: (i, j)),
                   pl.BlockSpec((1, HALO, tn), lambda i, j: (i, 0, j))],
        out_shape=[jax.ShapeDtypeStruct((n, width), F32), jax.ShapeDtypeStruct((n // tm, HALO, width), F32)],
        scratch_shapes=[pltpu.VMEM((tm, D_MODEL), BF16)],
        compiler_params=_params("parallel", "arbitrary"),
    )(x, x, sc, sh, w_qkv, state8, conv_w)
    return act, last.reshape(n_seq, per, HALO, width)[:, -1]


def _inv_masks(c):
    row = lax.broadcasted_iota(I32, (c, c), 0)
    col = lax.broadcasted_iota(I32, (c, c), 1)
    eye = jnp.where(row == col, 1.0, 0.0)
    diag = row // INV_BASE == col // INV_BASE
    offs = []
    size = INV_BASE
    while size < c:
        offs.append((row // (2 * size) == col // (2 * size)) & (row // size % 2 == 1) & (col // size % 2 == 0))
        size *= 2
    return eye, diag, offs


def _inv_unit_lower(lmats, masks):
    eye, diag, offs = masks
    pws = [jnp.where(diag, lm, 0.0) for lm in lmats]
    ts = [eye - pw for pw in pws]
    size = 2
    while size < INV_BASE:
        pws = [_dotb(pw, pw) for pw in pws]
        ts = [t + _dotb(t, pw) for t, pw in zip(ts, pws)]
        size *= 2
    for off in offs:
        tbs = [_dotb(t, jnp.where(off, lm, 0.0)) for t, lm in zip(ts, lmats)]
        ts = [t - _dotb(tb, t) for t, tb in zip(ts, tbs)]
    return ts


def _gdn_core_kernel(q_ref, k_ref, v_ref, z_ref, gc_ref, beta_ref, gct_ref, s0_ref, nw_ref, o_ref, s_ref, *, c, hb):
    @pl.when(pl.program_id(2) == 0)
    def _():
        s_ref[...] = s0_ref[...]

    row = lax.broadcasted_iota(I32, (c, c), 0)
    col = lax.broadcasted_iota(I32, (c, c), 1)
    lower = col <= row
    strict = col < row
    masks = _inv_masks(c)
    heads = range(hb)
    ksl = [slice(kh * GDN_DK, (kh + 1) * GDN_DK) for kh in range(hb // 2)]
    vsl = [slice(h * GDN_DV, (h + 1) * GDN_DV) for h in heads]
    qn = [q_ref[0, :, sl] for sl in ksl]
    kn = [k_ref[0, :, sl] for sl in ksl]
    kb = [k.astype(BF16) for k in kn]
    kk = [_dot_nt(k, k) for k in kb]
    qk = [_dot_nt(q.astype(BF16), k) for q, k in zip(qn, kb)]
    gcc = [gc_ref[0, 0, :, h:h + 1] for h in heads]
    bc = [beta_ref[0, 0, :, h:h + 1] for h in heads]
    decay = [jnp.where(lower, jnp.exp(jnp.minimum(gcc[h] - gct_ref[0, 0, h:h + 1, :], 0.0)), 0.0) for h in heads]
    lmat = [jnp.where(strict, kk[h // 2] * decay[h], 0.0) * bc[h] for h in heads]
    ts = _inv_unit_lower(lmat, masks)
    egc = [jnp.exp(g) for g in gcc]
    rhs = [jnp.concatenate([v_ref[0, :, vsl[h]] * bc[h], kn[h // 2] * (bc[h] * egc[h])], axis=1).astype(BF16)
           for h in heads]
    uw = [_dot(ts[h].astype(BF16), rhs[h]) for h in heads]
    s_old = [s_ref[0, h] for h in heads]
    sb = [s.astype(BF16) for s in s_old]
    ws = [_dot(uw[h][:, GDN_DV:].astype(BF16), sb[h]) for h in heads]
    qs = [_dot((qn[h // 2] * egc[h]).astype(BF16), sb[h]) for h in heads]
    vb = [(uw[h][:, :GDN_DV] - ws[h]).astype(BF16) for h in heads]
    o = [qs[h] + _dot((qk[h // 2] * decay[h]).astype(BF16), vb[h]) for h in heads]
    glast = [g[c - 1:c, :] for g in gcc]
    kd = [(kn[h // 2] * jnp.exp(glast[h] - gcc[h])).astype(BF16) for h in heads]
    s_new = [s_old[h] * jnp.exp(glast[h]) + _dot_tn(kd[h], vb[h]) for h in heads]
    for h in heads:
        s_ref[0, h] = s_new[h]
        z = z_ref[0, :, vsl[h]].astype(F32)
        on = o[h] * lax.rsqrt(jnp.mean(o[h] * o[h], -1, keepdims=True) + RMS_EPS) * nw_ref[...]
        o_ref[0, :, vsl[h]] = (on * _silu(z)).astype(BF16)


def _gdn_core(qkv, z, gc, beta, gct, s0, norm_w, c, hb=GDN_V_HEADS):
    b, t, _ = z.shape
    ng = GDN_V_HEADS // hb
    kw = hb // 2 * GDN_DK
    vw = hb * GDN_DV
    gc_g = gc.reshape(b, t, ng, hb).transpose(0, 2, 1, 3)
    beta_g = beta.reshape(b, t, ng, hb).transpose(0, 2, 1, 3)
    gct_g = gct.reshape(b, ng, hb, t)
    kern = functools.partial(_gdn_core_kernel, c=c, hb=hb)
    nkb = GDN_QK_DIM // kw
    nvb = 2 * GDN_QK_DIM // vw
    return pl.pallas_call(
        kern,
        grid=(b, ng, t // c),
        in_specs=[pl.BlockSpec((1, c, kw), lambda b_, g, i: (b_, i, g)),
                  pl.BlockSpec((1, c, kw), lambda b_, g, i: (b_, i, nkb + g)),
                  pl.BlockSpec((1, c, vw), lambda b_, g, i: (b_, i, nvb + g)),
                  pl.BlockSpec((1, c, vw), lambda b_, g, i: (b_, i, g)),
                  pl.BlockSpec((1, 1, c, hb), lambda b_, g, i: (b_, g, i, 0)),
                  pl.BlockSpec((1, 1, c, hb), lambda b_, g, i: (b_, g, i, 0)),
                  pl.BlockSpec((1, 1, hb, c), lambda b_, g, i: (b_, g, 0, i)),
                  pl.BlockSpec((1, hb, GDN_DK, GDN_DV), lambda b_, g, i: (b_, g, 0, 0)),
                  _full((1, GDN_DV))],
        out_specs=[pl.BlockSpec((1, c, vw), lambda b_, g, i: (b_, i, g)),
                   pl.BlockSpec((1, hb, GDN_DK, GDN_DV), lambda b_, g, i: (b_, g, 0, 0))],
        out_shape=[jax.ShapeDtypeStruct((b, t, GDN_V_DIM), BF16),
                   jax.ShapeDtypeStruct((b, GDN_V_HEADS, GDN_DK, GDN_DV), F32)],
        compiler_params=_params("parallel", "parallel", "arbitrary"),
    )(qkv, qkv, qkv, z, gc_g, beta_g, gct_g, s0, norm_w)


def _prep_weights(p):
    f = {}
    pad_heads = lambda w_, dh: jnp.pad(w_.reshape(w_.shape[0], w_.shape[1], MLA_HEADS, dh),
                                       ((0, 0), (0, 0), (0, 0), (0, LANES - dh))).reshape(
                                           w_.shape[0], w_.shape[1], MLA_HEADS * LANES)
    w_in = p['mla_w_in']
    nl = w_in.shape[0]
    lo = MLA_Q_LORA + MLA_KV_LORA
    z = lambda k: jnp.zeros((nl, D_MODEL, k), F32)
    f['mla_w_in'] = jnp.concatenate([w_in[..., :lo], z(MLA_NOPE), w_in[..., lo:], z(LANES - MLA_QK)], -1).astype(BF16)
    f['mla_w_uq'] = pad_heads(p['mla_w_uq'], MLA_QK).astype(BF16)
    f['mla_w_uk'] = pad_heads(p['mla_w_uk'], MLA_NOPE).astype(BF16)
    f['mla_w_uv'] = pad_heads(p['mla_w_uv'], MLA_V).astype(BF16)
    f['mla_w_o'] = p['mla_w_o'].astype(BF16)
    per_head = lambda w_, dh: w_.reshape(w_.shape[0], w_.shape[1], MLA_HEADS, dh)
    f['mla_w_uk_t'] = per_head(p['mla_w_uk'], MLA_NOPE).transpose(0, 2, 3, 1).astype(BF16)
    f['mla_w_uv_h'] = per_head(p['mla_w_uv'], MLA_V).transpose(0, 2, 1, 3).astype(BF16)
    f['v_one'] = jnp.tile((jnp.arange(LANES) == MLA_V).astype(F32), MLA_HEADS)[None]
    g_in = p['gdn_w_in']
    f['gdn_w_qkv'] = g_in[..., :GDN_CONV_DIM].astype(BF16)
    f['gdn_w_z'] = g_in[..., GDN_CONV_DIM:GDN_CONV_DIM + GDN_V_DIM].astype(BF16)
    f['gdn_w_ab'] = g_in[..., GDN_CONV_DIM + GDN_V_DIM:].astype(BF16)
    f['gdn_w_a_t'] = jnp.swapaxes(g_in[..., GDN_CONV_DIM + GDN_V_DIM + GDN_V_HEADS:], 1, 2).astype(BF16)
    f['gdn_w_o'] = p['gdn_w_o'].astype(BF16)
    f['moe_w_r_t'] = jnp.swapaxes(p['moe_w_router'], 1, 2)
    return f


def _rope_tables(pos):
    half = MLA_ROPE // 2
    inv_freq = ROPE_BASE ** (-jnp.arange(half, dtype=F32) / half)
    ang = pos.astype(F32)[:, None] * inv_freq[None, :]
    cos, sin = jnp.cos(ang), jnp.sin(ang)
    t = pos.shape[0]
    z = lambda k: jnp.zeros((t, k), F32)
    cos_t = jnp.concatenate([jnp.ones((t, MLA_NOPE), F32), cos, cos, z(LANES - MLA_QK)], -1)
    sinp_t = jnp.concatenate([z(MLA_NOPE + half), sin, z(LANES - MLA_QK)], -1)
    sinm_t = jnp.concatenate([z(MLA_NOPE), -sin, z(half + LANES - MLA_QK)], -1)
    return cos_t, sinp_t, sinm_t


def _mixer(g, layer, mods, p, f):
    x, bsz, t, past = g['x'], g['n_seq'], g['seq_len'], g['past']
    sh_m, sc_m = mods[0], mods[1]
    n = bsz * t
    j = layer // 2
    if layer % 2 == 0:
        w = dict(w_in=f['mla_w_in'][j], q_norm=p['mla_q_norm'][j][None], kv_norm=p['mla_kv_norm'][j][None],
                 w_uq=f['mla_w_uq'][j], w_uk=f['mla_w_uk'][j], w_uv=f['mla_w_uv'][j], v_one=f['v_one'])
        q, lat, krt = _mla_proj(x, sc_m, sh_m, w, g['tabs'], bsz, t)
        g['lats'].append(lat.reshape(bsz, t, MLA_KV_LORA))
        g['krs'].append(krt[:, MLA_NOPE:MLA_QK].reshape(bsz, t, MLA_ROPE))
        wide = MLA_HEADS * LANES
        if past is None:
            k, v = _kv_expand(lat, krt, w)
            tq = _tile(t, 512)
            ctx = _attention(q.reshape(bsz, t, wide), k.reshape(bsz, t, wide), v.reshape(bsz, t, wide), tq, tq)
        else:
            ctx = _latent_attention(q.reshape(bsz, t, wide), lat.reshape(bsz, t, -1), krt.reshape(bsz, t, -1),
                                    past[0], past[1], j, f['mla_w_uk_t'][j], f['mla_w_uv_h'][j])
        return ctx.reshape(n, MLA_HEADS * MLA_V), f['mla_w_o'][j]
    chunk = _tile(t, 128)
    w = dict(w_z=f['gdn_w_z'][j], w_ab=f['gdn_w_ab'][j], w_a_t=f['gdn_w_a_t'][j], a_log=p['gdn_a_log'][j][None],
             dt_bias=p['gdn_dt_bias'][j][None])
    z, gc, beta, gct = _gdn_gates(x, sc_m, sh_m, w, bsz, t, chunk)
    if past is None:
        conv_state = jnp.zeros((bsz, GDN_CONV_W - 1, GDN_CONV_DIM), F32)
        s0 = jnp.zeros((bsz, GDN_V_HEADS, GDN_DK, GDN_DV), F32)
    else:
        conv_state, s0 = past[2][j], past[3][j]
    state8 = jnp.pad(conv_state, ((0, 0), (HALO - (GDN_CONV_W - 1), 0), (0, 0)))
    qkv, last = _gdn_qkv_conv(x, sc_m, sh_m, f['gdn_w_qkv'][j], state8, p['gdn_conv_w'][j], bsz, t)
    g['convs'].append(last[:, HALO - (GDN_CONV_W - 1):])
    o, s_new = _gdn_core(qkv.reshape(bsz, t, -1), z.reshape(bsz, t, -1), gc.reshape(bsz, t, -1),
                         beta.reshape(bsz, t, -1), gct.reshape(GDN_V_HEADS, bsz, t).transpose(1, 0, 2), s0,
                         p['gdn_norm'][j][None], chunk)
    g['ssms'].append(s_new)
    return o.reshape(n, GDN_V_DIM), f['gdn_w_o'][j]


def _forward(groups, p, f):
    mods = _cond(jnp.concatenate([g['c'] for g in groups], axis=0), p['w_cond'], p['b_cond'])
    seq0 = 0
    for g in groups:
        bsz, t, _ = g['x'].shape
        g.update(n_seq=bsz, seq_len=t, x=g['x'].reshape(bsz * t, D_MODEL), lats=[], krs=[], convs=[], ssms=[],
                 mod=mods[:, seq0:seq0 + bsz].reshape(DEPTH, bsz, 6, 1, D_MODEL))
        seq0 += bsz
        tabs = _rope_tables(g['pos'])
        tm_rope = _row_tile(bsz, t, 512)
        g['tabs'] = tuple(jnp.tile(tb, (tm_rope // t, 1)) for tb in tabs) if tm_rope > t else tabs
    x_rows = jnp.zeros((_moe_rows(groups), D_MODEL), BF16)
    for layer in range(DEPTH):
        routed = []
        for g in groups:
            sh_m, sc_m, g_m, sh_f, sc_f, g_f = [g['mod'][layer, :, i] for i in range(6)]
            a, w_o = _mixer(g, layer, (sh_m, sc_m), p, f)
            x, h, gate_t, pos_t, cnt = _mixer_out(
                a, w_o, g['x'], g_m, p['ln1_g'][layer][None], p['ln1_b'][layer][None], sc_f, sh_f,
                f['moe_w_r_t'][layer], p['moe_b_router'][layer][:, None], g['n_seq'], g['seq_len'])
            routed.append(dict(x=x, h=h, gate_t=gate_t, pos_t=pos_t, cnt=cnt, gf=g_f, n_seq=g['n_seq'],
                               seq_len=g['seq_len']))
        wm = dict(w_gu=p['moe_w_gu'], b_gu=p['moe_b_gu'][layer][:, None], w_down=p['moe_w_down'],
                  b_down=p['moe_b_down'][layer][:, None], layer=layer)
        xs, x_rows = _moe_and_norm(routed, p['ln2_g'][layer][None], p['ln2_b'][layer][None], wm, x_rows)
        for g, x in zip(groups, xs):
            g['x'] = x
    return [(g['x'].reshape(g['n_seq'], g['seq_len'], D_MODEL), jnp.stack(g['lats']), jnp.stack(g['krs']),
             jnp.stack(g['convs']), jnp.stack(g['ssms'])) for g in groups]


def kernel(x_prompt, x_sample, c_prompt, c_sample, cache_mla_latent, cache_mla_krope, state_gdn_conv, state_gdn_ssm, w_cond, b_cond, ln1_g, ln1_b, ln2_g, ln2_b, mla_w_in, mla_q_norm, mla_kv_norm, mla_w_uq, mla_w_uk, mla_w_uv, mla_w_o, gdn_w_in, gdn_conv_w, gdn_a_log, gdn_dt_bias, gdn_norm, gdn_w_o, moe_w_router, moe_b_router, moe_w_gu, moe_b_gu, moe_w_down, moe_b_down):
    p = dict(w_cond=w_cond, b_cond=b_cond, ln1_g=ln1_g, ln1_b=ln1_b, ln2_g=ln2_g, ln2_b=ln2_b,
             mla_w_in=mla_w_in, mla_q_norm=mla_q_norm, mla_kv_norm=mla_kv_norm, mla_w_uq=mla_w_uq,
             mla_w_uk=mla_w_uk, mla_w_uv=mla_w_uv, mla_w_o=mla_w_o, gdn_w_in=gdn_w_in,
             gdn_conv_w=gdn_conv_w, gdn_a_log=gdn_a_log, gdn_dt_bias=gdn_dt_bias, gdn_norm=gdn_norm,
             gdn_w_o=gdn_w_o, moe_w_router=moe_w_router, moe_b_router=moe_b_router, moe_w_gu=moe_w_gu,
             moe_b_gu=moe_b_gu, moe_w_down=moe_w_down, moe_b_down=moe_b_down)
    f = _prep_weights(p)
    past_len = cache_mla_latent.shape[2]
    assert past_len % CHUNK == 0 and x_sample.shape[1] <= CHUNK
    past = (cache_mla_latent, cache_mla_krope, state_gdn_conv, state_gdn_ssm)
    prompt = dict(x=x_prompt, c=c_prompt, pos=jnp.arange(x_prompt.shape[1], dtype=I32), past=None)
    sample = dict(x=x_sample, c=c_sample, pos=past_len + jnp.arange(x_sample.shape[1], dtype=I32), past=past)
    (y_p, p_lat, p_kr, p_conv, p_ssm), (y_s, s_lat, s_kr, s_conv, s_ssm) = _forward([prompt, sample], p, f)
    return (y_p, y_s, p_lat, p_kr, p_conv, p_ssm, s_lat, s_kr, s_conv, s_ssm)
```

```python
import functools
import math

import jax
import jax.numpy as jnp
from jax import lax
from jax.experimental import pallas as pl
from jax.experimental.pallas import tpu as pltpu

F32 = jnp.float32
BF16 = jnp.bfloat16
I32 = jnp.int32

D_MODEL = 1024
DEPTH = 4
CHUNK = 64
MLA_HEADS = 16
MLA_Q_LORA = 768
MLA_KV_LORA = 256
MLA_NOPE = 64
MLA_ROPE = 32
MLA_V = 64
MLA_QK = MLA_NOPE + MLA_ROPE
MLA_SCALE = MLA_QK ** -0.5
ROPE_BASE = 10000.0
GDN_QK_HEADS = 8
GDN_V_HEADS = 16
GDN_DK = 128
GDN_DV = 128
GDN_QK_DIM = GDN_QK_HEADS * GDN_DK
GDN_V_DIM = GDN_V_HEADS * GDN_DV
GDN_CONV_DIM = 2 * GDN_QK_DIM + GDN_V_DIM
GDN_CONV_W = 4
N_EXPERTS = 32
TOP_K = 4
D_FF = D_MODEL
SWIGLU_ALPHA = 1.702
SWIGLU_LIMIT = 7.0
DEEPNORM_ALPHA = (2 * DEPTH) ** 0.25
LN_EPS = 1e-5
RMS_EPS = 1e-6
L2_EPS = 1e-6

LANES = 128
HALO = 8
LOG2E = 1.4426950408889634
MOE_ROWS = 512
ROUTE_TILE = 512
SEG = 16
INV_BASE = 16
VMEM_LIMIT = 48 * 1024 * 1024


def _params(*sem):
    return pltpu.CompilerParams(dimension_semantics=sem, vmem_limit_bytes=VMEM_LIMIT)


def _tile(n, pref, mult=8):
    t = min(pref, n)
    while t >= mult:
        if n % t == 0 and t % mult == 0:
            return t
        t -= 1
    return n


def _dot(a, b):
    return jnp.dot(a, b, preferred_element_type=F32)


def _dot_nt(a, b):
    return lax.dot_general(a, b, (((1,), (1,)), ((), ())), preferred_element_type=F32)


def _dot_tn(a, b):
    return lax.dot_general(a, b, (((0,), (0,)), ((), ())), preferred_element_type=F32)


def _dotb(a, b):
    return _dot(a.astype(BF16), b.astype(BF16))


def _split(a):
    hi = a.astype(BF16)
    lo = (a - hi.astype(F32)).astype(BF16)
    return hi, lo


def _dot3(a, b, dot=_dot):
    ah, al = _split(a)
    bh, bl = _split(b)
    return dot(ah, bh) + (dot(ah, bl) + dot(al, bh))


def _dot_exact_rhs(a, b01, dot=_dot):
    a1 = a.astype(BF16)
    r1 = a - a1.astype(F32)
    a2 = r1.astype(BF16)
    a3 = (r1 - a2.astype(F32)).astype(BF16)
    return dot(a1, b01) + (dot(a2, b01) + dot(a3, b01))


def _sigmoid(x):
    return 1.0 / (1.0 + jnp.exp(-x))


def _silu(x):
    return x * _sigmoid(x)


def _softplus(x):
    return jnp.maximum(x, 0.0) + jnp.log(1.0 + jnp.exp(-jnp.abs(x)))


def _rows_scale(x, s):
    spt = s.shape[0]
    if spt == 1:
        return x * s[0]
    tm, d = x.shape
    return (x.reshape(spt, tm // spt, d) * s).reshape(tm, d)


def _rows_add(x, s):
    spt = s.shape[0]
    if spt == 1:
        return x + s[0]
    tm, d = x.shape
    return (x.reshape(spt, tm // spt, d) + s).reshape(tm, d)


def _modulate(x, sc, sh):
    return _rows_add(_rows_scale(x, 1.0 + sc), sh)


def _seq_spec(seq_len, tm, d=D_MODEL):
    if tm <= seq_len:
        per = seq_len // tm
        return pl.BlockSpec((1, 1, d), lambda i, *_: (i // per, 0, 0))
    return pl.BlockSpec((tm // seq_len, 1, d), lambda i, *_: (i, 0, 0))


def _row_tile(n_seq, seq_len, pref):
    if seq_len >= pref:
        return _tile(seq_len, pref)
    spt = _tile(n_seq, max(pref // seq_len, 1), mult=1)
    return spt * seq_len


def _full(shape):
    nd = len(shape)
    return pl.BlockSpec(shape, lambda *_: (0,) * nd)


def _cond_kernel(c_ref, w_ref, b_ref, o_ref):
    c = c_ref[...]
    o_ref[0] = _dot3(_silu(c), w_ref[0]) + b_ref[0]


def _cond(c, w_cond, b_cond):
    n_seq = c.shape[0]
    c = jnp.pad(c, ((0, -n_seq % HALO), (0, 0)))
    s = c.shape[0]
    n_out = w_cond.shape[-1]
    tn = _tile(n_out, 1536, LANES)
    return pl.pallas_call(
        _cond_kernel,
        grid=(DEPTH, n_out // tn),
        in_specs=[pl.BlockSpec((s, D_MODEL), lambda l, j: (0, 0)),
                  pl.BlockSpec((1, D_MODEL, tn), lambda l, j: (l, 0, j)),
                  pl.BlockSpec((1, 1, tn), lambda l, j: (l, 0, j))],
        out_specs=pl.BlockSpec((1, s, tn), lambda l, j: (l, 0, j)),
        out_shape=jax.ShapeDtypeStruct((DEPTH, s, n_out), F32),
        compiler_params=_params("parallel", "parallel"),
    )(c, w_cond, b_cond.reshape(DEPTH, 1, n_out))[:, :n_seq]


def _rope_tile(x, cos, sinp, sinm):
    half = MLA_ROPE // 2
    return x * cos + pltpu.roll(x, half, 1) * sinp + pltpu.roll(x, LANES - half, 1) * sinm


def _mla_proj_kernel(x_ref, sc_ref, sh_ref, win_ref, qn_ref, kvn_ref, wuq_ref, cos_ref, sinp_ref, sinm_ref,
                     q_ref, lat_ref, kr_ref):
    h = _modulate(x_ref[...], sc_ref[...], sh_ref[...]).astype(BF16)
    down = _dot(h, win_ref[...])
    cq = down[:, :MLA_Q_LORA]
    cq = cq * lax.rsqrt(jnp.mean(cq * cq, -1, keepdims=True) + RMS_EPS) * qn_ref[...]
    lat = down[:, MLA_Q_LORA:MLA_Q_LORA + MLA_KV_LORA]
    lat_ref[...] = lat * lax.rsqrt(jnp.mean(lat * lat, -1, keepdims=True) + RMS_EPS) * kvn_ref[...]
    cos, sinp, sinm = cos_ref[...], sinp_ref[...], sinm_ref[...]
    kr_ref[...] = _rope_tile(down[:, MLA_Q_LORA + MLA_KV_LORA:], cos, sinp, sinm)
    q = _dot(cq.astype(BF16), wuq_ref[...])
    for hd in range(MLA_HEADS):
        sl = slice(hd * LANES, (hd + 1) * LANES)
        q_ref[:, sl] = (_rope_tile(q[:, sl], cos, sinp, sinm) * (MLA_SCALE * LOG2E)).astype(BF16)


def _mla_proj(x, sc, sh, w, rope_tab, n_seq, seq_len):
    n = x.shape[0]
    tm = _row_tile(n_seq, seq_len, 512)
    tab_rows = rope_tab[0].shape[0]
    per = tab_rows // tm
    tab_spec = pl.BlockSpec((tm, LANES), lambda i: (i % per, 0))
    row = lambda w_: pl.BlockSpec((tm, w_), lambda i: (i, 0))
    return pl.pallas_call(
        _mla_proj_kernel,
        grid=(n // tm,),
        in_specs=[row(D_MODEL), _seq_spec(seq_len, tm), _seq_spec(seq_len, tm),
                  _full(w['w_in'].shape), _full((1, MLA_Q_LORA)), _full((1, MLA_KV_LORA)), _full(w['w_uq'].shape),
                  tab_spec, tab_spec, tab_spec],
        out_specs=[row(MLA_HEADS * LANES), row(MLA_KV_LORA), row(LANES)],
        out_shape=[jax.ShapeDtypeStruct((n, MLA_HEADS * LANES), BF16),
                   jax.ShapeDtypeStruct((n, MLA_KV_LORA), F32),
                   jax.ShapeDtypeStruct((n, LANES), F32)],
        compiler_params=_params("parallel"),
    )(x, sc, sh, w['w_in'], w['q_norm'], w['kv_norm'], w['w_uq'], *rope_tab)


def _kv_expand_kernel(lat_ref, kr_ref, wuk_ref, wuv_ref, one_ref, k_ref, v_ref):
    lat = lat_ref[...].astype(BF16)
    kn = _dot(lat, wuk_ref[...])
    kr = kr_ref[...]
    for hd in range(MLA_HEADS):
        sl = slice(hd * LANES, (hd + 1) * LANES)
        k_ref[:, sl] = (kn[:, sl] + kr).astype(BF16)
    v_ref[...] = (_dot(lat, wuv_ref[...]) + one_ref[...]).astype(BF16)


def _kv_expand(lat, kr, w):
    m = lat.shape[0]
    tm = _tile(m, 1024)
    wide = MLA_HEADS * LANES
    row = lambda w_: pl.BlockSpec((tm, w_), lambda i: (i, 0))
    return pl.pallas_call(
        _kv_expand_kernel,
        grid=(m // tm,),
        in_specs=[row(MLA_KV_LORA), row(LANES), _full((MLA_KV_LORA, wide)), _full((MLA_KV_LORA, wide)),
                  _full((1, wide))],
        out_specs=[row(wide), row(wide)],
        out_shape=[jax.ShapeDtypeStruct((m, wide), BF16)] * 2,
        compiler_params=_params("parallel"),
    )(lat, kr, w['w_uk'], w['w_uv'], w['v_one'])


def _attn_kernel(q_ref, k_ref, v_ref, o_ref, *, tq, tk, n_kv, q_off, hp):
    i = pl.program_id(2)
    qpos0 = q_off + i * tq
    n_full = jnp.minimum(((qpos0 // CHUNK) + 1) * CHUNK // tk, n_kv)
    n_end = jnp.minimum((((qpos0 + tq - 1) // CHUNK + 1) * CHUNK + tk - 1) // tk, n_kv)
    lane = lax.broadcasted_iota(I32, (tq, LANES), 1)
    heads = [slice(hh * LANES, (hh + 1) * LANES) for hh in range(hp)]
    qs = [q_ref[0, :, sl] for sl in heads]

    def step(j, carry, masked, width):
        ks = pl.multiple_of(j * width, 16)
        ss = [_dot_nt(q, k_ref[0, pl.ds(ks, width), sl]) for q, sl in zip(qs, heads)]
        if masked:
            qc = (qpos0 + lax.broadcasted_iota(I32, (tq, width), 0)) // CHUNK
            kc = (ks + lax.broadcasted_iota(I32, (tq, width), 1)) // CHUNK
            visible = kc <= qc
            ss = [jnp.where(visible, s, -1e30) for s in ss]
        ms = [jnp.maximum(m, jnp.max(s, axis=-1, keepdims=True)) for (m, _), s in zip(carry, ss)]
        ps = [jnp.exp2(s - m).astype(BF16) for s, m in zip(ss, ms)]
        pvs = [_dot(p, v_ref[0, pl.ds(ks, width), sl]) for p, sl in zip(ps, heads)]
        return tuple((m_new, acc * jnp.exp2(m - m_new) + pv) for (m, acc), m_new, pv in zip(carry, ms, pvs))

    carry = ((jnp.full((tq, 1), -1e30, F32), jnp.zeros((tq, LANES), F32)),) * hp
    n_pair = n_full // 2
    carry = lax.fori_loop(0, n_pair, functools.partial(step, masked=False, width=2 * tk), carry)
    carry = lax.fori_loop(2 * n_pair, n_full, functools.partial(step, masked=False, width=tk), carry)
    carry = lax.fori_loop(n_full, n_end, functools.partial(step, masked=True, width=tk), carry)
    outs = [acc / jnp.sum(jnp.where(lane == MLA_V, acc, 0.0), axis=-1, keepdims=True) for _, acc in carry]
    for pr in range(hp // 2):
        o = jnp.where(lane < MLA_V, outs[2 * pr], pltpu.roll(outs[2 * pr + 1], MLA_V, 1))
        o_ref[0, :, pr * LANES:(pr + 1) * LANES] = o.astype(BF16)


def _attention(q, k, v, tq, tk, hp=4):
    b, t_q, _ = q.shape
    t_k = k.shape[1]
    kern = functools.partial(_attn_kernel, tq=tq, tk=tk, n_kv=t_k // tk, q_off=t_k - t_q, hp=hp)
    wide = hp * LANES
    kv_spec = pl.BlockSpec((1, t_k, wide), lambda b_, g, i: (b_, 0, g), pipeline_mode=pl.Buffered(1))
    return pl.pallas_call(
        kern,
        grid=(b, MLA_HEADS // hp, t_q // tq),
        in_specs=[pl.BlockSpec((1, tq, wide), lambda b_, g, i: (b_, i, g)), kv_spec, kv_spec],
        out_specs=pl.BlockSpec((1, tq, hp * MLA_V), lambda b_, g, i: (b_, i, g)),
        out_shape=jax.ShapeDtypeStruct((b, t_q, MLA_HEADS * MLA_V), BF16),
        compiler_params=_params("parallel", "parallel", "arbitrary"),
    )(q, k, v)


def _latent_attn_kernel(q_ref, latn_ref, krn_ref, latp_ref, krp_ref, wuk_ref, wuv_ref, o_ref, *, group):
    latp = latp_ref[0].astype(BF16)
    latn = latn_ref[0].astype(BF16)
    krp = krp_ref[0].astype(BF16)
    krn = krn_ref[0][:, MLA_NOPE:MLA_QK].astype(BF16)
    t = latn.shape[0]
    for g0 in range(0, MLA_HEADS, group):
        hs = range(g0, g0 + group)
        qa = jnp.concatenate([_dot(q_ref[0, :, h * LANES:h * LANES + MLA_NOPE], wuk_ref[h]) for h in hs],
                             axis=0).astype(BF16)
        qr = jnp.concatenate([q_ref[0, :, h * LANES + MLA_NOPE:h * LANES + MLA_QK] for h in hs], axis=0)
        sp = _dot_nt(qa, latp) + _dot_nt(qr, krp)
        sn = _dot_nt(qa, latn) + _dot_nt(qr, krn)
        m = jnp.maximum(jnp.max(sp, axis=-1, keepdims=True), jnp.max(sn, axis=-1, keepdims=True))
        pp = jnp.exp2(sp - m)
        pn = jnp.exp2(sn - m)
        denom = jnp.sum(pp, axis=-1, keepdims=True) + jnp.sum(pn, axis=-1, keepdims=True)
        ctx = ((_dot(pp.astype(BF16), latp) + _dot(pn.astype(BF16), latn)) / denom).astype(BF16)
        outs = [_dot(ctx[i * t:(i + 1) * t], wuv_ref[h]) for i, h in enumerate(hs)]
        for i in range(0, group, 2):
            col = (g0 + i) * MLA_V
            o_ref[0, :, col:col + 2 * MLA_V] = jnp.concatenate([outs[i], outs[i + 1]], axis=1).astype(BF16)


def _latent_attention(q, lat_new, kr_new, cache_lat, cache_kr, j, w_uk_h, w_uv_h):
    b, t, _ = q.shape
    past = cache_lat.shape[2]
    return pl.pallas_call(
        functools.partial(_latent_attn_kernel, group=MLA_HEADS // 2),
        grid=(b,),
        in_specs=[pl.BlockSpec((1, t, MLA_HEADS * LANES), lambda i: (i, 0, 0)),
                  pl.BlockSpec((1, t, MLA_KV_LORA), lambda i: (i, 0, 0)),
                  pl.BlockSpec((1, t, LANES), lambda i: (i, 0, 0)),
                  pl.BlockSpec((None, 1, past, MLA_KV_LORA), lambda i: (j, i, 0, 0)),
                  pl.BlockSpec((None, 1, past, MLA_ROPE), lambda i: (j, i, 0, 0)),
                  _full(w_uk_h.shape), _full(w_uv_h.shape)],
        out_specs=pl.BlockSpec((1, t, MLA_HEADS * MLA_V), lambda i: (i, 0, 0)),
        out_shape=jax.ShapeDtypeStruct((b, t, MLA_HEADS * MLA_V), BF16),
        compiler_params=_params("parallel"),
    )(q, lat_new, kr_new, cache_lat, cache_kr, w_uk_h, w_uv_h)


def _layernorm(r, g, b):
    mu = jnp.mean(r, -1, keepdims=True)
    d = r - mu
    var = jnp.mean(d * d, -1, keepdims=True)
    return d * lax.rsqrt(var + LN_EPS) * g + b


def _mixer_out_kernel(a_ref, wo_ref, x_ref, gm_ref, lng_ref, lnb_ref, sc_ref, sh_ref, wr_ref, br_ref, up_ref, lo_ref,
                      xo_ref, h_ref, gate_ref, pos_ref, cnt_ref):
    y = _dot(a_ref[...], wo_ref[...])
    r = DEEPNORM_ALPHA * x_ref[...] + _rows_scale(y, 1.0 + gm_ref[...])
    xn = _layernorm(r, lng_ref[...], lnb_ref[...])
    xo_ref[...] = xn
    h = _modulate(xn, sc_ref[...], sh_ref[...])
    h_ref[...] = h.astype(BF16)

    logits = _dot3(wr_ref[...], h, _dot_nt) + br_ref[...]
    tm = logits.shape[1]
    eio = lax.broadcasted_iota(I32, (N_EXPERTS, tm), 0).astype(F32)
    sels, vals = [], []
    work = logits
    for k in range(TOP_K):
        m = jnp.max(work, axis=0, keepdims=True)
        ik = jnp.min(jnp.where(work == m, eio, float(N_EXPERTS)), axis=0, keepdims=True)
        sel = eio == ik
        work = jnp.where(sel, -jnp.inf, work)
        sels.append(sel)
        vals.append(m)
    es = [jnp.exp(v - vals[0]) for v in vals]
    tot = es[0] + es[1] + es[2] + es[3]
    for k in range(TOP_K):
        gate_ref[pl.ds(k, 1), :] = es[k] / tot
    multi = sels[0] | sels[1] | sels[2] | sels[3]
    mh = jnp.where(multi, 1.0, 0.0)
    before = _dot(mh.astype(BF16), up_ref[...])
    cnt = jnp.sum(mh, axis=1, keepdims=True)
    seg = jnp.floor((cnt + (SEG - 1)) * (1.0 / SEG)) * SEG
    seg_b = jnp.broadcast_to(seg, (N_EXPERTS, LANES)).astype(BF16)
    place = before + _dot(lo_ref[...], seg_b)[:, :1]
    for k in range(TOP_K):
        pos_ref[pl.ds(k, 1), :] = jnp.sum(jnp.where(sels[k], place, 0.0), axis=0, keepdims=True).astype(I32)
    cnt_ref[0] = cnt


def _mixer_out(a, w_o, x, gm, lng, lnb, sc, sh, w_r_t, b_r, n_seq, seq_len):
    n, kdim = a.shape
    tm = _row_tile(n_seq, seq_len, ROUTE_TILE)
    upper = (jnp.arange(tm)[:, None] < jnp.arange(tm)[None, :]).astype(BF16)
    lower_e = (jnp.arange(N_EXPERTS)[None, :] < jnp.arange(N_EXPERTS)[:, None]).astype(BF16)
    row = lambda w_: pl.BlockSpec((tm, w_), lambda i: (i, 0))
    col = pl.BlockSpec((TOP_K, tm), lambda i: (0, i))
    ss = _seq_spec(seq_len, tm)
    return pl.pallas_call(
        _mixer_out_kernel,
        grid=(n // tm,),
        in_specs=[row(kdim), _full(w_o.shape), row(D_MODEL), ss, _full((1, D_MODEL)), _full((1, D_MODEL)), ss, ss,
                  _full((N_EXPERTS, D_MODEL)), _full((N_EXPERTS, 1)), _full((tm, tm)),
                  _full((N_EXPERTS, N_EXPERTS))],
        out_specs=[row(D_MODEL), row(D_MODEL), col, col, pl.BlockSpec((1, N_EXPERTS, 1), lambda i: (i, 0, 0))],
        out_shape=[jax.ShapeDtypeStruct((n, D_MODEL), F32), jax.ShapeDtypeStruct((n, D_MODEL), BF16),
                   jax.ShapeDtypeStruct((TOP_K, n), F32), jax.ShapeDtypeStruct((TOP_K, n), I32),
                   jax.ShapeDtypeStruct((n // tm, N_EXPERTS, 1), F32)],
        compiler_params=_params("parallel"),
    )(a, w_o, x, gm, lng, lnb, sc, sh, w_r_t, b_r, upper, lower_e)


def _piece_sizes(tm):
    sizes, size = [], SEG
    while size <= tm:
        sizes.append(size)
        size *= 2
    assert sizes[-1] == tm
    return tuple(reversed(sizes))


def _segment_copies(seg_ref, off_ref, start_ref, tile, sizes, make):
    def body(e, carry):
        n = seg_ref[tile * N_EXPERTS + e]
        off = off_ref[tile * N_EXPERTS + e]
        start = start_ref[tile * N_EXPERTS + e]
        for size in sizes:
            done = n & (-2 * size)

            @pl.when((n & size) != 0)
            def _():
                make(pl.multiple_of(off + done, SEG), pl.multiple_of(start + done, SEG), size)
        return carry

    lax.fori_loop(0, N_EXPERTS, body, 0)


def _dispatch_kernel(seg_ref, off_ref, start_ref, h_ref, pos_ref, zero_ref, xr_ref, buf_ref, sem, *, nt, sizes):
    del zero_ref
    i = pl.program_id(0)
    slot = i % 2

    def copies(tile, sl, wait):
        def make(buf_row, hbm_row, size):
            cp = pltpu.make_async_copy(buf_ref.at[sl, pl.ds(buf_row, size)], xr_ref.at[pl.ds(hbm_row, size)],
                                       sem.at[sl])
            cp.wait() if wait else cp.start()
        _segment_copies(seg_ref, off_ref, start_ref, tile, sizes, make)

    @pl.when(i >= 2)
    def _():
        copies(i - 2, slot, True)

    pos = pos_ref[...]
    rows = buf_ref.shape[1]
    rio = lax.broadcasted_iota(I32, (rows, pos.shape[1]), 0)
    sel = (rio == pos[0:1]) | (rio == pos[1:2]) | (rio == pos[2:3]) | (rio == pos[3:4])
    buf_ref[slot] = _dot(jnp.where(sel, 1.0, 0.0).astype(BF16), h_ref[...]).astype(BF16)
    copies(i, slot, False)

    @pl.when(i == nt - 1)
    def _():
        copies(i, slot, True)

        @pl.when(i >= 1)
        def _():
            copies(i - 1, 1 - slot, True)


def _dispatch(h, pos_t, seg, off, start, x_rows, tm):
    n = h.shape[0]
    nt = n // tm
    sizes = _piece_sizes(tm)
    rows = TOP_K * tm + N_EXPERTS * SEG
    grid_spec = pltpu.PrefetchScalarGridSpec(
        num_scalar_prefetch=3,
        grid=(nt,),
        in_specs=[pl.BlockSpec((tm, D_MODEL), lambda i, *_: (i, 0)),
                  pl.BlockSpec((TOP_K, tm), lambda i, *_: (0, i)),
                  pl.BlockSpec(memory_space=pl.ANY)],
        out_specs=pl.BlockSpec(memory_space=pl.ANY),
        scratch_shapes=[pltpu.VMEM((2, rows, D_MODEL), BF16), pltpu.SemaphoreType.DMA((2,))],
    )
    return pl.pallas_call(
        functools.partial(_dispatch_kernel, nt=nt, sizes=sizes),
        grid_spec=grid_spec,
        out_shape=jax.ShapeDtypeStruct(x_rows.shape, BF16),
        input_output_aliases={5: 0},
        compiler_params=_params("arbitrary"),
    )(seg, off, start, h, pos_t, x_rows)


def _expert_kernel(be_ref, na_ref, x_ref, wgu_ref, bgu_ref, wd_ref, bd_ref, y_ref, wgu_s, wd_s):
    i = pl.program_id(0)

    @pl.when((i == 0) | (be_ref[i] != be_ref[jnp.maximum(i - 1, 0)]))
    def _():
        wgu_s[...] = wgu_ref[0].astype(BF16)
        wd_s[...] = wd_ref[0].astype(BF16)

    @pl.when(i < na_ref[0])
    def _():
        gu = _dot(x_ref[...], wgu_s[...]) + bgu_ref[0]
        gate = jnp.minimum(gu[:, :D_FF], SWIGLU_LIMIT)
        up = jnp.clip(gu[:, D_FF:], -SWIGLU_LIMIT, SWIGLU_LIMIT)
        act = (up + 1.0) * gate * _sigmoid(SWIGLU_ALPHA * gate)
        y_ref[...] = (_dot(act.astype(BF16), wd_s[...]) + bd_ref[0]).astype(BF16)

    @pl.when(i >= na_ref[0])
    def _():
        y_ref[...] = jnp.zeros_like(y_ref)


def _experts(x_rows, blk_e, n_act, w_gu, b_gu, w_down, b_down, layer):
    p = x_rows.shape[0]
    grid_spec = pltpu.PrefetchScalarGridSpec(
        num_scalar_prefetch=2,
        grid=(p // MOE_ROWS,),
        in_specs=[pl.BlockSpec((MOE_ROWS, D_MODEL), lambda i, be, na: (i, 0)),
                  pl.BlockSpec((None, 1, D_MODEL, 2 * D_FF), lambda i, be, na: (layer, be[i], 0, 0)),
                  pl.BlockSpec((1, 1, 2 * D_FF), lambda i, be, na: (be[i], 0, 0)),
                  pl.BlockSpec((None, 1, D_FF, D_MODEL), lambda i, be, na: (layer, be[i], 0, 0)),
                  pl.BlockSpec((1, 1, D_MODEL), lambda i, be, na: (be[i], 0, 0))],
        out_specs=pl.BlockSpec((MOE_ROWS, D_MODEL), lambda i, be, na: (i, 0)),
        scratch_shapes=[pltpu.VMEM((D_MODEL, 2 * D_FF), BF16), pltpu.VMEM((D_FF, D_MODEL), BF16)],
    )
    return pl.pallas_call(
        _expert_kernel,
        grid_spec=grid_spec,
        out_shape=jax.ShapeDtypeStruct((p, D_MODEL), BF16),
        compiler_params=_params("arbitrary"),
    )(blk_e, n_act, x_rows, w_gu, b_gu, w_down, b_down)


def _combine_kernel(seg_ref, off_ref, start_ref, y_ref, pos_ref, g_ref, x_ref, gf_ref, lng_ref, lnb_ref, o_ref,
                    buf_ref, sem, *, nt, sizes):
    i = pl.program_id(0)
    slot = i % 2

    def copies(tile, sl, wait):
        def make(buf_row, hbm_row, size):
            cp = pltpu.make_async_copy(y_ref.at[pl.ds(hbm_row, size)], buf_ref.at[sl, pl.ds(buf_row, size)],
                                       sem.at[sl])
            cp.wait() if wait else cp.start()
        _segment_copies(seg_ref, off_ref, start_ref, tile, sizes, make)

    @pl.when(i == 0)
    def _():
        buf_ref[...] = jnp.zeros_like(buf_ref)
        copies(0, 0, False)

    @pl.when(i + 1 < nt)
    def _():
        copies(i + 1, 1 - slot, False)

    copies(i, slot, True)
    pos = pos_ref[...]
    g = g_ref[...]
    cio = lax.broadcasted_iota(I32, (pos.shape[0], buf_ref.shape[1]), 1)
    gm = jnp.where(cio == pos[:, 0:1], g[:, 0:1], 0.0)
    for k in range(1, TOP_K):
        gm = gm + jnp.where(cio == pos[:, k:k + 1], g[:, k:k + 1], 0.0)
    y = _dot(gm.astype(BF16), buf_ref[slot])
    r = DEEPNORM_ALPHA * x_ref[...] + _rows_scale(y, 1.0 + gf_ref[...])
    o_ref[...] = _layernorm(r, lng_ref[...], lnb_ref[...])


def _combine(y_rows, pos, gates, seg, off, start, x, gf, lng, lnb, seq_len, tm):
    n = x.shape[0]
    nt = n // tm
    sizes = _piece_sizes(tm)
    rows = TOP_K * tm + N_EXPERTS * SEG
    row = pl.BlockSpec((tm, D_MODEL), lambda i, *_: (i, 0))
    four = pl.BlockSpec((tm, TOP_K), lambda i, *_: (i, 0))
    grid_spec = pltpu.PrefetchScalarGridSpec(
        num_scalar_prefetch=3,
        grid=(nt,),
        in_specs=[pl.BlockSpec(memory_space=pl.ANY), four, four, row, _seq_spec(seq_len, tm),
                  _full((1, D_MODEL)), _full((1, D_MODEL))],
        out_specs=row,
        scratch_shapes=[pltpu.VMEM((2, rows, D_MODEL), BF16), pltpu.SemaphoreType.DMA((2,))],
    )
    return pl.pallas_call(
        functools.partial(_combine_kernel, nt=nt, sizes=sizes),
        grid_spec=grid_spec,
        out_shape=jax.ShapeDtypeStruct((n, D_MODEL), F32),
        compiler_params=_params("arbitrary"),
    )(seg, off, start, y_rows, pos, gates, x, gf, lng, lnb)


def _moe_rows(groups):
    worst = N_EXPERTS * (MOE_ROWS - 1)
    for g in groups:
        n = g['n_seq'] * g['seq_len']
        worst += n * TOP_K + n // _row_tile(g['n_seq'], g['seq_len'], ROUTE_TILE) * N_EXPERTS * (SEG - 1)
    return (worst + MOE_ROWS - 1) // MOE_ROWS * MOE_ROWS


def _moe_and_norm(routed, lng, lnb, w, x_rows):
    tms = [_row_tile(r['n_seq'], r['seq_len'], ROUTE_TILE) for r in routed]
    counts = jnp.concatenate([r['cnt'][:, :, 0] for r in routed], axis=0).astype(I32)
    seg = (counts + SEG - 1) // SEG * SEG
    padded = (jnp.sum(seg, axis=0) + MOE_ROWS - 1) // MOE_ROWS * MOE_ROWS
    pad_end = jnp.cumsum(padded)
    start = (pad_end - padded)[None, :] + jnp.cumsum(seg, axis=0) - seg
    off = jnp.cumsum(seg, axis=1) - seg
    tables, t0 = [], 0
    for r in routed:
        nt = r['cnt'].shape[0]
        tables.append(tuple(a[t0:t0 + nt].reshape(-1) for a in (seg, off, start)))
        t0 += nt
    for r, tm, tab in zip(routed, tms, tables):
        x_rows = _dispatch(r['h'], r['pos_t'], *tab, x_rows, tm)
    nb = x_rows.shape[0] // MOE_ROWS
    blk_row = jnp.arange(nb, dtype=I32) * MOE_ROWS
    blk_e = jnp.minimum(jnp.sum(pad_end[None, :] <= blk_row[:, None], axis=1), N_EXPERTS - 1).astype(I32)
    n_act = (pad_end[-1:] // MOE_ROWS).astype(I32)
    y_rows = _experts(x_rows, blk_e, n_act, w['w_gu'], w['b_gu'], w['w_down'], w['b_down'], w['layer'])
    outs = [_combine(y_rows, r['pos_t'].T, r['gate_t'].T, *tab, r['x'], r['gf'], lng, lnb, r['seq_len'], tm)
            for r, tm, tab in zip(routed, tms, tables)]
    return outs, x_rows


def _gdn_gates_kernel(x_ref, sc_ref, sh_ref, wz_ref, wab_ref, wabt_ref, alog_ref, dtb_ref, alogt_ref, dtbt_ref,
                      tri_ref, trit_ref, z_ref, gc_ref, beta_ref, gct_ref):
    h = _modulate(x_ref[...], sc_ref[...], sh_ref[...]).astype(BF16)
    z_ref[...] = _dot(h, wz_ref[...]).astype(BF16)
    hv = GDN_V_HEADS
    ab = _dot(h, wab_ref[...])
    beta_ref[...] = _sigmoid(ab[:, :hv])
    g = -jnp.exp(alog_ref[...]) * _softplus(ab[:, hv:] + dtb_ref[...])
    gc_ref[...] = _dot_exact_rhs_lhs(tri_ref[...], g)
    abt = _dot_nt(wabt_ref[...], h)
    gt = -jnp.exp(alogt_ref[...]) * _softplus(abt + dtbt_ref[...])
    gct_ref[...] = _dot_exact_rhs(gt, trit_ref[...])


def _dot_exact_rhs_lhs(a01, b):
    b1 = b.astype(BF16)
    r1 = b - b1.astype(F32)
    b2 = r1.astype(BF16)
    b3 = (r1 - b2.astype(F32)).astype(BF16)
    return _dot(a01, b1) + (_dot(a01, b2) + _dot(a01, b3))


def _gdn_gates(x, sc, sh, w, n_seq, seq_len, chunk):
    n = x.shape[0]
    tm = _row_tile(n_seq, seq_len, 512)
    hv = GDN_V_HEADS
    r = jnp.arange(tm)
    tri = ((r[:, None] // chunk == r[None, :] // chunk) & (r[None, :] <= r[:, None])).astype(BF16)
    row = lambda w_: pl.BlockSpec((tm, w_), lambda i: (i, 0))
    return pl.pallas_call(
        _gdn_gates_kernel,
        grid=(n // tm,),
        in_specs=[row(D_MODEL), _seq_spec(seq_len, tm), _seq_spec(seq_len, tm), _full(w['w_z'].shape),
                  _full((D_MODEL, 2 * hv)), _full((hv, D_MODEL)), _full((1, hv)), _full((1, hv)),
                  _full((hv, 1)), _full((hv, 1)), _full((tm, tm)), _full((tm, tm))],
        out_specs=[row(GDN_V_DIM), row(hv), row(hv), pl.BlockSpec((hv, tm), lambda i: (0, i))],
        out_shape=[jax.ShapeDtypeStruct((n, GDN_V_DIM), BF16), jax.ShapeDtypeStruct((n, hv), F32),
                   jax.ShapeDtypeStruct((n, hv), F32), jax.ShapeDtypeStruct((hv, n), F32)],
        compiler_params=_params("parallel"),
    )(x, sc, sh, w['w_z'], w['w_ab'], w['w_a_t'], w['a_log'], w['dt_bias'], w['a_log'].T, w['dt_bias'].T,
      tri, tri.T)


def _qkv_conv_kernel(x_ref, xh_ref, sc_ref, sh_ref, w_ref, st_ref, cw_ref, o_ref, last_ref, h_s, *, per, n_q, n_qk):
    i = pl.program_id(0)
    j = pl.program_id(1)
    sc, sh = sc_ref[...], sh_ref[...]

    @pl.when(j == 0)
    def _():
        h_s[...] = _modulate(x_ref[...], sc, sh).astype(BF16)

    w = w_ref[...]
    x = _dot(h_s[...], w)
    tm = x.shape[0]
    prev = _dot(_modulate(xh_ref[...], sc, sh).astype(BF16), w)[HALO:]
    halo = jnp.where(i % per == 0, st_ref[0], prev)
    last_ref[0] = x[tm - HALO:]
    cw = cw_ref[...]
    rows = lax.broadcasted_iota(I32, (HALO, x.shape[1]), 0)
    acc = x * cw[GDN_CONV_W - 1:GDN_CONV_W]
    for s in range(1, GDN_CONV_W):
        xs = pltpu.roll(x, s, 0)
        head = jnp.where(rows < s, pltpu.roll(halo, s, 0), xs[:HALO])
        xs = jnp.concatenate([head, xs[HALO:]], axis=0) if tm > HALO else head
        acc = acc + xs * cw[GDN_CONV_W - 1 - s:GDN_CONV_W - s]
    y = _silu(acc)

    @pl.when(j < n_qk)
    def _():
        scale = jnp.where(j < n_q, GDN_DK ** -0.5, 1.0)
        for hd in range(y.shape[1] // GDN_DK):
            sl = slice(hd * GDN_DK, (hd + 1) * GDN_DK)
            yh = y[:, sl]
            o_ref[:, sl] = yh * (lax.rsqrt(jnp.sum(yh * yh, -1, keepdims=True) + L2_EPS) * scale)

    @pl.when(j >= n_qk)
    def _():
        o_ref[...] = y


def _gdn_qkv_conv(x, sc, sh, w_qkv, state8, conv_w, n_seq, seq_len):
    n = x.shape[0]
    assert seq_len >= 2 * HALO
    tm = _tile(seq_len, 1024)
    per = seq_len // tm
    tn = GDN_QK_DIM
    width = w_qkv.shape[1]
    kern = functools.partial(_qkv_conv_kernel, per=per, n_q=GDN_QK_DIM // tn, n_qk=2 * GDN_QK_DIM // tn)
    hb = tm // (2 * HALO)
    act, last = pl.pallas_call(
        kern,
        grid=(n // tm, width // tn),
        in_specs=[pl.BlockSpec((tm, D_MODEL), lambda i, j: (i, 0)),
                  pl.BlockSpec((2 * HALO, D_MODEL), lambda i, j: (jnp.maximum(i * hb - 1, 0), 0)),
                  _seq_spec(seq_len, tm), _seq_spec(seq_len, tm),
                  pl.BlockSpec((D_MODEL, tn), lambda i, j: (0, j)),
                  pl.BlockSpec((1, HALO, tn), lambda i, j: (i // per, 0, j)),
                  pl.BlockSpec((GDN_CONV_W, tn), lambda i, j: (0, j))],
        out_specs=[pl.BlockSpec((tm, tn), lambda i, j: (i, j)),
                   pl.BlockSpec((1, HALO, tn), lambda i, j: (i, 0, j))],
        out_shape=[jax.ShapeDtypeStruct((n, width), F32), jax.ShapeDtypeStruct((n // tm, HALO, width), F32)],
        scratch_shapes=[pltpu.VMEM((tm, D_MODEL), BF16)],
        compiler_params=_params("parallel", "arbitrary"),
    )(x, x, sc, sh, w_qkv, state8, conv_w)
    return act, last.reshape(n_seq, per, HALO, width)[:, -1]


def _inv_masks(c):
    row = lax.broadcasted_iota(I32, (c, c), 0)
    col = lax.broadcasted_iota(I32, (c, c), 1)
    eye = jnp.where(row == col, 1.0, 0.0)
    diag = row // INV_BASE == col // INV_BASE
    offs = []
    size = INV_BASE
    while size < c:
        offs.append((row // (2 * size) == col // (2 * size)) & (row // size % 2 == 1) & (col // size % 2 == 0))
        size *= 2
    return eye, diag, offs


def _inv_unit_lower(lmats, masks):
    eye, diag, offs = masks
    pws = [jnp.where(diag, lm, 0.0) for lm in lmats]
    ts = [eye - pw for pw in pws]
    size = 2
    while size < INV_BASE:
        pws = [_dotb(pw, pw) for pw in pws]
        ts = [t + _dotb(t, pw) for t, pw in zip(ts, pws)]
        size *= 2
    for off in offs:
        tbs = [_dotb(t, jnp.where(off, lm, 0.0)) for t, lm in zip(ts, lmats)]
        ts = [t - _dotb(tb, t) for t, tb in zip(ts, tbs)]
    return ts


def _gdn_core_kernel(q_ref, k_ref, v_ref, z_ref, gc_ref, beta_ref, gct_ref, s0_ref, nw_ref, o_ref, s_ref, *, c, hb):
    @pl.when(pl.program_id(2) == 0)
    def _():
        s_ref[...] = s0_ref[...]

    row = lax.broadcasted_iota(I32, (c, c), 0)
    col = lax.broadcasted_iota(I32, (c, c), 1)
    lower = col <= row
    strict = col < row
    masks = _inv_masks(c)
    heads = range(hb)
    ksl = [slice(kh * GDN_DK, (kh + 1) * GDN_DK) for kh in range(hb // 2)]
    vsl = [slice(h * GDN_DV, (h + 1) * GDN_DV) for h in heads]
    qn = [q_ref[0, :, sl] for sl in ksl]
    kn = [k_ref[0, :, sl] for sl in ksl]
    kb = [k.astype(BF16) for k in kn]
    kk = [_dot_nt(k, k) for k in kb]
    qk = [_dot_nt(q.astype(BF16), k) for q, k in zip(qn, kb)]
    gcc = [gc_ref[0, 0, :, h:h + 1] for h in heads]
    bc = [beta_ref[0, 0, :, h:h + 1] for h in heads]
    decay = [jnp.where(lower, jnp.exp(jnp.minimum(gcc[h] - gct_ref[0, 0, h:h + 1, :], 0.0)), 0.0) for h in heads]
    lmat = [jnp.where(strict, kk[h // 2] * decay[h], 0.0) * bc[h] for h in heads]
    ts = _inv_unit_lower(lmat, masks)
    egc = [jnp.exp(g) for g in gcc]
    rhs = [jnp.concatenate([v_ref[0, :, vsl[h]] * bc[h], kn[h // 2] * (bc[h] * egc[h])], axis=1).astype(BF16)
           for h in heads]
    uw = [_dot(ts[h].astype(BF16), rhs[h]) for h in heads]
    s_old = [s_ref[0, h] for h in heads]
    sb = [s.astype(BF16) for s in s_old]
    ws = [_dot(uw[h][:, GDN_DV:].astype(BF16), sb[h]) for h in heads]
    qs = [_dot((qn[h // 2] * egc[h]).astype(BF16), sb[h]) for h in heads]
    vb = [(uw[h][:, :GDN_DV] - ws[h]).astype(BF16) for h in heads]
    o = [qs[h] + _dot((qk[h // 2] * decay[h]).astype(BF16), vb[h]) for h in heads]
    glast = [g[c - 1:c, :] for g in gcc]
    kd = [(kn[h // 2] * jnp.exp(glast[h] - gcc[h])).astype(BF16) for h in heads]
    s_new = [s_old[h] * jnp.exp(glast[h]) + _dot_tn(kd[h], vb[h]) for h in heads]
    for h in heads:
        s_ref[0, h] = s_new[h]
        z = z_ref[0, :, vsl[h]].astype(F32)
        on = o[h] * lax.rsqrt(jnp.mean(o[h] * o[h], -1, keepdims=True) + RMS_EPS) * nw_ref[...]
        o_ref[0, :, vsl[h]] = (on * _silu(z)).astype(BF16)


def _gdn_core(qkv, z, gc, beta, gct, s0, norm_w, c, hb=GDN_V_HEADS):
    b, t, _ = z.shape
    ng = GDN_V_HEADS // hb
    kw = hb // 2 * GDN_DK
    vw = hb * GDN_DV
    gc_g = gc.reshape(b, t, ng, hb).transpose(0, 2, 1, 3)
    beta_g = beta.reshape(b, t, ng, hb).transpose(0, 2, 1, 3)
    gct_g = gct.reshape(b, ng, hb, t)
    kern = functools.partial(_gdn_core_kernel, c=c, hb=hb)
    nkb = GDN_QK_DIM // kw
    nvb = 2 * GDN_QK_DIM // vw
    return pl.pallas_call(
        kern,
        grid=(b, ng, t // c),
        in_specs=[pl.BlockSpec((1, c, kw), lambda b_, g, i: (b_, i, g)),
                  pl.BlockSpec((1, c, kw), lambda b_, g, i: (b_, i, nkb + g)),
                  pl.BlockSpec((1, c, vw), lambda b_, g, i: (b_, i, nvb + g)),
                  pl.BlockSpec((1, c, vw), lambda b_, g, i: (b_, i, g)),
                  pl.BlockSpec((1, 1, c, hb), lambda b_, g, i: (b_, g, i, 0)),
                  pl.BlockSpec((1, 1, c, hb), lambda b_, g, i: (b_, g, i, 0)),
                  pl.BlockSpec((1, 1, hb, c), lambda b_, g, i: (b_, g, 0, i)),
                  pl.BlockSpec((1, hb, GDN_DK, GDN_DV), lambda b_, g, i: (b_, g, 0, 0)),
                  _full((1, GDN_DV))],
        out_specs=[pl.BlockSpec((1, c, vw), lambda b_, g, i: (b_, i, g)),
                   pl.BlockSpec((1, hb, GDN_DK, GDN_DV), lambda b_, g, i: (b_, g, 0, 0))],
        out_shape=[jax.ShapeDtypeStruct((b, t, GDN_V_DIM), BF16),
                   jax.ShapeDtypeStruct((b, GDN_V_HEADS, GDN_DK, GDN_DV), F32)],
        compiler_params=_params("parallel", "parallel", "arbitrary"),
    )(qkv, qkv, qkv, z, gc_g, beta_g, gct_g, s0, norm_w)


def _prep_weights(p):
    f = {}
    pad_heads = lambda w_, dh: jnp.pad(w_.reshape(w_.shape[0], w_.shape[1], MLA_HEADS, dh),
                                       ((0, 0), (0, 0), (0, 0), (0, LANES - dh))).reshape(
                                           w_.shape[0], w_.shape[1], MLA_HEADS * LANES)
    w_in = p['mla_w_in']
    nl = w_in.shape[0]
    lo = MLA_Q_LORA + MLA_KV_LORA
    z = lambda k: jnp.zeros((nl, D_MODEL, k), F32)
    f['mla_w_in'] = jnp.concatenate([w_in[..., :lo], z(MLA_NOPE), w_in[..., lo:], z(LANES - MLA_QK)], -1).astype(BF16)
    f['mla_w_uq'] = pad_heads(p['mla_w_uq'], MLA_QK).astype(BF16)
    f['mla_w_uk'] = pad_heads(p['mla_w_uk'], MLA_NOPE).astype(BF16)
    f['mla_w_uv'] = pad_heads(p['mla_w_uv'], MLA_V).astype(BF16)
    f['mla_w_o'] = p['mla_w_o'].astype(BF16)
    per_head = lambda w_, dh: w_.reshape(w_.shape[0], w_.shape[1], MLA_HEADS, dh)
    f['mla_w_uk_t'] = per_head(p['mla_w_uk'], MLA_NOPE).transpose(0, 2, 3, 1).astype(BF16)
    f['mla_w_uv_h'] = per_head(p['mla_w_uv'], MLA_V).transpose(0, 2, 1, 3).astype(BF16)
    f['v_one'] = jnp.tile((jnp.arange(LANES) == MLA_V).astype(F32), MLA_HEADS)[None]
    g_in = p['gdn_w_in']
    f['gdn_w_qkv'] = g_in[..., :GDN_CONV_DIM].astype(BF16)
    f['gdn_w_z'] = g_in[..., GDN_CONV_DIM:GDN_CONV_DIM + GDN_V_DIM].astype(BF16)
    f['gdn_w_ab'] = g_in[..., GDN_CONV_DIM + GDN_V_DIM:].astype(BF16)
    f['gdn_w_a_t'] = jnp.swapaxes(g_in[..., GDN_CONV_DIM + GDN_V_DIM + GDN_V_HEADS:], 1, 2).astype(BF16)
    f['gdn_w_o'] = p['gdn_w_o'].astype(BF16)
    f['moe_w_r_t'] = jnp.swapaxes(p['moe_w_router'], 1, 2)
    return f


def _rope_tables(pos):
    half = MLA_ROPE // 2
    inv_freq = ROPE_BASE ** (-jnp.arange(half, dtype=F32) / half)
    ang = pos.astype(F32)[:, None] * inv_freq[None, :]
    cos, sin = jnp.cos(ang), jnp.sin(ang)
    t = pos.shape[0]
    z = lambda k: jnp.zeros((t, k), F32)
    cos_t = jnp.concatenate([jnp.ones((t, MLA_NOPE), F32), cos, cos, z(LANES - MLA_QK)], -1)
    sinp_t = jnp.concatenate([z(MLA_NOPE + half), sin, z(LANES - MLA_QK)], -1)
    sinm_t = jnp.concatenate([z(MLA_NOPE), -sin, z(half + LANES - MLA_QK)], -1)
    return cos_t, sinp_t, sinm_t


def _mixer(g, layer, mods, p, f):
    x, bsz, t, past = g['x'], g['n_seq'], g['seq_len'], g['past']
    sh_m, sc_m = mods[0], mods[1]
    n = bsz * t
    j = layer // 2
    if layer % 2 == 0:
        w = dict(w_in=f['mla_w_in'][j], q_norm=p['mla_q_norm'][j][None], kv_norm=p['mla_kv_norm'][j][None],
                 w_uq=f['mla_w_uq'][j], w_uk=f['mla_w_uk'][j], w_uv=f['mla_w_uv'][j], v_one=f['v_one'])
        q, lat, krt = _mla_proj(x, sc_m, sh_m, w, g['tabs'], bsz, t)
        g['lats'].append(lat.reshape(bsz, t, MLA_KV_LORA))
        g['krs'].append(krt[:, MLA_NOPE:MLA_QK].reshape(bsz, t, MLA_ROPE))
        wide = MLA_HEADS * LANES
        if past is None:
            k, v = _kv_expand(lat, krt, w)
            tq = _tile(t, 512)
            ctx = _attention(q.reshape(bsz, t, wide), k.reshape(bsz, t, wide), v.reshape(bsz, t, wide), tq, tq)
        else:
            ctx = _latent_attention(q.reshape(bsz, t, wide), lat.reshape(bsz, t, -1), krt.reshape(bsz, t, -1),
                                    past[0], past[1], j, f['mla_w_uk_t'][j], f['mla_w_uv_h'][j])
        return ctx.reshape(n, MLA_HEADS * MLA_V), f['mla_w_o'][j]
    chunk = _tile(t, 128)
    w = dict(w_z=f['gdn_w_z'][j], w_ab=f['gdn_w_ab'][j], w_a_t=f['gdn_w_a_t'][j], a_log=p['gdn_a_log'][j][None],
             dt_bias=p['gdn_dt_bias'][j][None])
    z, gc, beta, gct = _gdn_gates(x, sc_m, sh_m, w, bsz, t, chunk)
    if past is None:
        conv_state = jnp.zeros((bsz, GDN_CONV_W - 1, GDN_CONV_DIM), F32)
        s0 = jnp.zeros((bsz, GDN_V_HEADS, GDN_DK, GDN_DV), F32)
    else:
        conv_state, s0 = past[2][j], past[3][j]
    state8 = jnp.pad(conv_state, ((0, 0), (HALO - (GDN_CONV_W - 1), 0), (0, 0)))
    qkv, last = _gdn_qkv_conv(x, sc_m, sh_m, f['gdn_w_qkv'][j], state8, p['gdn_conv_w'][j], bsz, t)
    g['convs'].append(last[:, HALO - (GDN_CONV_W - 1):])
    o, s_new = _gdn_core(qkv.reshape(bsz, t, -1), z.reshape(bsz, t, -1), gc.reshape(bsz, t, -1),
                         beta.reshape(bsz, t, -1), gct.reshape(GDN_V_HEADS, bsz, t).transpose(1, 0, 2), s0,
                         p['gdn_norm'][j][None], chunk)
    g['ssms'].append(s_new)
    return o.reshape(n, GDN_V_DIM), f['gdn_w_o'][j]


def _forward(groups, p, f):
    mods = _cond(jnp.concatenate([g['c'] for g in groups], axis=0), p['w_cond'], p['b_cond'])
    seq0 = 0
    for g in groups:
        bsz, t, _ = g['x'].shape
        g.update(n_seq=bsz, seq_len=t, x=g['x'].reshape(bsz * t, D_MODEL), lats=[], krs=[], convs=[], ssms=[],
                 mod=mods[:, seq0:seq0 + bsz].reshape(DEPTH, bsz, 6, 1, D_MODEL))
        seq0 += bsz
        tabs = _rope_tables(g['pos'])
        tm_rope = _row_tile(bsz, t, 512)
        g['tabs'] = tuple(jnp.tile(tb, (tm_rope // t, 1)) for tb in tabs) if tm_rope > t else tabs
    x_rows = jnp.zeros((_moe_rows(groups), D_MODEL), BF16)
    for layer in range(DEPTH):
        routed = []
        for g in groups:
            sh_m, sc_m, g_m, sh_f, sc_f, g_f = [g['mod'][layer, :, i] for i in range(6)]
            a, w_o = _mixer(g, layer, (sh_m, sc_m), p, f)
            x, h, gate_t, pos_t, cnt = _mixer_out(
                a, w_o, g['x'], g_m, p['ln1_g'][layer][None], p['ln1_b'][layer][None], sc_f, sh_f,
                f['moe_w_r_t'][layer], p['moe_b_router'][layer][:, None], g['n_seq'], g['seq_len'])
            routed.append(dict(x=x, h=h, gate_t=gate_t, pos_t=pos_t, cnt=cnt, gf=g_f, n_seq=g['n_seq'],
                               seq_len=g['seq_len']))
        wm = dict(w_gu=p['moe_w_gu'], b_gu=p['moe_b_gu'][layer][:, None], w_down=p['moe_w_down'],
                  b_down=p['moe_b_down'][layer][:, None], layer=layer)
        xs, x_rows = _moe_and_norm(routed, p['ln2_g'][layer][None], p['ln2_b'][layer][None], wm, x_rows)
        for g, x in zip(groups, xs):
            g['x'] = x
    return [(g['x'].reshape(g['n_seq'], g['seq_len'], D_MODEL), jnp.stack(g['lats']), jnp.stack(g['krs']),
             jnp.stack(g['convs']), jnp.stack(g['ssms'])) for g in groups]


def kernel(x_prompt, x_sample, c_prompt, c_sample, cache_mla_latent, cache_mla_krope, state_gdn_conv, state_gdn_ssm, w_cond, b_cond, ln1_g, ln1_b, ln2_g, ln2_b, mla_w_in, mla_q_norm, mla_kv_norm, mla_w_uq, mla_w_uk, mla_w_uv, mla_w_o, gdn_w_in, gdn_conv_w, gdn_a_log, gdn_dt_bias, gdn_norm, gdn_w_o, moe_w_router, moe_b_router, moe_w_gu, moe_b_gu, moe_w_down, moe_b_down):
    p = dict(w_cond=w_cond, b_cond=b_cond, ln1_g=ln1_g, ln1_b=ln1_b, ln2_g=ln2_g, ln2_b=ln2_b,
             mla_w_in=mla_w_in, mla_q_norm=mla_q_norm, mla_kv_norm=mla_kv_norm, mla_w_uq=mla_w_uq,
             mla_w_uk=mla_w_uk, mla_w_uv=mla_w_uv, mla_w_o=mla_w_o, gdn_w_in=gdn_w_in,
             gdn_conv_w=gdn_conv_w, gdn_a_log=gdn_a_log, gdn_dt_bias=gdn_dt_bias, gdn_norm=gdn_norm,
             gdn_w_o=gdn_w_o, moe_w_router=moe_w_router, moe_b_router=moe_b_router, moe_w_gu=moe_w_gu,
             moe_b_gu=moe_b_gu, moe_w_down=moe_w_down, moe_b_down=moe_b_down)
    f = _prep_weights(p)
    past_len = cache_mla_latent.shape[2]
    assert past_len % CHUNK == 0 and x_sample.shape[1] <= CHUNK
    past = (cache_mla_latent, cache_mla_krope, state_gdn_conv, state_gdn_ssm)
    prompt = dict(x=x_prompt, c=c_prompt, pos=jnp.arange(x_prompt.shape[1], dtype=I32), past=None)
    sample = dict(x=x_sample, c=c_sample, pos=past_len + jnp.arange(x_sample.shape[1], dtype=I32), past=past)
    (y_p, p_lat, p_kr, p_conv, p_ssm), (y_s, s_lat, s_kr, s_conv, s_ssm) = _forward([prompt, sample], p, f)
    return (y_p, y_s, p_lat, p_kr, p_conv, p_ssm, s_lat, s_kr, s_conv, s_ssm)
```
